```python
import math
import jax
import jax.numpy as jnp
from jax import lax
import numpy as np

D_MODEL = 2048
BATCH = 2
SEQ = 4096
DEPTH = 4
DEC_BATCH = 8
DEC_SEQ = 8
PAST_LEN = 16384
PAGE_SIZE = 128

N_MIXERS = 3
N_SSD_LAYERS = (DEPTH + 2) // N_MIXERS
N_NSA_LAYERS = (DEPTH + 1) // N_MIXERS
N_MOBA_LAYERS = DEPTH // N_MIXERS

D_FF = ((8 * D_MODEL // 3 + 255) // 256) * 256
RMS_EPS = 1e-6

D_INNER = 2 * D_MODEL
SSD_HEAD_DIM = 64
SSD_HEADS = D_INNER // SSD_HEAD_DIM
D_STATE = 128
SSD_GROUPS = 8
HEADS_PER_GROUP = SSD_HEADS // SSD_GROUPS
GN = SSD_GROUPS * D_STATE
CONV_W = 4
CONV_DIM = D_INNER + 2 * GN
SSD_IN_DIM = D_INNER + CONV_DIM + SSD_HEADS
SSD_CHUNK = 128

HEAD_DIM = 128
N_HEADS = D_MODEL // HEAD_DIM
N_KV_HEADS = 4
Q_PER_KV = N_HEADS // N_KV_HEADS
Q_DIM = N_HEADS * HEAD_DIM
KV_DIM = N_KV_HEADS * HEAD_DIM
Q_BLOCK = 64

CMP_BLOCK = 32
CMP_STRIDE = 16
SEL_BLOCK = 64
SEL_TOPK = 16
N_LOCAL_SEL = 2
WINDOW = 512
NSA_IN_DIM = Q_DIM + 6 * KV_DIM + 3 * N_HEADS

MOBA_BLOCK = 256
MOBA_TOPK = 3
MOBA_IN_DIM = Q_DIM + 2 * KV_DIM

REL_BUCKETS = 32
REL_MAX_EXACT = REL_BUCKETS // 2
REL_MAX_DIST = 128

NEG_INF = -1e30
FORCE_SCORE = 1e9

kernel_name = 'hybrid_ssd_nsa_moba_macaron_decode_step'


def rms_norm(x, w):
    xf = x.astype(jnp.float32)
    xf = xf * lax.rsqrt(jnp.mean(xf * xf, axis=-1, keepdims=True) + RMS_EPS)
    return xf.astype(x.dtype) * w


def swiglu(x, w_gate, w_up, w_down):
    return (jax.nn.silu(x @ w_gate) * (x @ w_up)) @ w_down


def half_ffn(h, g, w_gate, w_up, w_down):
    return h + 0.5 * swiglu(rms_norm(h, g), w_gate, w_up, w_down)


def masked_softmax(logits, mask):
    logits = jnp.where(mask, logits.astype(jnp.float32), NEG_INF)
    m = jnp.max(logits, axis=-1, keepdims=True)
    e = jnp.where(mask, jnp.exp(logits - m), 0.0)
    return e / jnp.maximum(jnp.sum(e, axis=-1, keepdims=True), 1e-30)


def rel_bucket(dist):
    n = jnp.maximum(dist, 0)
    log_ratio = jnp.log(jnp.maximum(n, 1).astype(jnp.float32) / REL_MAX_EXACT) / math.log(REL_MAX_DIST / REL_MAX_EXACT)
    large = jnp.minimum(REL_MAX_EXACT + (log_ratio * (REL_BUCKETS - REL_MAX_EXACT)).astype(jnp.int32), REL_BUCKETS - 1)
    return jnp.where(n < REL_MAX_EXACT, n, large)


def shared_bias(rel_bias, q_pos, k_pos):
    bias = rel_bias[rel_bucket(q_pos[:, None] - k_pos[None, :])]
    nq, nk = bias.shape[:2]
    return bias.reshape(nq, nk, N_KV_HEADS, Q_PER_KV).transpose(0, 2, 3, 1).astype(jnp.float32)


def gathered_bias(rel_bias, dist):
    table = rel_bias.T.reshape(N_KV_HEADS, Q_PER_KV, REL_BUCKETS)
    g_idx = jnp.arange(N_KV_HEADS).reshape(1, 1, -1, 1, 1, 1)
    r_idx = jnp.arange(Q_PER_KV).reshape(1, 1, 1, -1, 1, 1)
    return table[g_idx, r_idx, rel_bucket(dist)].astype(jnp.float32)


def gather_pages(pool, page_table, comp):
    rows = pool[page_table, :, comp]
    return rows.reshape(page_table.shape[0], page_table.shape[1] * pool.shape[1], N_KV_HEADS, HEAD_DIM)


def ssd_scan(x, dt, a, bmat, cmat, init_state):
    b, t, nh, p = x.shape
    l = min(SSD_CHUNK, t)
    nc = -(-t // l)
    pad = nc * l - t

    def padt(arr):
        return jnp.pad(arr.astype(jnp.float32), [(0, 0), (0, pad)] + [(0, 0)] * (arr.ndim - 2))

    xr = padt(x).reshape(b, nc, l, SSD_GROUPS, HEADS_PER_GROUP, p)
    dtr = padt(dt).reshape(b, nc, l, SSD_GROUPS, HEADS_PER_GROUP)
    br = padt(bmat).reshape(b, nc, l, SSD_GROUPS, D_STATE)
    cr = padt(cmat).reshape(b, nc, l, SSD_GROUPS, D_STATE)
    acum = jnp.cumsum(dtr * a.reshape(SSD_GROUPS, HEADS_PER_GROUP), axis=2)
    xdt = xr * dtr[..., None]
    causal = jnp.tril(jnp.ones((l, l), dtype=bool))[None, None, :, :, None, None]
    decay = jnp.exp(jnp.where(causal, acum[:, :, :, None] - acum[:, :, None, :], -jnp.inf))
    cb = jnp.einsum('bclgn,bcsgn->bclsg', cr, br)
    y_diag = jnp.einsum('bclsg,bclsgr,bcsgrp->bclgrp', cb, decay, xdt)
    chunk_states = jnp.einsum('bclgn,bclgr,bclgrp->bcgrpn', br, jnp.exp(acum[:, :, -1:] - acum), xdt)
    chunk_decay = jnp.exp(acum[:, :, -1])

    def step(s, inp):
        st, dc = inp
        return s * dc[..., None, None] + st, s

    s0 = init_state.astype(jnp.float32).reshape(b, SSD_GROUPS, HEADS_PER_GROUP, p, D_STATE)
    final, prev = lax.scan(step, s0, (jnp.moveaxis(chunk_states, 1, 0), jnp.moveaxis(chunk_decay, 1, 0)))
    prev = jnp.moveaxis(prev, 0, 1)
    y_off = jnp.einsum('bclgn,bcgrpn,bclgr->bclgrp', cr, prev, jnp.exp(acum))
    y = (y_diag + y_off).reshape(b, nc * l, nh, p)[:, :t]
    return y.astype(x.dtype), final.reshape(b, nh, p, D_STATE).astype(init_state.dtype)


def ssd_mixer(u, conv_state, ssm_state, w_in, conv_w, conv_b, dt_bias, a_log, d_skip, norm_w, w_out):
    b, t, _ = u.shape
    proj = u @ w_in
    z = proj[..., :D_INNER]
    xbc = proj[..., D_INNER:D_INNER + CONV_DIM]
    dt_raw = proj[..., D_INNER + CONV_DIM:]
    xbc_ext = jnp.concatenate([conv_state.astype(xbc.dtype), xbc], axis=1)
    conv = conv_b + sum(xbc_ext[:, k:k + t] * conv_w[k] for k in range(CONV_W))
    xbc = jax.nn.silu(conv)
    x = xbc[..., :D_INNER].reshape(b, t, SSD_HEADS, SSD_HEAD_DIM)
    bmat = xbc[..., D_INNER:D_INNER + GN].reshape(b, t, SSD_GROUPS, D_STATE)
    cmat = xbc[..., D_INNER + GN:].reshape(b, t, SSD_GROUPS, D_STATE)
    dt = jax.nn.softplus((dt_raw + dt_bias).astype(jnp.float32))
    a = -jnp.exp(a_log.astype(jnp.float32))
    y, new_ssm = ssd_scan(x, dt, a, bmat, cmat, ssm_state)
    y = (y + x * d_skip[:, None]).reshape(b, t, D_INNER) * jax.nn.silu(z)
    yg = y.reshape(b, t, SSD_GROUPS, D_INNER // SSD_GROUPS).astype(jnp.float32)
    yg = yg * lax.rsqrt(jnp.mean(yg * yg, axis=-1, keepdims=True) + RMS_EPS)
    y = yg.reshape(b, t, D_INNER).astype(u.dtype) * norm_w
    return y @ w_out, xbc_ext[:, xbc_ext.shape[1] - (CONV_W - 1):], new_ssm


def nsa_project(u, w_in):
    b, t, _ = u.shape
    proj = u @ w_in
    q = proj[..., :Q_DIM].reshape(b, t, N_KV_HEADS, Q_PER_KV, HEAD_DIM)
    kv = proj[..., Q_DIM:Q_DIM + 6 * KV_DIM].reshape(b, t, 6, N_KV_HEADS, HEAD_DIM)
    gates = jax.nn.sigmoid(proj[..., Q_DIM + 6 * KV_DIM:]).reshape(b, t, N_KV_HEADS, Q_PER_KV, 3)
    return q, kv, gates


def nsa_compress(k, pe, w):
    b, t = k.shape[:2]
    n_str = t // CMP_STRIDE
    ch = k[:, :n_str * CMP_STRIDE].reshape(b, n_str, CMP_STRIDE, N_KV_HEADS, HEAD_DIM)
    blk = jnp.concatenate([ch[:, :-1], ch[:, 1:]], axis=2) + pe[:, None, :]
    flat = blk.transpose(0, 1, 3, 2, 4).reshape(b, n_str - 1, N_KV_HEADS, CMP_BLOCK * HEAD_DIM)
    return flat @ w


def nsa_branch_keys(k_cmp, v_cmp, k_sel, v_sel, cmp_pe, cmp_w):
    b, t = k_cmp.shape[:2]
    kc = nsa_compress(k_cmp, cmp_pe[0], cmp_w[0])
    vc = nsa_compress(v_cmp, cmp_pe[1], cmp_w[1])
    c_end = jnp.arange(kc.shape[1], dtype=jnp.int32) * CMP_STRIDE + (CMP_BLOCK - 1)
    n_sel = -(-t // SEL_BLOCK)
    pad = ((0, 0), (0, n_sel * SEL_BLOCK - t), (0, 0), (0, 0))
    ks = jnp.pad(k_sel, pad).reshape(b, n_sel, SEL_BLOCK, N_KV_HEADS, HEAD_DIM)
    vs = jnp.pad(v_sel, pad).reshape(b, n_sel, SEL_BLOCK, N_KV_HEADS, HEAD_DIM)
    return kc, vc, c_end, ks, vs


def nsa_core(q, q_pos, kc, vc, c_end, ks_blk, vs_blk, kw, vw, w_pos, gates, rel_bias):
    bsz, nq = q.shape[:2]
    scale = HEAD_DIM ** -0.5
    tq = q_pos[:, None]
    lc = jnp.einsum('bqgrd,bkgd->bqgrk', q, kc) * scale + shared_bias(rel_bias, q_pos, c_end)
    pc = masked_softmax(lc, (c_end[None, :] <= tq)[None, :, None, None, :])
    o_cmp = jnp.einsum('bqgrk,bkgd->bqgrd', pc.astype(vc.dtype), vc)
    nc, ns = kc.shape[1], ks_blk.shape[1]
    ci = jnp.arange(nc)[:, None]
    sj = jnp.arange(ns)[None, :]
    overlap = ((ci * CMP_STRIDE < (sj + 1) * SEL_BLOCK) & (ci * CMP_STRIDE + CMP_BLOCK > sj * SEL_BLOCK)).astype(jnp.float32)
    imp = jnp.einsum('bqgk,kj->bqgj', jnp.sum(pc, axis=3), overlap)
    own = q_pos // SEL_BLOCK
    j = jnp.arange(ns)
    diff = own[:, None] - j[None, :]
    valid = diff >= 0
    forced = (j[None, :] == 0) | (valid & (diff < N_LOCAL_SEL))
    score = jnp.where((valid & forced)[None, :, None, :], FORCE_SCORE, jnp.where(valid[None, :, None, :], imp, NEG_INF))
    k_top = min(SEL_TOPK, ns)
    _, idx = lax.top_k(score, k_top)
    sel_ok = idx <= own[None, :, None, None]
    bi = jnp.arange(bsz).reshape(-1, 1, 1, 1)
    gi = jnp.arange(N_KV_HEADS).reshape(1, 1, -1, 1)
    kg = ks_blk[bi, idx, :, gi]
    vg = vs_blk[bi, idx, :, gi]
    pos = idx[..., None] * SEL_BLOCK + jnp.arange(SEL_BLOCK)
    tq5 = q_pos[None, :, None, None, None]
    ms = sel_ok[..., None] & (pos <= tq5)
    ls = jnp.einsum('bqgrd,bqgkcd->bqgrkc', q, kg) * scale + gathered_bias(rel_bias, (tq5 - pos)[:, :, :, None])
    ps = masked_softmax(ls.reshape(bsz, nq, N_KV_HEADS, Q_PER_KV, k_top * SEL_BLOCK),
                        ms[:, :, :, None].reshape(bsz, nq, N_KV_HEADS, 1, k_top * SEL_BLOCK))
    ps = ps.reshape(bsz, nq, N_KV_HEADS, Q_PER_KV, k_top, SEL_BLOCK).astype(vg.dtype)
    o_sel = jnp.einsum('bqgrkc,bqgkcd->bqgrd', ps, vg)
    lw = jnp.einsum('bqgrd,bkgd->bqgrk', q, kw) * scale + shared_bias(rel_bias, q_pos, w_pos)
    dw = tq - w_pos[None, :]
    mw = ((dw >= 0) & (dw < WINDOW) & (w_pos[None, :] >= 0))[None, :, None, None, :]
    o_win = jnp.einsum('bqgrk,bkgd->bqgrd', masked_softmax(lw, mw).astype(vw.dtype), vw)
    return gates[..., 0:1] * o_cmp + gates[..., 1:2] * o_sel + gates[..., 2:3] * o_win


def nsa_prompt(u, w_in, cmp_pe, cmp_w, w_out, rel_bias):
    b, t, _ = u.shape
    q, kv, gates = nsa_project(u, w_in)
    kc, vc, c_end, ks, vs = nsa_branch_keys(kv[:, :, 0], kv[:, :, 1], kv[:, :, 2], kv[:, :, 3], cmp_pe, cmp_w)
    wpad = ((0, 0), (WINDOW, 0), (0, 0), (0, 0))
    kw_pad, vw_pad = jnp.pad(kv[:, :, 4], wpad), jnp.pad(kv[:, :, 5], wpad)

    def block(i):
        s0 = i * Q_BLOCK
        q_pos = s0 + jnp.arange(Q_BLOCK, dtype=jnp.int32)
        w_pos = s0 - WINDOW + jnp.arange(WINDOW + Q_BLOCK, dtype=jnp.int32)
        qb = lax.dynamic_slice_in_dim(q, s0, Q_BLOCK, axis=1)
        gb = lax.dynamic_slice_in_dim(gates, s0, Q_BLOCK, axis=1)
        kw = lax.dynamic_slice_in_dim(kw_pad, s0, WINDOW + Q_BLOCK, axis=1)
        vw = lax.dynamic_slice_in_dim(vw_pad, s0, WINDOW + Q_BLOCK, axis=1)
        return nsa_core(qb, q_pos, kc, vc, c_end, ks, vs, kw, vw, w_pos, gb, rel_bias)

    o = lax.map(block, jnp.arange(t // Q_BLOCK, dtype=jnp.int32))
    o = jnp.moveaxis(o, 0, 1).reshape(b, t, Q_DIM)
    wb = min(WINDOW, t)
    return o @ w_out, kv[:, :, :4], kv[:, t - wb:, 4:]


def nsa_sample(u, cache_kv, win_buf, page_table, w_in, cmp_pe, cmp_w, w_out, rel_bias):
    b, t, _ = u.shape
    q, kv, gates = nsa_project(u, w_in)
    full = [jnp.concatenate([gather_pages(cache_kv, page_table, c).astype(kv.dtype), kv[:, :, c]], axis=1) for c in range(4)]
    kc, vc, c_end, ks, vs = nsa_branch_keys(full[0], full[1], full[2], full[3], cmp_pe, cmp_w)
    wb = win_buf.shape[1]
    win = jnp.concatenate([win_buf.astype(kv.dtype), kv[:, :, 4:]], axis=1)
    w_pos = PAST_LEN - wb + jnp.arange(wb + t, dtype=jnp.int32)
    q_pos = PAST_LEN + jnp.arange(t, dtype=jnp.int32)
    o = nsa_core(q, q_pos, kc, vc, c_end, ks, vs, win[:, :, 0], win[:, :, 1], w_pos, gates, rel_bias)
    return o.reshape(b, t, Q_DIM) @ w_out, kv[:, :, :4], win[:, t:]


def moba_project(u, w_in):
    b, t, _ = u.shape
    proj = u @ w_in
    q = proj[..., :Q_DIM].reshape(b, t, N_KV_HEADS, Q_PER_KV, HEAD_DIM)
    kv = proj[..., Q_DIM:].reshape(b, t, 2, N_KV_HEADS, HEAD_DIM)
    return q, kv


def moba_blocks(k, v):
    b, t = k.shape[:2]
    nf = t // MOBA_BLOCK
    kb = k[:, :nf * MOBA_BLOCK].reshape(b, nf, MOBA_BLOCK, N_KV_HEADS, HEAD_DIM)
    vb = v[:, :nf * MOBA_BLOCK].reshape(b, nf, MOBA_BLOCK, N_KV_HEADS, HEAD_DIM)
    return kb, vb, jnp.mean(kb.astype(jnp.float32), axis=2)


def moba_core(q, q_pos, kb, vb, kmean, own_k, own_v, own_pos, rel_bias):
    bsz, nq = q.shape[:2]
    scale = HEAD_DIM ** -0.5
    q_blk = q_pos // MOBA_BLOCK
    lo = jnp.einsum('bqgrd,bkgd->bqgrk', q, own_k) * scale + shared_bias(rel_bias, q_pos, own_pos)
    mo = (own_pos[None, :] <= q_pos[:, None]) & ((own_pos // MOBA_BLOCK)[None, :] == q_blk[:, None])
    mo = jnp.broadcast_to(mo[None, :, None, None, :], lo.shape)
    nf = kmean.shape[1]
    k_top = min(MOBA_TOPK, nf)
    if k_top == 0:
        po = masked_softmax(lo, mo).astype(own_v.dtype)
        return jnp.einsum('bqgrk,bkgd->bqgrd', po, own_v)
    gate = jnp.einsum('bqgrd,bngd->bqgrn', q.astype(jnp.float32), kmean)
    past_ok = (jnp.arange(nf)[None, :] < q_blk[:, None])[None, :, None, None, :]
    _, idx = lax.top_k(jnp.where(past_ok, gate, NEG_INF), k_top)
    sel_ok = idx < q_blk[None, :, None, None, None]
    bi = jnp.arange(bsz).reshape(-1, 1, 1, 1, 1)
    gi = jnp.arange(N_KV_HEADS).reshape(1, 1, -1, 1, 1)
    kg = kb[bi, idx, :, gi]
    vg = vb[bi, idx, :, gi]
    pos = idx[..., None] * MOBA_BLOCK + jnp.arange(MOBA_BLOCK)
    lp = jnp.einsum('bqgrd,bqgrkcd->bqgrkc', q, kg) * scale + gathered_bias(rel_bias, q_pos[None, :, None, None, None, None] - pos)
    mp = jnp.broadcast_to(sel_ok[..., None], lp.shape)
    n_past = k_top * MOBA_BLOCK
    logits = jnp.concatenate([lp.reshape(bsz, nq, N_KV_HEADS, Q_PER_KV, n_past), lo], axis=-1)
    mask = jnp.concatenate([mp.reshape(bsz, nq, N_KV_HEADS, Q_PER_KV, n_past), mo], axis=-1)
    p = masked_softmax(logits, mask).astype(vg.dtype)
    pp = p[..., :n_past].reshape(bsz, nq, N_KV_HEADS, Q_PER_KV, k_top, MOBA_BLOCK)
    return jnp.einsum('bqgrkc,bqgrkcd->bqgrd', pp, vg) + jnp.einsum('bqgrk,bkgd->bqgrd', p[..., n_past:], own_v)


def moba_prompt(u, w_in, w_out, rel_bias):
    b, t, _ = u.shape
    q, kv = moba_project(u, w_in)
    k, v = kv[:, :, 0], kv[:, :, 1]
    kb, vb, kmean = moba_blocks(k, v)
    n_blk = -(-t // MOBA_BLOCK)
    pad = ((0, 0), (0, n_blk * MOBA_BLOCK - t), (0, 0), (0, 0))
    k_pad, v_pad = jnp.pad(k, pad), jnp.pad(v, pad)

    def block(i):
        s0 = i * Q_BLOCK
        q_pos = s0 + jnp.arange(Q_BLOCK, dtype=jnp.int32)
        own0 = (s0 // MOBA_BLOCK) * MOBA_BLOCK
        own_pos = own0 + jnp.arange(MOBA_BLOCK, dtype=jnp.int32)
        qb = lax.dynamic_slice_in_dim(q, s0, Q_BLOCK, axis=1)
        ok = lax.dynamic_slice_in_dim(k_pad, own0, MOBA_BLOCK, axis=1)
        ov = lax.dynamic_slice_in_dim(v_pad, own0, MOBA_BLOCK, axis=1)
        return moba_core(qb, q_pos, kb, vb, kmean, ok, ov, own_pos, rel_bias)

    o = lax.map(block, jnp.arange(t // Q_BLOCK, dtype=jnp.int32))
    o = jnp.moveaxis(o, 0, 1).reshape(b, t, Q_DIM)
    return o @ w_out, kv


def moba_sample(u, cache_kv, page_table, w_in, w_out, rel_bias):
    b, t, _ = u.shape
    q, kv = moba_project(u, w_in)
    k = jnp.concatenate([gather_pages(cache_kv, page_table, 0).astype(kv.dtype), kv[:, :, 0]], axis=1)
    v = jnp.concatenate([gather_pages(cache_kv, page_table, 1).astype(kv.dtype), kv[:, :, 1]], axis=1)
    kb, vb, kmean = moba_blocks(k, v)
    own0 = (PAST_LEN // MOBA_BLOCK) * MOBA_BLOCK
    own_pos = own0 + jnp.arange(k.shape[1] - own0, dtype=jnp.int32)
    q_pos = PAST_LEN + jnp.arange(t, dtype=jnp.int32)
    o = moba_core(q, q_pos, kb, vb, kmean, k[:, own0:], v[:, own0:], own_pos, rel_bias)
    return o.reshape(b, t, Q_DIM) @ w_out, kv


def setup_inputs(seed: int = 0) -> dict:
    key = jax.random.key(seed)
    ks = jax.random.split(key, 28)

    def nrm(k, shape, scale):
        return jax.random.normal(k, shape, jnp.float32) * scale

    n_pages = PAST_LEN // PAGE_SIZE
    n_used = DEC_BATCH * n_pages
    n_pool = n_used + -(-n_used // 4)
    page_table = jax.random.permutation(ks[0], n_pool)[:n_used].reshape(DEC_BATCH, n_pages).astype(jnp.int32)
    win_buf = min(WINDOW, PAST_LEN)
    dt0 = jnp.exp(jax.random.uniform(ks[1], (N_SSD_LAYERS, SSD_HEADS), jnp.float32, math.log(1e-3), math.log(1e-1)))
    return {
        'x_prompt': nrm(ks[2], (BATCH, SEQ, D_MODEL), 1.0),
        'x_sample': nrm(ks[3], (DEC_BATCH, DEC_SEQ, D_MODEL), 1.0),
        'state_ssm': nrm(ks[4], (N_SSD_LAYERS, DEC_BATCH, SSD_HEADS, SSD_HEAD_DIM, D_STATE), 0.1),
        'state_conv': nrm(ks[5], (N_SSD_LAYERS, DEC_BATCH, CONV_W - 1, CONV_DIM), 1.0),
        'cache_nsa_kv': nrm(ks[6], (N_NSA_LAYERS, n_pool, PAGE_SIZE, 4, N_KV_HEADS, HEAD_DIM), 1.0),
        'cache_nsa_win': nrm(ks[7], (N_NSA_LAYERS, DEC_BATCH, win_buf, 2, N_KV_HEADS, HEAD_DIM), 1.0),
        'cache_moba_kv': nrm(ks[8], (N_MOBA_LAYERS, n_pool, PAGE_SIZE, 2, N_KV_HEADS, HEAD_DIM), 1.0),
        'page_table': page_table,
        'rel_bias': nrm(ks[9], (REL_BUCKETS, N_HEADS), 0.2),
        'norm_w': 1.0 + nrm(ks[10], (DEPTH, 3, D_MODEL), 0.02),
        'final_norm_w': 1.0 + nrm(ks[11], (D_MODEL,), 0.02),
        'ffn_w_gate': nrm(ks[12], (DEPTH, 2, D_MODEL, D_FF), D_MODEL ** -0.5),
        'ffn_w_up': nrm(ks[13], (DEPTH, 2, D_MODEL, D_FF), D_MODEL ** -0.5),
        'ffn_w_down': nrm(ks[14], (DEPTH, 2, D_FF, D_MODEL), D_FF ** -0.5),
        'ssd_w_in': nrm(ks[15], (N_SSD_LAYERS, D_MODEL, SSD_IN_DIM), D_MODEL ** -0.5),
        'ssd_conv_w': nrm(ks[16], (N_SSD_LAYERS, CONV_W, CONV_DIM), CONV_W ** -0.5),
        'ssd_conv_b': nrm(ks[17], (N_SSD_LAYERS, CONV_DIM), 0.02),
        'ssd_dt_bias': dt0 + jnp.log(-jnp.expm1(-dt0)),
        'ssd_a_log': jnp.log(jax.random.uniform(ks[18], (N_SSD_LAYERS, SSD_HEADS), jnp.float32, 1.0, 16.0)),
        'ssd_d': 1.0 + nrm(ks[19], (N_SSD_LAYERS, SSD_HEADS), 0.1),
        'ssd_norm_w': 1.0 + nrm(ks[20], (N_SSD_LAYERS, D_INNER), 0.02),
        'ssd_w_out': nrm(ks[21], (N_SSD_LAYERS, D_INNER, D_MODEL), D_INNER ** -0.5),
        'nsa_w_in': nrm(ks[22], (N_NSA_LAYERS, D_MODEL, NSA_IN_DIM), D_MODEL ** -0.5),
        'nsa_cmp_pe': nrm(ks[23], (N_NSA_LAYERS, 2, CMP_BLOCK, HEAD_DIM), 0.1),
        'nsa_cmp_w': nrm(ks[24], (N_NSA_LAYERS, 2, CMP_BLOCK * HEAD_DIM, HEAD_DIM), (CMP_BLOCK * HEAD_DIM) ** -0.5),
        'nsa_w_out': nrm(ks[25], (N_NSA_LAYERS, Q_DIM, D_MODEL), Q_DIM ** -0.5),
        'moba_w_in': nrm(ks[26], (N_MOBA_LAYERS, D_MODEL, MOBA_IN_DIM), D_MODEL ** -0.5),
        'moba_w_out': nrm(ks[27], (N_MOBA_LAYERS, Q_DIM, D_MODEL), Q_DIM ** -0.5),
    }


def reference(x_prompt, x_sample, state_ssm, state_conv, cache_nsa_kv, cache_nsa_win, cache_moba_kv, page_table,
              rel_bias, norm_w, final_norm_w, ffn_w_gate, ffn_w_up, ffn_w_down,
              ssd_w_in, ssd_conv_w, ssd_conv_b, ssd_dt_bias, ssd_a_log, ssd_d, ssd_norm_w, ssd_w_out,
              nsa_w_in, nsa_cmp_pe, nsa_cmp_w, nsa_w_out, moba_w_in, moba_w_out):
    h_p, h_s = x_prompt, x_sample
    ssm_p, ssm_s, conv_p, conv_s = [], [], [], []
    nsa_kv_p, nsa_kv_s, nsa_win_p, nsa_win_s = [], [], [], []
    moba_kv_p, moba_kv_s = [], []
    for i in range(DEPTH):
        kind, j = i % N_MIXERS, i // N_MIXERS
        h_p = half_ffn(h_p, norm_w[i, 0], ffn_w_gate[i, 0], ffn_w_up[i, 0], ffn_w_down[i, 0])
        h_s = half_ffn(h_s, norm_w[i, 0], ffn_w_gate[i, 0], ffn_w_up[i, 0], ffn_w_down[i, 0])
        u_p = rms_norm(h_p, norm_w[i, 1])
        u_s = rms_norm(h_s, norm_w[i, 1])
        if kind == 0:
            w = (ssd_w_in[j], ssd_conv_w[j], ssd_conv_b[j], ssd_dt_bias[j], ssd_a_log[j], ssd_d[j], ssd_norm_w[j], ssd_w_out[j])
            conv0 = jnp.zeros((u_p.shape[0], CONV_W - 1, CONV_DIM), u_p.dtype)
            ssm0 = jnp.zeros((u_p.shape[0], SSD_HEADS, SSD_HEAD_DIM, D_STATE), jnp.float32)
            m_p, c_p, s_p = ssd_mixer(u_p, conv0, ssm0, *w)
            m_s, c_s, s_s = ssd_mixer(u_s, state_conv[j], state_ssm[j], *w)
            conv_p.append(c_p)
            conv_s.append(c_s)
            ssm_p.append(s_p)
            ssm_s.append(s_s)
        elif kind == 1:
            m_p, kv_p, win_p = nsa_prompt(u_p, nsa_w_in[j], nsa_cmp_pe[j], nsa_cmp_w[j], nsa_w_out[j], rel_bias)
            m_s, kv_s, win_s = nsa_sample(u_s, cache_nsa_kv[j], cache_nsa_win[j], page_table,
                                          nsa_w_in[j], nsa_cmp_pe[j], nsa_cmp_w[j], nsa_w_out[j], rel_bias)
            nsa_kv_p.append(kv_p)
            nsa_kv_s.append(kv_s)
            nsa_win_p.append(win_p)
            nsa_win_s.append(win_s)
        else:
            m_p, kv_p = moba_prompt(u_p, moba_w_in[j], moba_w_out[j], rel_bias)
            m_s, kv_s = moba_sample(u_s, cache_moba_kv[j], page_table, moba_w_in[j], moba_w_out[j], rel_bias)
            moba_kv_p.append(kv_p)
            moba_kv_s.append(kv_s)
        h_p = h_p + m_p
        h_s = h_s + m_s
        h_p = half_ffn(h_p, norm_w[i, 2], ffn_w_gate[i, 1], ffn_w_up[i, 1], ffn_w_down[i, 1])
        h_s = half_ffn(h_s, norm_w[i, 2], ffn_w_gate[i, 1], ffn_w_up[i, 1], ffn_w_down[i, 1])
    y_prompt = rms_norm(h_p, final_norm_w)
    y_sample = rms_norm(h_s, final_norm_w)
    return (y_prompt, y_sample,
            jnp.stack(ssm_p), jnp.stack(ssm_s), jnp.stack(conv_p), jnp.stack(conv_s),
            jnp.stack(nsa_kv_p), jnp.stack(nsa_kv_s), jnp.stack(nsa_win_p), jnp.stack(nsa_win_s),
            jnp.stack(moba_kv_p), jnp.stack(moba_kv_s))
```

```python
import functools
import math

import jax
import jax.numpy as jnp
from jax import lax
from jax.experimental import pallas as pl
from jax.experimental.pallas import tpu as pltpu

D_MODEL = 2048
BATCH = 2
SEQ = 4096
DEPTH = 4
DEC_BATCH = 8
DEC_SEQ = 8
PAST_LEN = 16384
PAGE_SIZE = 128
N_MIXERS = 3
D_FF = 5632
RMS_EPS = 1e-6
D_INNER = 2 * D_MODEL
SSD_HEAD_DIM = 64
SSD_HEADS = D_INNER // SSD_HEAD_DIM
D_STATE = 128
SSD_GROUPS = 8
HEADS_PER_GROUP = SSD_HEADS // SSD_GROUPS
GN = SSD_GROUPS * D_STATE
CONV_W = 4
CONV_DIM = D_INNER + 2 * GN
SSD_IN_DIM = D_INNER + CONV_DIM + SSD_HEADS
SSD_CHUNK = 128
HEAD_DIM = 128
N_HEADS = D_MODEL // HEAD_DIM
N_KV_HEADS = 4
Q_PER_KV = N_HEADS // N_KV_HEADS
Q_DIM = N_HEADS * HEAD_DIM
KV_DIM = N_KV_HEADS * HEAD_DIM
Q_BLOCK = 64
CMP_BLOCK = 32
CMP_STRIDE = 16
SEL_BLOCK = 64
SEL_TOPK = 16
N_LOCAL_SEL = 2
WINDOW = 512
NSA_IN_DIM = Q_DIM + 6 * KV_DIM + 3 * N_HEADS
MOBA_BLOCK = 256
MOBA_TOPK = 3
MOBA_IN_DIM = Q_DIM + 2 * KV_DIM
REL_BUCKETS = 32
REL_MAX_EXACT = REL_BUCKETS // 2
REL_MAX_DIST = 128
NEG_INF = -1e30
FORCE_SCORE = 1e9

M_PROMPT = BATCH * SEQ
M_SAMPLE = DEC_BATCH * DEC_SEQ
M_TOTAL = M_PROMPT + M_SAMPLE
ROW_TILE = 688
LANE = 128
VMEM_LIMIT = 56 * 1024 * 1024


def _rms_rows(x, g):
    ms = jnp.mean(x * x, axis=-1, keepdims=True)
    return x * lax.rsqrt(ms + RMS_EPS) * g


def _ffn_kernel(h_ref, g_ref, wg_ref, wu_ref, wd_ref, o_ref, xn_ref, acc_ref):
    f = pl.program_id(1)

    @pl.when(f == 0)
    def _():
        xn_ref[...] = _rms_rows(h_ref[...], g_ref[...]).astype(jnp.bfloat16)
        acc_ref[...] = jnp.zeros_like(acc_ref)

    xn = xn_ref[...]
    gate = jnp.dot(xn, wg_ref[...], preferred_element_type=jnp.float32)
    up = jnp.dot(xn, wu_ref[...], preferred_element_type=jnp.float32)
    act = (gate * jax.nn.sigmoid(gate) * up).astype(jnp.bfloat16)
    acc_ref[...] += jnp.dot(act, wd_ref[...], preferred_element_type=jnp.float32)

    @pl.when(f == pl.num_programs(1) - 1)
    def _():
        o_ref[...] = h_ref[...] + 0.5 * acc_ref[...]


def _ffn_half(h, g, wg, wu, wd, *, ff_tile=512):
    m, d = h.shape
    dff = wg.shape[1]
    grid = (m // ROW_TILE, dff // ff_tile)
    return pl.pallas_call(
        _ffn_kernel,
        grid=grid,
        in_specs=[
            pl.BlockSpec((ROW_TILE, d), lambda i, f: (i, 0)),
            pl.BlockSpec((1, d), lambda i, f: (0, 0)),
            pl.BlockSpec((d, ff_tile), lambda i, f: (0, f)),
            pl.BlockSpec((d, ff_tile), lambda i, f: (0, f)),
            pl.BlockSpec((ff_tile, d), lambda i, f: (f, 0)),
        ],
        out_specs=pl.BlockSpec((ROW_TILE, d), lambda i, f: (i, 0)),
        out_shape=jax.ShapeDtypeStruct((m, d), jnp.float32),
        scratch_shapes=[
            pltpu.VMEM((ROW_TILE, d), jnp.bfloat16),
            pltpu.VMEM((ROW_TILE, d), jnp.float32),
        ],
        compiler_params=pltpu.CompilerParams(
            dimension_semantics=("parallel", "arbitrary"),
            vmem_limit_bytes=VMEM_LIMIT),
    )(h, g.reshape(1, d), wg, wu, wd)


def _norm_proj_kernel(h_ref, g_ref, w_ref, o_ref, xn_ref):
    @pl.when(pl.program_id(1) == 0)
    def _():
        xn_ref[...] = _rms_rows(h_ref[...], g_ref[...]).astype(jnp.bfloat16)

    o_ref[...] = jnp.dot(xn_ref[...], w_ref[...], preferred_element_type=jnp.float32)


def _norm_proj(h, g, w, *, n_tile=512):
    m, d = h.shape
    n = w.shape[1]
    grid = (m // ROW_TILE, n // n_tile)
    return pl.pallas_call(
        _norm_proj_kernel,
        grid=grid,
        in_specs=[
            pl.BlockSpec((ROW_TILE, d), lambda i, j: (i, 0)),
            pl.BlockSpec((1, d), lambda i, j: (0, 0)),
            pl.BlockSpec((d, n_tile), lambda i, j: (0, j)),
        ],
        out_specs=pl.BlockSpec((ROW_TILE, n_tile), lambda i, j: (i, j)),
        out_shape=jax.ShapeDtypeStruct((m, n), jnp.float32),
        scratch_shapes=[pltpu.VMEM((ROW_TILE, d), jnp.bfloat16)],
        compiler_params=pltpu.CompilerParams(
            dimension_semantics=("parallel", "arbitrary"),
            vmem_limit_bytes=VMEM_LIMIT),
    )(h, g.reshape(1, d), w)


def _proj_residual_kernel(y_ref, w_ref, h_ref, o_ref):
    o_ref[...] = h_ref[...] + jnp.dot(
        y_ref[...].astype(jnp.bfloat16), w_ref[...], preferred_element_type=jnp.float32)


def _proj_residual(y, w, h, *, n_tile=512):
    m, k = y.shape
    d = w.shape[1]
    grid = (m // ROW_TILE, d // n_tile)
    return pl.pallas_call(
        _proj_residual_kernel,
        grid=grid,
        in_specs=[
            pl.BlockSpec((ROW_TILE, k), lambda i, j: (i, 0)),
            pl.BlockSpec((k, n_tile), lambda i, j: (0, j)),
            pl.BlockSpec((ROW_TILE, n_tile), lambda i, j: (i, j)),
        ],
        out_specs=pl.BlockSpec((ROW_TILE, n_tile), lambda i, j: (i, j)),
        out_shape=jax.ShapeDtypeStruct((m, d), jnp.float32),
        compiler_params=pltpu.CompilerParams(
            dimension_semantics=("parallel", "arbitrary"),
            vmem_limit_bytes=VMEM_LIMIT),
    )(y, w, h)


def _final_norm_kernel(h_ref, g_ref, o_ref):
    o_ref[...] = _rms_rows(h_ref[...], g_ref[...])


def _final_norm(h, g):
    m, d = h.shape
    return pl.pallas_call(
        _final_norm_kernel,
        grid=(m // ROW_TILE,),
        in_specs=[pl.BlockSpec((ROW_TILE, d), lambda i: (i, 0)),
                  pl.BlockSpec((1, d), lambda i: (0, 0))],
        out_specs=pl.BlockSpec((ROW_TILE, d), lambda i: (i, 0)),
        out_shape=jax.ShapeDtypeStruct((m, d), jnp.float32),
        compiler_params=pltpu.CompilerParams(dimension_semantics=("parallel",)),
    )(h, g.reshape(1, d))


def _masked_softmax(logits, mask):
    logits = jnp.where(mask, logits.astype(jnp.float32), NEG_INF)
    m = jnp.max(logits, axis=-1, keepdims=True)
    e = jnp.where(mask, jnp.exp(logits - m), 0.0)
    return e / jnp.maximum(jnp.sum(e, axis=-1, keepdims=True), 1e-30)


def _rel_bucket(dist):
    n = jnp.maximum(dist, 0)
    log_ratio = jnp.log(jnp.maximum(n, 1).astype(jnp.float32) / REL_MAX_EXACT) / math.log(REL_MAX_DIST / REL_MAX_EXACT)
    large = jnp.minimum(REL_MAX_EXACT + (log_ratio * (REL_BUCKETS - REL_MAX_EXACT)).astype(jnp.int32), REL_BUCKETS - 1)
    return jnp.where(n < REL_MAX_EXACT, n, large)


def _shared_bias(rel_bias, q_pos, k_pos):
    bias = rel_bias[_rel_bucket(q_pos[:, None] - k_pos[None, :])]
    nq, nk = bias.shape[:2]
    return bias.reshape(nq, nk, N_KV_HEADS, Q_PER_KV).transpose(0, 2, 3, 1).astype(jnp.float32)


def _gathered_bias(rel_bias, dist):
    table = rel_bias.T.reshape(N_KV_HEADS, Q_PER_KV, REL_BUCKETS)
    g_idx = jnp.arange(N_KV_HEADS).reshape(1, 1, -1, 1, 1, 1)
    r_idx = jnp.arange(Q_PER_KV).reshape(1, 1, 1, -1, 1, 1)
    return table[g_idx, r_idx, _rel_bucket(dist)].astype(jnp.float32)


def _gather_pages(pool, page_table, comp):
    rows = pool[page_table, :, comp]
    return rows.reshape(page_table.shape[0], page_table.shape[1] * pool.shape[1], N_KV_HEADS, HEAD_DIM)


def _ssd_scan(x, dt, a, bmat, cmat, init_state):
    b, t, nh, p = x.shape
    l = min(SSD_CHUNK, t)
    nc = -(-t // l)
    pad = nc * l - t

    def padt(arr):
        return jnp.pad(arr.astype(jnp.float32), [(0, 0), (0, pad)] + [(0, 0)] * (arr.ndim - 2))

    xr = padt(x).reshape(b, nc, l, SSD_GROUPS, HEADS_PER_GROUP, p)
    dtr = padt(dt).reshape(b, nc, l, SSD_GROUPS, HEADS_PER_GROUP)
    br = padt(bmat).reshape(b, nc, l, SSD_GROUPS, D_STATE)
    cr = padt(cmat).reshape(b, nc, l, SSD_GROUPS, D_STATE)
    acum = jnp.cumsum(dtr * a.reshape(SSD_GROUPS, HEADS_PER_GROUP), axis=2)
    xdt = xr * dtr[..., None]
    causal = jnp.tril(jnp.ones((l, l), dtype=bool))[None, None, :, :, None, None]
    decay = jnp.exp(jnp.where(causal, acum[:, :, :, None] - acum[:, :, None, :], -jnp.inf))
    cb = jnp.einsum('bclgn,bcsgn->bclsg', cr, br)
    y_diag = jnp.einsum('bclsg,bclsgr,bcsgrp->bclgrp', cb, decay, xdt)
    chunk_states = jnp.einsum('bclgn,bclgr,bclgrp->bcgrpn', br, jnp.exp(acum[:, :, -1:] - acum), xdt)
    chunk_decay = jnp.exp(acum[:, :, -1])

    def step(s, inp):
        st, dc = inp
        return s * dc[..., None, None] + st, s

    s0 = init_state.astype(jnp.float32).reshape(b, SSD_GROUPS, HEADS_PER_GROUP, p, D_STATE)
    final, prev = lax.scan(step, s0, (jnp.moveaxis(chunk_states, 1, 0), jnp.moveaxis(chunk_decay, 1, 0)))
    prev = jnp.moveaxis(prev, 0, 1)
    y_off = jnp.einsum('bclgn,bcgrpn,bclgr->bclgrp', cr, prev, jnp.exp(acum))
    y = (y_diag + y_off).reshape(b, nc * l, nh, p)[:, :t]
    return y.astype(x.dtype), final.reshape(b, nh, p, D_STATE).astype(init_state.dtype)


def _ssd_core(proj, conv_state, ssm_state, conv_w, conv_b, dt_bias, a_log, d_skip, norm_w):
    b, t, _ = proj.shape
    z = proj[..., :D_INNER]
    xbc = proj[..., D_INNER:D_INNER + CONV_DIM]
    dt_raw = proj[..., D_INNER + CONV_DIM:D_INNER + CONV_DIM + SSD_HEADS]
    xbc_ext = jnp.concatenate([conv_state.astype(xbc.dtype), xbc], axis=1)
    conv = conv_b + sum(xbc_ext[:, k:k + t] * conv_w[k] for k in range(CONV_W))
    xbc = jax.nn.silu(conv)
    x = xbc[..., :D_INNER].reshape(b, t, SSD_HEADS, SSD_HEAD_DIM)
    bmat = xbc[..., D_INNER:D_INNER + GN].reshape(b, t, SSD_GROUPS, D_STATE)
    cmat = xbc[..., D_INNER + GN:].reshape(b, t, SSD_GROUPS, D_STATE)
    dt = jax.nn.softplus((dt_raw + dt_bias).astype(jnp.float32))
    a = -jnp.exp(a_log.astype(jnp.float32))
    y, new_ssm = _ssd_scan(x, dt, a, bmat, cmat, ssm_state)
    y = (y + x * d_skip[:, None]).reshape(b, t, D_INNER) * jax.nn.silu(z)
    yg = y.reshape(b, t, SSD_GROUPS, D_INNER // SSD_GROUPS).astype(jnp.float32)
    yg = yg * lax.rsqrt(jnp.mean(yg * yg, axis=-1, keepdims=True) + RMS_EPS)
    y = yg.reshape(b, t, D_INNER) * norm_w
    return y, xbc_ext[:, xbc_ext.shape[1] - (CONV_W - 1):], new_ssm


def _nsa_split(proj):
    b, t, _ = proj.shape
    q = proj[..., :Q_DIM].reshape(b, t, N_KV_HEADS, Q_PER_KV, HEAD_DIM)
    kv = proj[..., Q_DIM:Q_DIM + 6 * KV_DIM].reshape(b, t, 6, N_KV_HEADS, HEAD_DIM)
    gates = jax.nn.sigmoid(proj[..., Q_DIM + 6 * KV_DIM:NSA_IN_DIM]).reshape(b, t, N_KV_HEADS, Q_PER_KV, 3)
    return q, kv, gates


def _nsa_compress(k, pe, w):
    b, t = k.shape[:2]
    n_str = t // CMP_STRIDE
    ch = k[:, :n_str * CMP_STRIDE].reshape(b, n_str, CMP_STRIDE, N_KV_HEADS, HEAD_DIM)
    blk = jnp.concatenate([ch[:, :-1], ch[:, 1:]], axis=2) + pe[:, None, :]
    flat = blk.transpose(0, 1, 3, 2, 4).reshape(b, n_str - 1, N_KV_HEADS, CMP_BLOCK * HEAD_DIM)
    return flat @ w


def _nsa_branch_keys(k_cmp, v_cmp, k_sel, v_sel, cmp_pe, cmp_w):
    b, t = k_cmp.shape[:2]
    kc = _nsa_compress(k_cmp, cmp_pe[0], cmp_w[0])
    vc = _nsa_compress(v_cmp, cmp_pe[1], cmp_w[1])
    c_end = jnp.arange(kc.shape[1], dtype=jnp.int32) * CMP_STRIDE + (CMP_BLOCK - 1)
    n_sel = -(-t // SEL_BLOCK)
    pad = ((0, 0), (0, n_sel * SEL_BLOCK - t), (0, 0), (0, 0))
    ks = jnp.pad(k_sel, pad).reshape(b, n_sel, SEL_BLOCK, N_KV_HEADS, HEAD_DIM)
    vs = jnp.pad(v_sel, pad).reshape(b, n_sel, SEL_BLOCK, N_KV_HEADS, HEAD_DIM)
    return kc, vc, c_end, ks, vs


def _nsa_core(q, q_pos, kc, vc, c_end, ks_blk, vs_blk, kw, vw, w_pos, gates, rel_bias):
    bsz, nq = q.shape[:2]
    scale = HEAD_DIM ** -0.5
    tq = q_pos[:, None]
    lc = jnp.einsum('bqgrd,bkgd->bqgrk', q, kc) * scale + _shared_bias(rel_bias, q_pos, c_end)
    pc = _masked_softmax(lc, (c_end[None, :] <= tq)[None, :, None, None, :])
    o_cmp = jnp.einsum('bqgrk,bkgd->bqgrd', pc.astype(vc.dtype), vc)
    nc, ns = kc.shape[1], ks_blk.shape[1]
    ci = jnp.arange(nc)[:, None]
    sj = jnp.arange(ns)[None, :]
    overlap = ((ci * CMP_STRIDE < (sj + 1) * SEL_BLOCK) & (ci * CMP_STRIDE + CMP_BLOCK > sj * SEL_BLOCK)).astype(jnp.float32)
    imp = jnp.einsum('bqgk,kj->bqgj', jnp.sum(pc, axis=3), overlap)
    own = q_pos // SEL_BLOCK
    j = jnp.arange(ns)
    diff = own[:, None] - j[None, :]
    valid = diff >= 0
    forced = (j[None, :] == 0) | (valid & (diff < N_LOCAL_SEL))
    score = jnp.where((valid & forced)[None, :, None, :], FORCE_SCORE, jnp.where(valid[None, :, None, :], imp, NEG_INF))
    k_top = min(SEL_TOPK, ns)
    _, idx = lax.top_k(score, k_top)
    sel_ok = idx <= own[None, :, None, None]
    bi = jnp.arange(bsz).reshape(-1, 1, 1, 1)
    gi = jnp.arange(N_KV_HEADS).reshape(1, 1, -1, 1)
    kg = ks_blk[bi, idx, :, gi]
    vg = vs_blk[bi, idx, :, gi]
    pos = idx[..., None] * SEL_BLOCK + jnp.arange(SEL_BLOCK)
    tq5 = q_pos[None, :, None, None, None]
    ms = sel_ok[..., None] & (pos <= tq5)
    ls = jnp.einsum('bqgrd,bqgkcd->bqgrkc', q, kg) * scale + _gathered_bias(rel_bias, (tq5 - pos)[:, :, :, None])
    ps = _masked_softmax(ls.reshape(bsz, nq, N_KV_HEADS, Q_PER_KV, k_top * SEL_BLOCK),
                         ms[:, :, :, None].reshape(bsz, nq, N_KV_HEADS, 1, k_top * SEL_BLOCK))
    ps = ps.reshape(bsz, nq, N_KV_HEADS, Q_PER_KV, k_top, SEL_BLOCK).astype(vg.dtype)
    o_sel = jnp.einsum('bqgrkc,bqgkcd->bqgrd', ps, vg)
    lw = jnp.einsum('bqgrd,bkgd->bqgrk', q, kw) * scale + _shared_bias(rel_bias, q_pos, w_pos)
    dw = tq - w_pos[None, :]
    mw = ((dw >= 0) & (dw < WINDOW) & (w_pos[None, :] >= 0))[None, :, None, None, :]
    o_win = jnp.einsum('bqgrk,bkgd->bqgrd', _masked_softmax(lw, mw).astype(vw.dtype), vw)
    return gates[..., 0:1] * o_cmp + gates[..., 1:2] * o_sel + gates[..., 2:3] * o_win


def _nsa_prompt_core(proj, cmp_pe, cmp_w, rel_bias):
    b, t, _ = proj.shape
    q, kv, gates = _nsa_split(proj)
    kc, vc, c_end, ks, vs = _nsa_branch_keys(kv[:, :, 0], kv[:, :, 1], kv[:, :, 2], kv[:, :, 3], cmp_pe, cmp_w)
    wpad = ((0, 0), (WINDOW, 0), (0, 0), (0, 0))
    kw_pad, vw_pad = jnp.pad(kv[:, :, 4], wpad), jnp.pad(kv[:, :, 5], wpad)

    def block(i):
        s0 = i * Q_BLOCK
        q_pos = s0 + jnp.arange(Q_BLOCK, dtype=jnp.int32)
        w_pos = s0 - WINDOW + jnp.arange(WINDOW + Q_BLOCK, dtype=jnp.int32)
        qb = lax.dynamic_slice_in_dim(q, s0, Q_BLOCK, axis=1)
        gb = lax.dynamic_slice_in_dim(gates, s0, Q_BLOCK, axis=1)
        kw = lax.dynamic_slice_in_dim(kw_pad, s0, WINDOW + Q_BLOCK, axis=1)
        vw = lax.dynamic_slice_in_dim(vw_pad, s0, WINDOW + Q_BLOCK, axis=1)
        return _nsa_core(qb, q_pos, kc, vc, c_end, ks, vs, kw, vw, w_pos, gb, rel_bias)

    o = lax.map(block, jnp.arange(t // Q_BLOCK, dtype=jnp.int32))
    o = jnp.moveaxis(o, 0, 1).reshape(b, t, Q_DIM)
    wb = min(WINDOW, t)
    return o, kv[:, :, :4], kv[:, t - wb:, 4:]


def _nsa_sample_core(proj, cache_kv, win_buf, page_table, cmp_pe, cmp_w, rel_bias):
    b, t, _ = proj.shape
    q, kv, gates = _nsa_split(proj)
    full = [jnp.concatenate([_gather_pages(cache_kv, page_table, c).astype(kv.dtype), kv[:, :, c]], axis=1) for c in range(4)]
    kc, vc, c_end, ks, vs = _nsa_branch_keys(full[0], full[1], full[2], full[3], cmp_pe, cmp_w)
    wb = win_buf.shape[1]
    win = jnp.concatenate([win_buf.astype(kv.dtype), kv[:, :, 4:]], axis=1)
    w_pos = PAST_LEN - wb + jnp.arange(wb + t, dtype=jnp.int32)
    q_pos = PAST_LEN + jnp.arange(t, dtype=jnp.int32)
    o = _nsa_core(q, q_pos, kc, vc, c_end, ks, vs, win[:, :, 0], win[:, :, 1], w_pos, gates, rel_bias)
    return o.reshape(b, t, Q_DIM), kv[:, :, :4], win[:, t:]


def _moba_split(proj):
    b, t, _ = proj.shape
    q = proj[..., :Q_DIM].reshape(b, t, N_KV_HEADS, Q_PER_KV, HEAD_DIM)
    kv = proj[..., Q_DIM:].reshape(b, t, 2, N_KV_HEADS, HEAD_DIM)
    return q, kv


def _moba_blocks(k, v):
    b, t = k.shape[:2]
    nf = t // MOBA_BLOCK
    kb = k[:, :nf * MOBA_BLOCK].reshape(b, nf, MOBA_BLOCK, N_KV_HEADS, HEAD_DIM)
    vb = v[:, :nf * MOBA_BLOCK].reshape(b, nf, MOBA_BLOCK, N_KV_HEADS, HEAD_DIM)
    return kb, vb, jnp.mean(kb.astype(jnp.float32), axis=2)


def _moba_core(q, q_pos, kb, vb, kmean, own_k, own_v, own_pos, rel_bias):
    bsz, nq = q.shape[:2]
    scale = HEAD_DIM ** -0.5
    q_blk = q_pos // MOBA_BLOCK
    lo = jnp.einsum('bqgrd,bkgd->bqgrk', q, own_k) * scale + _shared_bias(rel_bias, q_pos, own_pos)
    mo = (own_pos[None, :] <= q_pos[:, None]) & ((own_pos // MOBA_BLOCK)[None, :] == q_blk[:, None])
    mo = jnp.broadcast_to(mo[None, :, None, None, :], lo.shape)
    nf = kmean.shape[1]
    k_top = min(MOBA_TOPK, nf)
    gate = jnp.einsum('bqgrd,bngd->bqgrn', q.astype(jnp.float32), kmean)
    past_ok = (jnp.arange(nf)[None, :] < q_blk[:, None])[None, :, None, None, :]
    _, idx = lax.top_k(jnp.where(past_ok, gate, NEG_INF), k_top)
    sel_ok = idx < q_blk[None, :, None, None, None]
    bi = jnp.arange(bsz).reshape(-1, 1, 1, 1, 1)
    gi = jnp.arange(N_KV_HEADS).reshape(1, 1, -1, 1, 1)
    kg = kb[bi, idx, :, gi]
    vg = vb[bi, idx, :, gi]
    pos = idx[..., None] * MOBA_BLOCK + jnp.arange(MOBA_BLOCK)
    lp = jnp.einsum('bqgrd,bqgrkcd->bqgrkc', q, kg) * scale + _gathered_bias(rel_bias, q_pos[None, :, None, None, None, None] - pos)
    mp = jnp.broadcast_to(sel_ok[..., None], lp.shape)
    n_past = k_top * MOBA_BLOCK
    logits = jnp.concatenate([lp.reshape(bsz, nq, N_KV_HEADS, Q_PER_KV, n_past), lo], axis=-1)
    mask = jnp.concatenate([mp.reshape(bsz, nq, N_KV_HEADS, Q_PER_KV, n_past), mo], axis=-1)
    p = _masked_softmax(logits, mask).astype(vg.dtype)
    pp = p[..., :n_past].reshape(bsz, nq, N_KV_HEADS, Q_PER_KV, k_top, MOBA_BLOCK)
    return jnp.einsum('bqgrkc,bqgrkcd->bqgrd', pp, vg) + jnp.einsum('bqgrk,bkgd->bqgrd', p[..., n_past:], own_v)


def _moba_prompt_core(proj, rel_bias):
    b, t, _ = proj.shape
    q, kv = _moba_split(proj)
    k, v = kv[:, :, 0], kv[:, :, 1]
    kb, vb, kmean = _moba_blocks(k, v)

    def block(i):
        s0 = i * Q_BLOCK
        q_pos = s0 + jnp.arange(Q_BLOCK, dtype=jnp.int32)
        own0 = (s0 // MOBA_BLOCK) * MOBA_BLOCK
        own_pos = own0 + jnp.arange(MOBA_BLOCK, dtype=jnp.int32)
        qb = lax.dynamic_slice_in_dim(q, s0, Q_BLOCK, axis=1)
        ok = lax.dynamic_slice_in_dim(k, own0, MOBA_BLOCK, axis=1)
        ov = lax.dynamic_slice_in_dim(v, own0, MOBA_BLOCK, axis=1)
        return _moba_core(qb, q_pos, kb, vb, kmean, ok, ov, own_pos, rel_bias)

    o = lax.map(block, jnp.arange(t // Q_BLOCK, dtype=jnp.int32))
    o = jnp.moveaxis(o, 0, 1).reshape(b, t, Q_DIM)
    return o, kv


def _moba_sample_core(proj, cache_kv, page_table, rel_bias):
    b, t, _ = proj.shape
    q, kv = _moba_split(proj)
    k = jnp.concatenate([_gather_pages(cache_kv, page_table, 0).astype(kv.dtype), kv[:, :, 0]], axis=1)
    v = jnp.concatenate([_gather_pages(cache_kv, page_table, 1).astype(kv.dtype), kv[:, :, 1]], axis=1)
    kb, vb, kmean = _moba_blocks(k, v)
    own0 = (PAST_LEN // MOBA_BLOCK) * MOBA_BLOCK
    own_pos = own0 + jnp.arange(k.shape[1] - own0, dtype=jnp.int32)
    q_pos = PAST_LEN + jnp.arange(t, dtype=jnp.int32)
    o = _moba_core(q, q_pos, kb, vb, kmean, k[:, own0:], v[:, own0:], own_pos, rel_bias)
    return o.reshape(b, t, Q_DIM), kv


def _pad_cols(w, n):
    return jnp.pad(w, ((0, 0), (0, n - w.shape[1])))


def _split_rows(a):
    return (a[:M_PROMPT].reshape(BATCH, SEQ, a.shape[-1]),
            a[M_PROMPT:].reshape(DEC_BATCH, DEC_SEQ, a.shape[-1]))


def _join_rows(p, s):
    return jnp.concatenate([p.reshape(M_PROMPT, p.shape[-1]), s.reshape(M_SAMPLE, s.shape[-1])], axis=0)


def kernel(x_prompt, x_sample, state_ssm, state_conv, cache_nsa_kv, cache_nsa_win, cache_moba_kv, page_table, rel_bias, norm_w, final_norm_w, ffn_w_gate, ffn_w_up, ffn_w_down, ssd_w_in, ssd_conv_w, ssd_conv_b, ssd_dt_bias, ssd_a_log, ssd_d, ssd_norm_w, ssd_w_out, nsa_w_in, nsa_cmp_pe, nsa_cmp_w, nsa_w_out, moba_w_in, moba_w_out):
    bf = jnp.bfloat16
    h = _join_rows(x_prompt, x_sample)
    ssm_p, ssm_s, conv_p, conv_s = [], [], [], []
    nsa_kv_p, nsa_kv_s, nsa_win_p, nsa_win_s = [], [], [], []
    moba_kv_p, moba_kv_s = [], []
    for i in range(DEPTH):
        kind, j = i % N_MIXERS, i // N_MIXERS
        h = _ffn_half(h, norm_w[i, 0], ffn_w_gate[i, 0].astype(bf), ffn_w_up[i, 0].astype(bf), ffn_w_down[i, 0].astype(bf))
        if kind == 0:
            n_pad = -(-SSD_IN_DIM // 512) * 512
            proj = _norm_proj(h, norm_w[i, 1], _pad_cols(ssd_w_in[j], n_pad).astype(bf))
            proj_p, proj_s = _split_rows(proj)
            w = (ssd_conv_w[j], ssd_conv_b[j], ssd_dt_bias[j], ssd_a_log[j], ssd_d[j], ssd_norm_w[j])
            conv0 = jnp.zeros((BATCH, CONV_W - 1, CONV_DIM), jnp.float32)
            ssm0 = jnp.zeros((BATCH, SSD_HEADS, SSD_HEAD_DIM, D_STATE), jnp.float32)
            y_p, c_p, s_p = _ssd_core(proj_p, conv0, ssm0, *w)
            y_s, c_s, s_s = _ssd_core(proj_s, state_conv[j], state_ssm[j], *w)
            conv_p.append(c_p)
            conv_s.append(c_s)
            ssm_p.append(s_p)
            ssm_s.append(s_s)
            w_out = ssd_w_out[j]
        elif kind == 1:
            n_pad = -(-NSA_IN_DIM // 512) * 512
            proj = _norm_proj(h, norm_w[i, 1], _pad_cols(nsa_w_in[j], n_pad).astype(bf))
            proj_p, proj_s = _split_rows(proj)
            y_p, kv_p, win_p = _nsa_prompt_core(proj_p, nsa_cmp_pe[j], nsa_cmp_w[j], rel_bias)
            y_s, kv_s, win_s = _nsa_sample_core(proj_s, cache_nsa_kv[j], cache_nsa_win[j], page_table,
                                                nsa_cmp_pe[j], nsa_cmp_w[j], rel_bias)
            nsa_kv_p.append(kv_p)
            nsa_kv_s.append(kv_s)
            nsa_win_p.append(win_p)
            nsa_win_s.append(win_s)
            w_out = nsa_w_out[j]
        else:
            proj = _norm_proj(h, norm_w[i, 1], moba_w_in[j].astype(bf))
            proj_p, proj_s = _split_rows(proj)
            y_p, kv_p = _moba_prompt_core(proj_p, rel_bias)
            y_s, kv_s = _moba_sample_core(proj_s, cache_moba_kv[j], page_table, rel_bias)
            moba_kv_p.append(kv_p)
            moba_kv_s.append(kv_s)
            w_out = moba_w_out[j]
        h = _proj_residual(_join_rows(y_p, y_s), w_out.astype(bf), h)
        h = _ffn_half(h, norm_w[i, 2], ffn_w_gate[i, 1].astype(bf), ffn_w_up[i, 1].astype(bf), ffn_w_down[i, 1].astype(bf))
    y = _final_norm(h, final_norm_w)
    y_prompt, y_sample = _split_rows(y)
    return (y_prompt, y_sample,
            jnp.stack(ssm_p), jnp.stack(ssm_s), jnp.stack(conv_p), jnp.stack(conv_s),
            jnp.stack(nsa_kv_p), jnp.stack(nsa_kv_s), jnp.stack(nsa_win_p), jnp.stack(nsa_win_s),
            jnp.stack(moba_kv_p), jnp.stack(moba_kv_s))
```

```python
import functools
import math

import jax
import jax.numpy as jnp
import numpy as np
from jax import lax
from jax.experimental import pallas as pl
from jax.experimental.pallas import tpu as pltpu

D_MODEL = 2048
BATCH = 2
SEQ = 4096
DEPTH = 4
DEC_BATCH = 8
DEC_SEQ = 8
PAST_LEN = 16384
PAGE_SIZE = 128
N_MIXERS = 3
D_FF = 5632
RMS_EPS = 1e-6
D_INNER = 2 * D_MODEL
SSD_HEAD_DIM = 64
SSD_HEADS = D_INNER // SSD_HEAD_DIM
D_STATE = 128
SSD_GROUPS = 8
HEADS_PER_GROUP = SSD_HEADS // SSD_GROUPS
GN = SSD_GROUPS * D_STATE
CONV_W = 4
CONV_DIM = D_INNER + 2 * GN
SSD_IN_DIM = D_INNER + CONV_DIM + SSD_HEADS
SSD_CHUNK = 128
HEAD_DIM = 128
N_HEADS = D_MODEL // HEAD_DIM
N_KV_HEADS = 4
Q_PER_KV = N_HEADS // N_KV_HEADS
Q_DIM = N_HEADS * HEAD_DIM
KV_DIM = N_KV_HEADS * HEAD_DIM
Q_BLOCK = 64
CMP_BLOCK = 32
CMP_STRIDE = 16
SEL_BLOCK = 64
SEL_TOPK = 16
N_LOCAL_SEL = 2
WINDOW = 512
NSA_IN_DIM = Q_DIM + 6 * KV_DIM + 3 * N_HEADS
MOBA_BLOCK = 256
MOBA_TOPK = 3
MOBA_IN_DIM = Q_DIM + 2 * KV_DIM
REL_BUCKETS = 32
REL_MAX_EXACT = REL_BUCKETS // 2
REL_MAX_DIST = 128
NEG_INF = -1e30
FORCE_SCORE = 1e9

M_PROMPT = BATCH * SEQ
M_SAMPLE = DEC_BATCH * DEC_SEQ
M_TOTAL = M_PROMPT + M_SAMPLE
ROW_TILE = 688
LANE = 128
VMEM_LIMIT = 56 * 1024 * 1024


def _rms_rows(x, g):
    ms = jnp.mean(x * x, axis=-1, keepdims=True)
    return x * lax.rsqrt(ms + RMS_EPS) * g


def _ffn_kernel(h_ref, g_ref, wg_ref, wu_ref, wd_ref, o_ref, xn_ref, acc_ref):
    f = pl.program_id(1)

    @pl.when(f == 0)
    def _():
        xn_ref[...] = _rms_rows(h_ref[...], g_ref[...]).astype(jnp.bfloat16)
        acc_ref[...] = jnp.zeros_like(acc_ref)

    xn = xn_ref[...]
    gate = jnp.dot(xn, wg_ref[...], preferred_element_type=jnp.float32)
    up = jnp.dot(xn, wu_ref[...], preferred_element_type=jnp.float32)
    act = (gate * jax.nn.sigmoid(gate) * up).astype(jnp.bfloat16)
    acc_ref[...] += jnp.dot(act, wd_ref[...], preferred_element_type=jnp.float32)

    @pl.when(f == pl.num_programs(1) - 1)
    def _():
        o_ref[...] = h_ref[...] + 0.5 * acc_ref[...]


def _ffn_half(h, g, wg, wu, wd, *, ff_tile=512):
    m, d = h.shape
    dff = wg.shape[1]
    grid = (m // ROW_TILE, dff // ff_tile)
    return pl.pallas_call(
        _ffn_kernel,
        grid=grid,
        in_specs=[
            pl.BlockSpec((ROW_TILE, d), lambda i, f: (i, 0)),
            pl.BlockSpec((1, d), lambda i, f: (0, 0)),
            pl.BlockSpec((d, ff_tile), lambda i, f: (0, f)),
            pl.BlockSpec((d, ff_tile), lambda i, f: (0, f)),
            pl.BlockSpec((ff_tile, d), lambda i, f: (f, 0)),
        ],
        out_specs=pl.BlockSpec((ROW_TILE, d), lambda i, f: (i, 0)),
        out_shape=jax.ShapeDtypeStruct((m, d), jnp.float32),
        scratch_shapes=[
            pltpu.VMEM((ROW_TILE, d), jnp.bfloat16),
            pltpu.VMEM((ROW_TILE, d), jnp.float32),
        ],
        compiler_params=pltpu.CompilerParams(
            dimension_semantics=("parallel", "arbitrary"),
            vmem_limit_bytes=VMEM_LIMIT),
    )(h, g.reshape(1, d), wg, wu, wd)


def _norm_proj_kernel(h_ref, g_ref, w_ref, o_ref, xn_ref):
    @pl.when(pl.program_id(1) == 0)
    def _():
        xn_ref[...] = _rms_rows(h_ref[...], g_ref[...]).astype(jnp.bfloat16)

    o_ref[...] = jnp.dot(xn_ref[...], w_ref[...], preferred_element_type=jnp.float32)


def _norm_proj(h, g, w, *, n_tile=512):
    m, d = h.shape
    n = w.shape[1]
    grid = (m // ROW_TILE, n // n_tile)
    return pl.pallas_call(
        _norm_proj_kernel,
        grid=grid,
        in_specs=[
            pl.BlockSpec((ROW_TILE, d), lambda i, j: (i, 0)),
            pl.BlockSpec((1, d), lambda i, j: (0, 0)),
            pl.BlockSpec((d, n_tile), lambda i, j: (0, j)),
        ],
        out_specs=pl.BlockSpec((ROW_TILE, n_tile), lambda i, j: (i, j)),
        out_shape=jax.ShapeDtypeStruct((m, n), jnp.float32),
        scratch_shapes=[pltpu.VMEM((ROW_TILE, d), jnp.bfloat16)],
        compiler_params=pltpu.CompilerParams(
            dimension_semantics=("parallel", "arbitrary"),
            vmem_limit_bytes=VMEM_LIMIT),
    )(h, g.reshape(1, d), w)


def _proj_residual_kernel(y_ref, w_ref, h_ref, o_ref):
    o_ref[...] = h_ref[...] + jnp.dot(
        y_ref[...].astype(jnp.bfloat16), w_ref[...], preferred_element_type=jnp.float32)


def _proj_residual(y, w, h, *, n_tile=512):
    m, k = y.shape
    d = w.shape[1]
    grid = (m // ROW_TILE, d // n_tile)
    return pl.pallas_call(
        _proj_residual_kernel,
        grid=grid,
        in_specs=[
            pl.BlockSpec((ROW_TILE, k), lambda i, j: (i, 0)),
            pl.BlockSpec((k, n_tile), lambda i, j: (0, j)),
            pl.BlockSpec((ROW_TILE, n_tile), lambda i, j: (i, j)),
        ],
        out_specs=pl.BlockSpec((ROW_TILE, n_tile), lambda i, j: (i, j)),
        out_shape=jax.ShapeDtypeStruct((m, d), jnp.float32),
        compiler_params=pltpu.CompilerParams(
            dimension_semantics=("parallel", "arbitrary"),
            vmem_limit_bytes=VMEM_LIMIT),
    )(y, w, h)


def _final_norm_kernel(h_ref, g_ref, o_ref):
    o_ref[...] = _rms_rows(h_ref[...], g_ref[...])


def _final_norm(h, g):
    m, d = h.shape
    return pl.pallas_call(
        _final_norm_kernel,
        grid=(m // ROW_TILE,),
        in_specs=[pl.BlockSpec((ROW_TILE, d), lambda i: (i, 0)),
                  pl.BlockSpec((1, d), lambda i: (0, 0))],
        out_specs=pl.BlockSpec((ROW_TILE, d), lambda i: (i, 0)),
        out_shape=jax.ShapeDtypeStruct((m, d), jnp.float32),
        compiler_params=pltpu.CompilerParams(dimension_semantics=("parallel",)),
    )(h, g.reshape(1, d))


ATT_TILE = 256
HEAD_ROWS = Q_PER_KV * ATT_TILE
SCALE = HEAD_DIM ** -0.5
_NT = (((1,), (1,)), ((), ()))


def _bucket_upper_bounds():
    n = np.arange(0, 2 * REL_MAX_DIST)
    ratio = np.log(np.maximum(n, 1).astype(np.float32) / np.float32(REL_MAX_EXACT)) / np.float32(
        math.log(REL_MAX_DIST / REL_MAX_EXACT))
    large = np.minimum(REL_MAX_EXACT + (ratio * np.float32(REL_BUCKETS - REL_MAX_EXACT)).astype(np.int32),
                       REL_BUCKETS - 1)
    bucket = np.where(n < REL_MAX_EXACT, n, large)
    return [int(n[bucket > b].min()) for b in range(REL_BUCKETS - 1)]


_BUCKET_HI = _bucket_upper_bounds()


def _bias_from_dist(dist, head_vals):
    accs = [jnp.zeros(dist.shape, jnp.float32) + vals[REL_BUCKETS - 1] for vals in head_vals]
    for b in range(REL_BUCKETS - 2, -1, -1):
        near = dist < _BUCKET_HI[b]
        accs = [jnp.where(near, vals[b], acc) for vals, acc in zip(head_vals, accs)]
    return accs


def _head_vals(rb_ref, g):
    return [[rb_ref[b, g * Q_PER_KV + r] for b in range(REL_BUCKETS)] for r in range(Q_PER_KV)]


def _build_bias_tiles(rb_ref, g, bias_ref):
    row = lax.broadcasted_iota(jnp.int32, (ATT_TILE, ATT_TILE), 0)
    col = lax.broadcasted_iota(jnp.int32, (ATT_TILE, ATT_TILE), 1)
    vals = _head_vals(rb_ref, g)
    for kind in range(2):
        tiles = _bias_from_dist(row - col + kind * ATT_TILE, vals)
        for r in range(Q_PER_KV):
            bias_ref[kind, r * ATT_TILE:(r + 1) * ATT_TILE, :] = tiles[r]
    for r in range(Q_PER_KV):
        bias_ref[2, r * ATT_TILE:(r + 1) * ATT_TILE, :] = (
            jnp.zeros((ATT_TILE, ATT_TILE), jnp.float32) + vals[r][REL_BUCKETS - 1])


def _stack_heads(q_ref):
    return jnp.concatenate(
        [q_ref[:, r * HEAD_DIM:(r + 1) * HEAD_DIM] for r in range(Q_PER_KV)], axis=0).astype(jnp.bfloat16)


def _tile4(x):
    return jnp.concatenate([x] * Q_PER_KV, axis=0)


def _flash_first(qs, k_tile, v_tile, bias, mask):
    s = lax.dot_general(qs, k_tile, _NT, preferred_element_type=jnp.float32) * SCALE + bias
    s = jnp.where(mask, s, NEG_INF)
    m = jnp.max(s, axis=-1, keepdims=True)
    p = jnp.exp(s - m)
    l = jnp.sum(p, axis=-1, keepdims=True)
    acc = jnp.dot(p.astype(jnp.bfloat16), v_tile, preferred_element_type=jnp.float32)
    return m, l, acc


def _flash_next(qs, k_tile, v_tile, bias, mask, m, l, acc):
    s = lax.dot_general(qs, k_tile, _NT, preferred_element_type=jnp.float32) * SCALE + bias
    s = jnp.where(mask, s, NEG_INF)
    m_new = jnp.maximum(m, jnp.max(s, axis=-1, keepdims=True))
    alpha = jnp.exp(m - m_new)
    p = jnp.exp(s - m_new)
    l = alpha * l + jnp.sum(p, axis=-1, keepdims=True)
    acc = alpha * acc + jnp.dot(p.astype(jnp.bfloat16), v_tile, preferred_element_type=jnp.float32)
    return m_new, l, acc


def _kv_tile(ref, j):
    return ref[pl.ds(pl.multiple_of(j * ATT_TILE, ATT_TILE), ATT_TILE), :].astype(jnp.bfloat16)


def _rank_select(score, blk, top_k):
    cnt = jnp.zeros(score.shape, jnp.float32)
    for m in range(score.shape[0]):
        row = score[m:m + 1, :]
        beats = jnp.where(row > score, 1.0, jnp.where((row == score) & (blk > m), 1.0, 0.0))
        cnt = cnt + beats
    return cnt < top_k


def _moba_kernel(rb_ref, q_ref, k_ref, v_ref, o_ref, bias_ref, kmean_ref, *, n_tiles):
    g = pl.program_id(1)
    t = pl.program_id(2)
    n_blk = kmean_ref.shape[0]

    @pl.when(t == 0)
    def _():
        _build_bias_tiles(rb_ref, g, bias_ref)
        kmean_ref[...] = jnp.zeros_like(kmean_ref)
        for n in range(n_tiles):
            kmean_ref[n:n + 1, :] = jnp.mean(k_ref[n * ATT_TILE:(n + 1) * ATT_TILE, :], axis=0, keepdims=True)

    qs = _stack_heads(q_ref)
    gate = lax.dot_general(kmean_ref[...].astype(jnp.bfloat16), qs, _NT, preferred_element_type=jnp.float32)
    blk = lax.broadcasted_iota(jnp.int32, gate.shape, 0)
    past = blk < t
    gate = jnp.where(past, gate, NEG_INF)
    chosen = _rank_select(gate, blk, MOBA_TOPK) & past
    sel = jnp.concatenate(
        [jnp.where(chosen, 1.0, 0.0), jnp.zeros((LANE - n_blk, HEAD_ROWS), jnp.float32)], axis=0)
    sel = sel.T.astype(jnp.bfloat16)

    row = lax.broadcasted_iota(jnp.int32, (HEAD_ROWS, ATT_TILE), 0) & (ATT_TILE - 1)
    col = lax.broadcasted_iota(jnp.int32, (HEAD_ROWS, ATT_TILE), 1)
    m, l, acc = _flash_first(qs, _kv_tile(k_ref, t), _kv_tile(v_ref, t), bias_ref[0], col <= row)

    def body(j, carry):
        pick = jnp.where(lax.broadcasted_iota(jnp.int32, (LANE, ATT_TILE), 0) == j, 1.0, 0.0).astype(jnp.bfloat16)
        mask = jnp.dot(sel, pick, preferred_element_type=jnp.float32) > 0.5
        bias = bias_ref[jnp.where(j == t - 1, 1, 2)]
        return _flash_next(qs, _kv_tile(k_ref, j), _kv_tile(v_ref, j), bias, mask, *carry)

    m, l, acc = lax.fori_loop(0, t, body, (m, l, acc))
    out = acc / jnp.maximum(l, 1e-30)
    for r in range(Q_PER_KV):
        o_ref[:, r * HEAD_DIM:(r + 1) * HEAD_DIM] = out[r * ATT_TILE:(r + 1) * ATT_TILE]


def _moba_attention(proj, rel_bias, n_batch, seq):
    n_tiles = seq // ATT_TILE
    k_col = Q_DIM // HEAD_DIM
    v_col = (Q_DIM + KV_DIM) // HEAD_DIM
    q_cols = Q_PER_KV * HEAD_DIM
    return pl.pallas_call(
        functools.partial(_moba_kernel, n_tiles=n_tiles),
        grid=(n_batch, N_KV_HEADS, n_tiles),
        in_specs=[
            pl.BlockSpec(memory_space=pltpu.SMEM),
            pl.BlockSpec((ATT_TILE, q_cols), lambda b, g, t: (b * n_tiles + t, g)),
            pl.BlockSpec((seq, HEAD_DIM), lambda b, g, t: (b, k_col + g)),
            pl.BlockSpec((seq, HEAD_DIM), lambda b, g, t: (b, v_col + g)),
        ],
        out_specs=pl.BlockSpec((ATT_TILE, q_cols), lambda b, g, t: (b * n_tiles + t, g)),
        out_shape=jax.ShapeDtypeStruct((n_batch * seq, Q_DIM), jnp.float32),
        scratch_shapes=[
            pltpu.VMEM((3, HEAD_ROWS, ATT_TILE), jnp.float32),
            pltpu.VMEM((max(n_tiles, 16), HEAD_DIM), jnp.float32),
        ],
        compiler_params=pltpu.CompilerParams(
            dimension_semantics=("parallel", "parallel", "arbitrary"),
            vmem_limit_bytes=VMEM_LIMIT),
    )(rel_bias, proj, proj, proj)


def _nsa_compress_kernel(k_ref, v_ref, pe_ref, w_ref, kc_ref, vc_ref, *, n_cmp):
    for c, (x_ref, o_ref) in enumerate(((k_ref, kc_ref), (v_ref, vc_ref))):
        top = jnp.zeros((n_cmp, HEAD_DIM), jnp.float32)
        bot = jnp.zeros((n_cmp, HEAD_DIM), jnp.float32)
        for r in range(CMP_STRIDE):
            x = x_ref[pl.ds(r, n_cmp, stride=CMP_STRIDE), :]
            xt = (x + pe_ref[c, r:r + 1, :]).astype(jnp.bfloat16)
            xb = (x + pe_ref[c, CMP_STRIDE + r:CMP_STRIDE + r + 1, :]).astype(jnp.bfloat16)
            top = top + jnp.dot(xt, w_ref[c, r * HEAD_DIM:(r + 1) * HEAD_DIM, :], preferred_element_type=jnp.float32)
            bot = bot + jnp.dot(xb, w_ref[c, (CMP_STRIDE + r) * HEAD_DIM:(CMP_STRIDE + r + 1) * HEAD_DIM, :],
                                preferred_element_type=jnp.float32)
        o_ref[0, 0] = top + pltpu.roll(bot, n_cmp - 1, axis=0)


def _nsa_compress_prompt(proj, cmp_pe, cmp_w, n_batch, seq):
    n_cmp = seq // CMP_STRIDE
    k_col = Q_DIM // HEAD_DIM
    v_col = (Q_DIM + KV_DIM) // HEAD_DIM
    out = jax.ShapeDtypeStruct((n_batch, N_KV_HEADS, n_cmp, HEAD_DIM), jnp.float32)
    return pl.pallas_call(
        functools.partial(_nsa_compress_kernel, n_cmp=n_cmp),
        grid=(n_batch, N_KV_HEADS),
        in_specs=[
            pl.BlockSpec((seq, HEAD_DIM), lambda b, g: (b, k_col + g)),
            pl.BlockSpec((seq, HEAD_DIM), lambda b, g: (b, v_col + g)),
            pl.BlockSpec((2, CMP_BLOCK, HEAD_DIM), lambda b, g: (0, 0, 0)),
            pl.BlockSpec((2, CMP_BLOCK * HEAD_DIM, HEAD_DIM), lambda b, g: (0, 0, 0)),
        ],
        out_specs=[pl.BlockSpec((1, 1, n_cmp, HEAD_DIM), lambda b, g: (b, g, 0, 0))] * 2,
        out_shape=[out, out],
        compiler_params=pltpu.CompilerParams(
            dimension_semantics=("parallel", "parallel"), vmem_limit_bytes=VMEM_LIMIT),
    )(proj, proj, cmp_pe, cmp_w)


def _nsa_kernel(rb_ref, q_ref, gate_ref, kc_ref, vc_ref, ks_ref, vs_ref, kw_ref, vw_ref, ovt_ref, o_ref,
                bias_ref, *, n_cmp):
    g = pl.program_id(1)
    t = pl.program_id(2)

    @pl.when(t == 0)
    def _():
        _build_bias_tiles(rb_ref, g, bias_ref)

    qs = _stack_heads(q_ref)
    row1 = lax.broadcasted_iota(jnp.int32, (ATT_TILE, ATT_TILE), 0)
    col1 = lax.broadcasted_iota(jnp.int32, (ATT_TILE, ATT_TILE), 1)

    q_pos = t * ATT_TILE + lax.broadcasted_iota(jnp.int32, (ATT_TILE, n_cmp), 0)
    c_end = lax.broadcasted_iota(jnp.int32, (ATT_TILE, n_cmp), 1) * CMP_STRIDE + (CMP_BLOCK - 1)
    ok_c = _tile4(c_end <= q_pos)
    bias_c = jnp.concatenate(_bias_from_dist(q_pos - c_end, _head_vals(rb_ref, g)), axis=0)
    lc = lax.dot_general(qs, kc_ref[0, 0].astype(jnp.bfloat16), _NT, preferred_element_type=jnp.float32) * SCALE + bias_c
    lc = jnp.where(ok_c, lc, NEG_INF)
    e = jnp.where(ok_c, jnp.exp(lc - jnp.max(lc, axis=-1, keepdims=True)), 0.0)
    pc = e / jnp.maximum(jnp.sum(e, axis=-1, keepdims=True), 1e-30)
    o_cmp = jnp.dot(pc.astype(jnp.bfloat16), vc_ref[0, 0].astype(jnp.bfloat16), preferred_element_type=jnp.float32)

    p_sum = pc[0:ATT_TILE]
    for r in range(1, Q_PER_KV):
        p_sum = p_sum + pc[r * ATT_TILE:(r + 1) * ATT_TILE]
    imp = lax.dot_general(ovt_ref[...], p_sum.astype(jnp.bfloat16), _NT, preferred_element_type=jnp.float32)
    blk = lax.broadcasted_iota(jnp.int32, (LANE, ATT_TILE), 0)
    own = t * (ATT_TILE // SEL_BLOCK) + (lax.broadcasted_iota(jnp.int32, (LANE, ATT_TILE), 1) // SEL_BLOCK)
    diff = own - blk
    valid = diff >= 0
    forced = valid & ((blk == 0) | (diff < N_LOCAL_SEL))
    score = jnp.where(forced, FORCE_SCORE, jnp.where(valid, imp, NEG_INF))
    chosen = _rank_select(score, blk, SEL_TOPK) & valid
    sel = jnp.where(chosen, 1.0, 0.0).T.astype(jnp.bfloat16)

    def sel_mask(j):
        pick = jnp.where(
            lax.broadcasted_iota(jnp.int32, (LANE, ATT_TILE), 0)
            == j * (ATT_TILE // SEL_BLOCK) + lax.broadcasted_iota(jnp.int32, (LANE, ATT_TILE), 1) // SEL_BLOCK,
            1.0, 0.0).astype(jnp.bfloat16)
        return jnp.dot(sel, pick, preferred_element_type=jnp.float32) > 0.5

    causal = col1 <= row1
    m, l, acc = _flash_first(qs, _kv_tile(ks_ref, t), _kv_tile(vs_ref, t), bias_ref[0],
                             _tile4(sel_mask(t) & causal))

    def body(j, carry):
        bias = bias_ref[jnp.where(j == t - 1, 1, 2)]
        return _flash_next(qs, _kv_tile(ks_ref, j), _kv_tile(vs_ref, j), bias, _tile4(sel_mask(j)), *carry)

    m, l, acc = lax.fori_loop(0, t, body, (m, l, acc))
    o_sel = acc / jnp.maximum(l, 1e-30)

    m, l, acc = _flash_first(qs, _kv_tile(kw_ref, t), _kv_tile(vw_ref, t), bias_ref[0], _tile4(causal))
    t1 = jnp.maximum(t - 1, 0)
    m, l, acc = _flash_next(qs, _kv_tile(kw_ref, t1), _kv_tile(vw_ref, t1), bias_ref[1],
                            _tile4((row1 >= 0) & (t >= 1)), m, l, acc)
    t2 = jnp.maximum(t - 2, 0)
    m, l, acc = _flash_next(qs, _kv_tile(kw_ref, t2), _kv_tile(vw_ref, t2), bias_ref[2],
                            _tile4((col1 > row1) & (t >= 2)), m, l, acc)
    o_win = acc / jnp.maximum(l, 1e-30)

    gates = jax.nn.sigmoid(gate_ref[...])
    lane = lax.broadcasted_iota(jnp.int32, gates.shape, 1)
    for r in range(Q_PER_KV):
        rows = slice(r * ATT_TILE, (r + 1) * ATT_TILE)
        out = jnp.zeros((ATT_TILE, HEAD_DIM), jnp.float32)
        for c, branch in enumerate((o_cmp, o_sel, o_win)):
            gcol = jnp.sum(jnp.where(lane == g * (3 * Q_PER_KV) + r * 3 + c, gates, 0.0), axis=-1, keepdims=True)
            out = out + gcol * branch[rows]
        o_ref[:, r * HEAD_DIM:(r + 1) * HEAD_DIM] = out


def _nsa_overlap_t(n_cmp):
    ci = np.arange(n_cmp)[None, :]
    sj = np.arange(LANE)[:, None]
    ov = (ci * CMP_STRIDE < (sj + 1) * SEL_BLOCK) & (ci * CMP_STRIDE + CMP_BLOCK > sj * SEL_BLOCK) & (ci < n_cmp - 1)
    return jnp.asarray(ov, jnp.bfloat16)


def _nsa_attention(proj, kc, vc, rel_bias, n_batch, seq):
    n_tiles = seq // ATT_TILE
    n_cmp = seq // CMP_STRIDE
    q_cols = Q_PER_KV * HEAD_DIM
    kv0 = Q_DIM // HEAD_DIM
    per = KV_DIM // HEAD_DIM
    gate_col = (Q_DIM + 6 * KV_DIM) // LANE

    def kv_spec(comp):
        return pl.BlockSpec((seq, HEAD_DIM), lambda b, g, t: (b, kv0 + comp * per + g))

    cmp_spec = pl.BlockSpec((1, 1, n_cmp, HEAD_DIM), lambda b, g, t: (b, g, 0, 0))
    return pl.pallas_call(
        functools.partial(_nsa_kernel, n_cmp=n_cmp),
        grid=(n_batch, N_KV_HEADS, n_tiles),
        in_specs=[
            pl.BlockSpec(memory_space=pltpu.SMEM),
            pl.BlockSpec((ATT_TILE, q_cols), lambda b, g, t: (b * n_tiles + t, g)),
            pl.BlockSpec((ATT_TILE, LANE), lambda b, g, t: (b * n_tiles + t, gate_col)),
            cmp_spec, cmp_spec,
            kv_spec(2), kv_spec(3), kv_spec(4), kv_spec(5),
            pl.BlockSpec((LANE, n_cmp), lambda b, g, t: (0, 0)),
        ],
        out_specs=pl.BlockSpec((ATT_TILE, q_cols), lambda b, g, t: (b * n_tiles + t, g)),
        out_shape=jax.ShapeDtypeStruct((n_batch * seq, Q_DIM), jnp.float32),
        scratch_shapes=[pltpu.VMEM((3, HEAD_ROWS, ATT_TILE), jnp.float32)],
        compiler_params=pltpu.CompilerParams(
            dimension_semantics=("parallel", "parallel", "arbitrary"),
            vmem_limit_bytes=VMEM_LIMIT),
    )(rel_bias, proj, proj, kc, vc, proj, proj, proj, proj, _nsa_overlap_t(n_cmp))


def _masked_softmax(logits, mask):
    logits = jnp.where(mask, logits.astype(jnp.float32), NEG_INF)
    m = jnp.max(logits, axis=-1, keepdims=True)
    e = jnp.where(mask, jnp.exp(logits - m), 0.0)
    return e / jnp.maximum(jnp.sum(e, axis=-1, keepdims=True), 1e-30)


def _rel_bucket(dist):
    n = jnp.maximum(dist, 0)
    log_ratio = jnp.log(jnp.maximum(n, 1).astype(jnp.float32) / REL_MAX_EXACT) / math.log(REL_MAX_DIST / REL_MAX_EXACT)
    large = jnp.minimum(REL_MAX_EXACT + (log_ratio * (REL_BUCKETS - REL_MAX_EXACT)).astype(jnp.int32), REL_BUCKETS - 1)
    return jnp.where(n < REL_MAX_EXACT, n, large)


def _shared_bias(rel_bias, q_pos, k_pos):
    bias = rel_bias[_rel_bucket(q_pos[:, None] - k_pos[None, :])]
    nq, nk = bias.shape[:2]
    return bias.reshape(nq, nk, N_KV_HEADS, Q_PER_KV).transpose(0, 2, 3, 1).astype(jnp.float32)


def _gathered_bias(rel_bias, dist):
    table = rel_bias.T.reshape(N_KV_HEADS, Q_PER_KV, REL_BUCKETS)
    g_idx = jnp.arange(N_KV_HEADS).reshape(1, 1, -1, 1, 1, 1)
    r_idx = jnp.arange(Q_PER_KV).reshape(1, 1, 1, -1, 1, 1)
    return table[g_idx, r_idx, _rel_bucket(dist)].astype(jnp.float32)


def _gather_pages(pool, page_table, comp):
    rows = pool[page_table, :, comp]
    return rows.reshape(page_table.shape[0], page_table.shape[1] * pool.shape[1], N_KV_HEADS, HEAD_DIM)


def _ssd_scan(x, dt, a, bmat, cmat, init_state):
    b, t, nh, p = x.shape
    l = min(SSD_CHUNK, t)
    nc = -(-t // l)
    pad = nc * l - t

    def padt(arr):
        return jnp.pad(arr.astype(jnp.float32), [(0, 0), (0, pad)] + [(0, 0)] * (arr.ndim - 2))

    xr = padt(x).reshape(b, nc, l, SSD_GROUPS, HEADS_PER_GROUP, p)
    dtr = padt(dt).reshape(b, nc, l, SSD_GROUPS, HEADS_PER_GROUP)
    br = padt(bmat).reshape(b, nc, l, SSD_GROUPS, D_STATE)
    cr = padt(cmat).reshape(b, nc, l, SSD_GROUPS, D_STATE)
    acum = jnp.cumsum(dtr * a.reshape(SSD_GROUPS, HEADS_PER_GROUP), axis=2)
    xdt = xr * dtr[..., None]
    causal = jnp.tril(jnp.ones((l, l), dtype=bool))[None, None, :, :, None, None]
    decay = jnp.exp(jnp.where(causal, acum[:, :, :, None] - acum[:, :, None, :], -jnp.inf))
    cb = jnp.einsum('bclgn,bcsgn->bclsg', cr, br)
    y_diag = jnp.einsum('bclsg,bclsgr,bcsgrp->bclgrp', cb, decay, xdt)
    chunk_states = jnp.einsum('bclgn,bclgr,bclgrp->bcgrpn', br, jnp.exp(acum[:, :, -1:] - acum), xdt)
    chunk_decay = jnp.exp(acum[:, :, -1])

    def step(s, inp):
        st, dc = inp
        return s * dc[..., None, None] + st, s

    s0 = init_state.astype(jnp.float32).reshape(b, SSD_GROUPS, HEADS_PER_GROUP, p, D_STATE)
    final, prev = lax.scan(step, s0, (jnp.moveaxis(chunk_states, 1, 0), jnp.moveaxis(chunk_decay, 1, 0)))
    prev = jnp.moveaxis(prev, 0, 1)
    y_off = jnp.einsum('bclgn,bcgrpn,bclgr->bclgrp', cr, prev, jnp.exp(acum))
    y = (y_diag + y_off).reshape(b, nc * l, nh, p)[:, :t]
    return y.astype(x.dtype), final.reshape(b, nh, p, D_STATE).astype(init_state.dtype)


def _ssd_core(proj, conv_state, ssm_state, conv_w, conv_b, dt_bias, a_log, d_skip, norm_w):
    b, t, _ = proj.shape
    z = proj[..., :D_INNER]
    xbc = proj[..., D_INNER:D_INNER + CONV_DIM]
    dt_raw = proj[..., D_INNER + CONV_DIM:D_INNER + CONV_DIM + SSD_HEADS]
    xbc_ext = jnp.concatenate([conv_state.astype(xbc.dtype), xbc], axis=1)
    conv = conv_b + sum(xbc_ext[:, k:k + t] * conv_w[k] for k in range(CONV_W))
    xbc = jax.nn.silu(conv)
    x = xbc[..., :D_INNER].reshape(b, t, SSD_HEADS, SSD_HEAD_DIM)
    bmat = xbc[..., D_INNER:D_INNER + GN].reshape(b, t, SSD_GROUPS, D_STATE)
    cmat = xbc[..., D_INNER + GN:].reshape(b, t, SSD_GROUPS, D_STATE)
    dt = jax.nn.softplus((dt_raw + dt_bias).astype(jnp.float32))
    a = -jnp.exp(a_log.astype(jnp.float32))
    y, new_ssm = _ssd_scan(x, dt, a, bmat, cmat, ssm_state)
    y = (y + x * d_skip[:, None]).reshape(b, t, D_INNER) * jax.nn.silu(z)
    yg = y.reshape(b, t, SSD_GROUPS, D_INNER // SSD_GROUPS).astype(jnp.float32)
    yg = yg * lax.rsqrt(jnp.mean(yg * yg, axis=-1, keepdims=True) + RMS_EPS)
    y = yg.reshape(b, t, D_INNER) * norm_w
    return y, xbc_ext[:, xbc_ext.shape[1] - (CONV_W - 1):], new_ssm


def _nsa_split(proj):
    b, t, _ = proj.shape
    q = proj[..., :Q_DIM].reshape(b, t, N_KV_HEADS, Q_PER_KV, HEAD_DIM)
    kv = proj[..., Q_DIM:Q_DIM + 6 * KV_DIM].reshape(b, t, 6, N_KV_HEADS, HEAD_DIM)
    gates = jax.nn.sigmoid(proj[..., Q_DIM + 6 * KV_DIM:NSA_IN_DIM]).reshape(b, t, N_KV_HEADS, Q_PER_KV, 3)
    return q, kv, gates


def _nsa_compress(k, pe, w):
    b, t = k.shape[:2]
    n_str = t // CMP_STRIDE
    ch = k[:, :n_str * CMP_STRIDE].reshape(b, n_str, CMP_STRIDE, N_KV_HEADS, HEAD_DIM)
    blk = jnp.concatenate([ch[:, :-1], ch[:, 1:]], axis=2) + pe[:, None, :]
    flat = blk.transpose(0, 1, 3, 2, 4).reshape(b, n_str - 1, N_KV_HEADS, CMP_BLOCK * HEAD_DIM)
    return flat @ w


def _nsa_branch_keys(k_cmp, v_cmp, k_sel, v_sel, cmp_pe, cmp_w):
    b, t = k_cmp.shape[:2]
    kc = _nsa_compress(k_cmp, cmp_pe[0], cmp_w[0])
    vc = _nsa_compress(v_cmp, cmp_pe[1], cmp_w[1])
    c_end = jnp.arange(kc.shape[1], dtype=jnp.int32) * CMP_STRIDE + (CMP_BLOCK - 1)
    n_sel = -(-t // SEL_BLOCK)
    pad = ((0, 0), (0, n_sel * SEL_BLOCK - t), (0, 0), (0, 0))
    ks = jnp.pad(k_sel, pad).reshape(b, n_sel, SEL_BLOCK, N_KV_HEADS, HEAD_DIM)
    vs = jnp.pad(v_sel, pad).reshape(b, n_sel, SEL_BLOCK, N_KV_HEADS, HEAD_DIM)
    return kc, vc, c_end, ks, vs


def _nsa_core(q, q_pos, kc, vc, c_end, ks_blk, vs_blk, kw, vw, w_pos, gates, rel_bias):
    bsz, nq = q.shape[:2]
    scale = HEAD_DIM ** -0.5
    tq = q_pos[:, None]
    lc = jnp.einsum('bqgrd,bkgd->bqgrk', q, kc) * scale + _shared_bias(rel_bias, q_pos, c_end)
    pc = _masked_softmax(lc, (c_end[None, :] <= tq)[None, :, None, None, :])
    o_cmp = jnp.einsum('bqgrk,bkgd->bqgrd', pc.astype(vc.dtype), vc)
    nc, ns = kc.shape[1], ks_blk.shape[1]
    ci = jnp.arange(nc)[:, None]
    sj = jnp.arange(ns)[None, :]
    overlap = ((ci * CMP_STRIDE < (sj + 1) * SEL_BLOCK) & (ci * CMP_STRIDE + CMP_BLOCK > sj * SEL_BLOCK)).astype(jnp.float32)
    imp = jnp.einsum('bqgk,kj->bqgj', jnp.sum(pc, axis=3), overlap)
    own = q_pos // SEL_BLOCK
    j = jnp.arange(ns)
    diff = own[:, None] - j[None, :]
    valid = diff >= 0
    forced = (j[None, :] == 0) | (valid & (diff < N_LOCAL_SEL))
    score = jnp.where((valid & forced)[None, :, None, :], FORCE_SCORE, jnp.where(valid[None, :, None, :], imp, NEG_INF))
    k_top = min(SEL_TOPK, ns)
    _, idx = lax.top_k(score, k_top)
    sel_ok = idx <= own[None, :, None, None]
    bi = jnp.arange(bsz).reshape(-1, 1, 1, 1)
    gi = jnp.arange(N_KV_HEADS).reshape(1, 1, -1, 1)
    kg = ks_blk[bi, idx, :, gi]
    vg = vs_blk[bi, idx, :, gi]
    pos = idx[..., None] * SEL_BLOCK + jnp.arange(SEL_BLOCK)
    tq5 = q_pos[None, :, None, None, None]
    ms = sel_ok[..., None] & (pos <= tq5)
    ls = jnp.einsum('bqgrd,bqgkcd->bqgrkc', q, kg) * scale + _gathered_bias(rel_bias, (tq5 - pos)[:, :, :, None])
    ps = _masked_softmax(ls.reshape(bsz, nq, N_KV_HEADS, Q_PER_KV, k_top * SEL_BLOCK),
                         ms[:, :, :, None].reshape(bsz, nq, N_KV_HEADS, 1, k_top * SEL_BLOCK))
    ps = ps.reshape(bsz, nq, N_KV_HEADS, Q_PER_KV, k_top, SEL_BLOCK).astype(vg.dtype)
    o_sel = jnp.einsum('bqgrkc,bqgkcd->bqgrd', ps, vg)
    lw = jnp.einsum('bqgrd,bkgd->bqgrk', q, kw) * scale + _shared_bias(rel_bias, q_pos, w_pos)
    dw = tq - w_pos[None, :]
    mw = ((dw >= 0) & (dw < WINDOW) & (w_pos[None, :] >= 0))[None, :, None, None, :]
    o_win = jnp.einsum('bqgrk,bkgd->bqgrd', _masked_softmax(lw, mw).astype(vw.dtype), vw)
    return gates[..., 0:1] * o_cmp + gates[..., 1:2] * o_sel + gates[..., 2:3] * o_win


def _nsa_prompt_core(proj, cmp_pe, cmp_w, rel_bias):
    b, t, _ = proj.shape
    q, kv, gates = _nsa_split(proj)
    kc, vc, c_end, ks, vs = _nsa_branch_keys(kv[:, :, 0], kv[:, :, 1], kv[:, :, 2], kv[:, :, 3], cmp_pe, cmp_w)
    wpad = ((0, 0), (WINDOW, 0), (0, 0), (0, 0))
    kw_pad, vw_pad = jnp.pad(kv[:, :, 4], wpad), jnp.pad(kv[:, :, 5], wpad)

    def block(i):
        s0 = i * Q_BLOCK
        q_pos = s0 + jnp.arange(Q_BLOCK, dtype=jnp.int32)
        w_pos = s0 - WINDOW + jnp.arange(WINDOW + Q_BLOCK, dtype=jnp.int32)
        qb = lax.dynamic_slice_in_dim(q, s0, Q_BLOCK, axis=1)
        gb = lax.dynamic_slice_in_dim(gates, s0, Q_BLOCK, axis=1)
        kw = lax.dynamic_slice_in_dim(kw_pad, s0, WINDOW + Q_BLOCK, axis=1)
        vw = lax.dynamic_slice_in_dim(vw_pad, s0, WINDOW + Q_BLOCK, axis=1)
        return _nsa_core(qb, q_pos, kc, vc, c_end, ks, vs, kw, vw, w_pos, gb, rel_bias)

    o = lax.map(block, jnp.arange(t // Q_BLOCK, dtype=jnp.int32))
    o = jnp.moveaxis(o, 0, 1).reshape(b, t, Q_DIM)
    wb = min(WINDOW, t)
    return o, kv[:, :, :4], kv[:, t - wb:, 4:]


def _nsa_sample_core(proj, cache_kv, win_buf, page_table, cmp_pe, cmp_w, rel_bias):
    b, t, _ = proj.shape
    q, kv, gates = _nsa_split(proj)
    full = [jnp.concatenate([_gather_pages(cache_kv, page_table, c).astype(kv.dtype), kv[:, :, c]], axis=1) for c in range(4)]
    kc, vc, c_end, ks, vs = _nsa_branch_keys(full[0], full[1], full[2], full[3], cmp_pe, cmp_w)
    wb = win_buf.shape[1]
    win = jnp.concatenate([win_buf.astype(kv.dtype), kv[:, :, 4:]], axis=1)
    w_pos = PAST_LEN - wb + jnp.arange(wb + t, dtype=jnp.int32)
    q_pos = PAST_LEN + jnp.arange(t, dtype=jnp.int32)
    o = _nsa_core(q, q_pos, kc, vc, c_end, ks, vs, win[:, :, 0], win[:, :, 1], w_pos, gates, rel_bias)
    return o.reshape(b, t, Q_DIM), kv[:, :, :4], win[:, t:]


def _moba_split(proj):
    b, t, _ = proj.shape
    q = proj[..., :Q_DIM].reshape(b, t, N_KV_HEADS, Q_PER_KV, HEAD_DIM)
    kv = proj[..., Q_DIM:].reshape(b, t, 2, N_KV_HEADS, HEAD_DIM)
    return q, kv


def _moba_blocks(k, v):
    b, t = k.shape[:2]
    nf = t // MOBA_BLOCK
    kb = k[:, :nf * MOBA_BLOCK].reshape(b, nf, MOBA_BLOCK, N_KV_HEADS, HEAD_DIM)
    vb = v[:, :nf * MOBA_BLOCK].reshape(b, nf, MOBA_BLOCK, N_KV_HEADS, HEAD_DIM)
    return kb, vb, jnp.mean(kb.astype(jnp.float32), axis=2)


def _moba_core(q, q_pos, kb, vb, kmean, own_k, own_v, own_pos, rel_bias):
    bsz, nq = q.shape[:2]
    scale = HEAD_DIM ** -0.5
    q_blk = q_pos // MOBA_BLOCK
    lo = jnp.einsum('bqgrd,bkgd->bqgrk', q, own_k) * scale + _shared_bias(rel_bias, q_pos, own_pos)
    mo = (own_pos[None, :] <= q_pos[:, None]) & ((own_pos // MOBA_BLOCK)[None, :] == q_blk[:, None])
    mo = jnp.broadcast_to(mo[None, :, None, None, :], lo.shape)
    nf = kmean.shape[1]
    k_top = min(MOBA_TOPK, nf)
    gate = jnp.einsum('bqgrd,bngd->bqgrn', q.astype(jnp.float32), kmean)
    past_ok = (jnp.arange(nf)[None, :] < q_blk[:, None])[None, :, None, None, :]
    _, idx = lax.top_k(jnp.where(past_ok, gate, NEG_INF), k_top)
    sel_ok = idx < q_blk[None, :, None, None, None]
    bi = jnp.arange(bsz).reshape(-1, 1, 1, 1, 1)
    gi = jnp.arange(N_KV_HEADS).reshape(1, 1, -1, 1, 1)
    kg = kb[bi, idx, :, gi]
    vg = vb[bi, idx, :, gi]
    pos = idx[..., None] * MOBA_BLOCK + jnp.arange(MOBA_BLOCK)
    lp = jnp.einsum('bqgrd,bqgrkcd->bqgrkc', q, kg) * scale + _gathered_bias(rel_bias, q_pos[None, :, None, None, None, None] - pos)
    mp = jnp.broadcast_to(sel_ok[..., None], lp.shape)
    n_past = k_top * MOBA_BLOCK
    logits = jnp.concatenate([lp.reshape(bsz, nq, N_KV_HEADS, Q_PER_KV, n_past), lo], axis=-1)
    mask = jnp.concatenate([mp.reshape(bsz, nq, N_KV_HEADS, Q_PER_KV, n_past), mo], axis=-1)
    p = _masked_softmax(logits, mask).astype(vg.dtype)
    pp = p[..., :n_past].reshape(bsz, nq, N_KV_HEADS, Q_PER_KV, k_top, MOBA_BLOCK)
    return jnp.einsum('bqgrkc,bqgrkcd->bqgrd', pp, vg) + jnp.einsum('bqgrk,bkgd->bqgrd', p[..., n_past:], own_v)


def _moba_prompt_core(proj, rel_bias):
    b, t, _ = proj.shape
    q, kv = _moba_split(proj)
    k, v = kv[:, :, 0], kv[:, :, 1]
    kb, vb, kmean = _moba_blocks(k, v)

    def block(i):
        s0 = i * Q_BLOCK
        q_pos = s0 + jnp.arange(Q_BLOCK, dtype=jnp.int32)
        own0 = (s0 // MOBA_BLOCK) * MOBA_BLOCK
        own_pos = own0 + jnp.arange(MOBA_BLOCK, dtype=jnp.int32)
        qb = lax.dynamic_slice_in_dim(q, s0, Q_BLOCK, axis=1)
        ok = lax.dynamic_slice_in_dim(k, own0, MOBA_BLOCK, axis=1)
        ov = lax.dynamic_slice_in_dim(v, own0, MOBA_BLOCK, axis=1)
        return _moba_core(qb, q_pos, kb, vb, kmean, ok, ov, own_pos, rel_bias)

    o = lax.map(block, jnp.arange(t // Q_BLOCK, dtype=jnp.int32))
    o = jnp.moveaxis(o, 0, 1).reshape(b, t, Q_DIM)
    return o, kv


def _moba_sample_core(proj, cache_kv, page_table, rel_bias):
    b, t, _ = proj.shape
    q, kv = _moba_split(proj)
    k = jnp.concatenate([_gather_pages(cache_kv, page_table, 0).astype(kv.dtype), kv[:, :, 0]], axis=1)
    v = jnp.concatenate([_gather_pages(cache_kv, page_table, 1).astype(kv.dtype), kv[:, :, 1]], axis=1)
    kb, vb, kmean = _moba_blocks(k, v)
    own0 = (PAST_LEN // MOBA_BLOCK) * MOBA_BLOCK
    own_pos = own0 + jnp.arange(k.shape[1] - own0, dtype=jnp.int32)
    q_pos = PAST_LEN + jnp.arange(t, dtype=jnp.int32)
    o = _moba_core(q, q_pos, kb, vb, kmean, k[:, own0:], v[:, own0:], own_pos, rel_bias)
    return o.reshape(b, t, Q_DIM), kv


def _pad_cols(w, n):
    return jnp.pad(w, ((0, 0), (0, n - w.shape[1])))


def _split_rows(a):
    return (a[:M_PROMPT].reshape(BATCH, SEQ, a.shape[-1]),
            a[M_PROMPT:].reshape(DEC_BATCH, DEC_SEQ, a.shape[-1]))


def _join_rows(p, s):
    return jnp.concatenate([p.reshape(M_PROMPT, p.shape[-1]), s.reshape(M_SAMPLE, s.shape[-1])], axis=0)


def kernel(x_prompt, x_sample, state_ssm, state_conv, cache_nsa_kv, cache_nsa_win, cache_moba_kv, page_table, rel_bias, norm_w, final_norm_w, ffn_w_gate, ffn_w_up, ffn_w_down, ssd_w_in, ssd_conv_w, ssd_conv_b, ssd_dt_bias, ssd_a_log, ssd_d, ssd_norm_w, ssd_w_out, nsa_w_in, nsa_cmp_pe, nsa_cmp_w, nsa_w_out, moba_w_in, moba_w_out):
    bf = jnp.bfloat16
    h = _join_rows(x_prompt, x_sample)
    ssm_p, ssm_s, conv_p, conv_s = [], [], [], []
    nsa_kv_p, nsa_kv_s, nsa_win_p, nsa_win_s = [], [], [], []
    moba_kv_p, moba_kv_s = [], []
    for i in range(DEPTH):
        kind, j = i % N_MIXERS, i // N_MIXERS
        h = _ffn_half(h, norm_w[i, 0], ffn_w_gate[i, 0].astype(bf), ffn_w_up[i, 0].astype(bf), ffn_w_down[i, 0].astype(bf))
        if kind == 0:
            n_pad = -(-SSD_IN_DIM // 512) * 512
            proj = _norm_proj(h, norm_w[i, 1], _pad_cols(ssd_w_in[j], n_pad).astype(bf))
            proj_p, proj_s = _split_rows(proj)
            w = (ssd_conv_w[j], ssd_conv_b[j], ssd_dt_bias[j], ssd_a_log[j], ssd_d[j], ssd_norm_w[j])
            conv0 = jnp.zeros((BATCH, CONV_W - 1, CONV_DIM), jnp.float32)
            ssm0 = jnp.zeros((BATCH, SSD_HEADS, SSD_HEAD_DIM, D_STATE), jnp.float32)
            y_p, c_p, s_p = _ssd_core(proj_p, conv0, ssm0, *w)
            y_s, c_s, s_s = _ssd_core(proj_s, state_conv[j], state_ssm[j], *w)
            conv_p.append(c_p)
            conv_s.append(c_s)
            ssm_p.append(s_p)
            ssm_s.append(s_s)
            w_out = ssd_w_out[j]
        elif kind == 1:
            n_pad = -(-NSA_IN_DIM // 512) * 512
            proj = _norm_proj(h, norm_w[i, 1], _pad_cols(nsa_w_in[j], n_pad).astype(bf))
            proj_p, proj_s = _split_rows(proj)
            kc, vc = _nsa_compress_prompt(proj, nsa_cmp_pe[j], nsa_cmp_w[j].astype(bf), BATCH, SEQ)
            y_p = _nsa_attention(proj, kc, vc, rel_bias, BATCH, SEQ)
            kv_all = proj_p[..., Q_DIM:Q_DIM + 6 * KV_DIM].reshape(BATCH, SEQ, 6, N_KV_HEADS, HEAD_DIM)
            kv_p, win_p = kv_all[:, :, :4], kv_all[:, SEQ - WINDOW:, 4:]
            y_s, kv_s, win_s = _nsa_sample_core(proj_s, cache_nsa_kv[j], cache_nsa_win[j], page_table,
                                                nsa_cmp_pe[j], nsa_cmp_w[j], rel_bias)
            nsa_kv_p.append(kv_p)
            nsa_kv_s.append(kv_s)
            nsa_win_p.append(win_p)
            nsa_win_s.append(win_s)
            w_out = nsa_w_out[j]
        else:
            proj = _norm_proj(h, norm_w[i, 1], moba_w_in[j].astype(bf))
            proj_p, proj_s = _split_rows(proj)
            y_p = _moba_attention(proj, rel_bias, BATCH, SEQ)
            kv_p = proj_p[..., Q_DIM:].reshape(BATCH, SEQ, 2, N_KV_HEADS, HEAD_DIM)
            y_s, kv_s = _moba_sample_core(proj_s, cache_moba_kv[j], page_table, rel_bias)
            moba_kv_p.append(kv_p)
            moba_kv_s.append(kv_s)
            w_out = moba_w_out[j]
        h = _proj_residual(_join_rows(y_p, y_s), w_out.astype(bf), h)
        h = _ffn_half(h, norm_w[i, 2], ffn_w_gate[i, 1].astype(bf), ffn_w_up[i, 1].astype(bf), ffn_w_down[i, 1].astype(bf))
    y = _final_norm(h, final_norm_w)
    y_prompt, y_sample = _split_rows(y)
    return (y_prompt, y_sample,
            jnp.stack(ssm_p), jnp.stack(ssm_s), jnp.stack(conv_p), jnp.stack(conv_s),
            jnp.stack(nsa_kv_p), jnp.stack(nsa_kv_s), jnp.stack(nsa_win_p), jnp.stack(nsa_win_s),
            jnp.stack(moba_kv_p), jnp.stack(moba_kv_s))
```

```python
import functools
import math

import jax
import jax.numpy as jnp
import numpy as np
from jax import lax
from jax.experimental import pallas as pl
from jax.experimental.pallas import tpu as pltpu

D_MODEL = 2048
BATCH = 2
SEQ = 4096
DEPTH = 4
DEC_BATCH = 8
DEC_SEQ = 8
PAST_LEN = 16384
PAGE_SIZE = 128
N_MIXERS = 3
D_FF = 5632
RMS_EPS = 1e-6
D_INNER = 2 * D_MODEL
SSD_HEAD_DIM = 64
SSD_HEADS = D_INNER // SSD_HEAD_DIM
D_STATE = 128
SSD_GROUPS = 8
HEADS_PER_GROUP = SSD_HEADS // SSD_GROUPS
GN = SSD_GROUPS * D_STATE
CONV_W = 4
CONV_DIM = D_INNER + 2 * GN
SSD_IN_DIM = D_INNER + CONV_DIM + SSD_HEADS
SSD_CHUNK = 128
HEAD_DIM = 128
N_HEADS = D_MODEL // HEAD_DIM
N_KV_HEADS = 4
Q_PER_KV = N_HEADS // N_KV_HEADS
Q_DIM = N_HEADS * HEAD_DIM
KV_DIM = N_KV_HEADS * HEAD_DIM
Q_BLOCK = 64
CMP_BLOCK = 32
CMP_STRIDE = 16
SEL_BLOCK = 64
SEL_TOPK = 16
N_LOCAL_SEL = 2
WINDOW = 512
NSA_IN_DIM = Q_DIM + 6 * KV_DIM + 3 * N_HEADS
MOBA_BLOCK = 256
MOBA_TOPK = 3
MOBA_IN_DIM = Q_DIM + 2 * KV_DIM
REL_BUCKETS = 32
REL_MAX_EXACT = REL_BUCKETS // 2
REL_MAX_DIST = 128
NEG_INF = -1e30
FORCE_SCORE = 1e9

M_PROMPT = BATCH * SEQ
M_SAMPLE = DEC_BATCH * DEC_SEQ
M_TOTAL = M_PROMPT + M_SAMPLE
ROW_TILE = 688
LANE = 128
VMEM_LIMIT = 56 * 1024 * 1024


def _rms_rows(x, g):
    ms = jnp.mean(x * x, axis=-1, keepdims=True)
    return x * lax.rsqrt(ms + RMS_EPS) * g


def _ffn_kernel(h_ref, g_ref, wg_ref, wu_ref, wd_ref, o_ref, xn_ref, acc_ref):
    f = pl.program_id(1)

    @pl.when(f == 0)
    def _():
        xn_ref[...] = _rms_rows(h_ref[...], g_ref[...]).astype(jnp.bfloat16)
        acc_ref[...] = jnp.zeros_like(acc_ref)

    xn = xn_ref[...]
    gate = jnp.dot(xn, wg_ref[...], preferred_element_type=jnp.float32)
    up = jnp.dot(xn, wu_ref[...], preferred_element_type=jnp.float32)
    act = (gate * jax.nn.sigmoid(gate) * up).astype(jnp.bfloat16)
    acc_ref[...] += jnp.dot(act, wd_ref[...], preferred_element_type=jnp.float32)

    @pl.when(f == pl.num_programs(1) - 1)
    def _():
        o_ref[...] = h_ref[...] + 0.5 * acc_ref[...]


def _ffn_half(h, g, wg, wu, wd, *, ff_tile=512):
    m, d = h.shape
    dff = wg.shape[1]
    grid = (m // ROW_TILE, dff // ff_tile)
    return pl.pallas_call(
        _ffn_kernel,
        grid=grid,
        in_specs=[
            pl.BlockSpec((ROW_TILE, d), lambda i, f: (i, 0)),
            pl.BlockSpec((1, d), lambda i, f: (0, 0)),
            pl.BlockSpec((d, ff_tile), lambda i, f: (0, f)),
            pl.BlockSpec((d, ff_tile), lambda i, f: (0, f)),
            pl.BlockSpec((ff_tile, d), lambda i, f: (f, 0)),
        ],
        out_specs=pl.BlockSpec((ROW_TILE, d), lambda i, f: (i, 0)),
        out_shape=jax.ShapeDtypeStruct((m, d), jnp.float32),
        scratch_shapes=[
            pltpu.VMEM((ROW_TILE, d), jnp.bfloat16),
            pltpu.VMEM((ROW_TILE, d), jnp.float32),
        ],
        compiler_params=pltpu.CompilerParams(
            dimension_semantics=("parallel", "arbitrary"),
            vmem_limit_bytes=VMEM_LIMIT),
    )(h, g.reshape(1, d), wg, wu, wd)


def _norm_proj_kernel(h_ref, g_ref, w_ref, o_ref, xn_ref):
    @pl.when(pl.program_id(1) == 0)
    def _():
        xn_ref[...] = _rms_rows(h_ref[...], g_ref[...]).astype(jnp.bfloat16)

    o_ref[...] = jnp.dot(xn_ref[...], w_ref[...], preferred_element_type=jnp.float32)


def _norm_proj(h, g, w, *, n_tile=512):
    m, d = h.shape
    n = w.shape[1]
    grid = (m // ROW_TILE, n // n_tile)
    return pl.pallas_call(
        _norm_proj_kernel,
        grid=grid,
        in_specs=[
            pl.BlockSpec((ROW_TILE, d), lambda i, j: (i, 0)),
            pl.BlockSpec((1, d), lambda i, j: (0, 0)),
            pl.BlockSpec((d, n_tile), lambda i, j: (0, j)),
        ],
        out_specs=pl.BlockSpec((ROW_TILE, n_tile), lambda i, j: (i, j)),
        out_shape=jax.ShapeDtypeStruct((m, n), jnp.float32),
        scratch_shapes=[pltpu.VMEM((ROW_TILE, d), jnp.bfloat16)],
        compiler_params=pltpu.CompilerParams(
            dimension_semantics=("parallel", "arbitrary"),
            vmem_limit_bytes=VMEM_LIMIT),
    )(h, g.reshape(1, d), w)


def _proj_residual_kernel(y_ref, w_ref, h_ref, o_ref):
    o_ref[...] = h_ref[...] + jnp.dot(
        y_ref[...].astype(jnp.bfloat16), w_ref[...], preferred_element_type=jnp.float32)


def _proj_residual(y, w, h, *, n_tile=512):
    m, k = y.shape
    d = w.shape[1]
    grid = (m // ROW_TILE, d // n_tile)
    return pl.pallas_call(
        _proj_residual_kernel,
        grid=grid,
        in_specs=[
            pl.BlockSpec((ROW_TILE, k), lambda i, j: (i, 0)),
            pl.BlockSpec((k, n_tile), lambda i, j: (0, j)),
            pl.BlockSpec((ROW_TILE, n_tile), lambda i, j: (i, j)),
        ],
        out_specs=pl.BlockSpec((ROW_TILE, n_tile), lambda i, j: (i, j)),
        out_shape=jax.ShapeDtypeStruct((m, d), jnp.float32),
        compiler_params=pltpu.CompilerParams(
            dimension_semantics=("parallel", "arbitrary"),
            vmem_limit_bytes=VMEM_LIMIT),
    )(y, w, h)


def _final_norm_kernel(h_ref, g_ref, o_ref):
    o_ref[...] = _rms_rows(h_ref[...], g_ref[...])


def _final_norm(h, g):
    m, d = h.shape
    return pl.pallas_call(
        _final_norm_kernel,
        grid=(m // ROW_TILE,),
        in_specs=[pl.BlockSpec((ROW_TILE, d), lambda i: (i, 0)),
                  pl.BlockSpec((1, d), lambda i: (0, 0))],
        out_specs=pl.BlockSpec((ROW_TILE, d), lambda i: (i, 0)),
        out_shape=jax.ShapeDtypeStruct((m, d), jnp.float32),
        compiler_params=pltpu.CompilerParams(dimension_semantics=("parallel",)),
    )(h, g.reshape(1, d))


ATT_TILE = 256
HEAD_ROWS = Q_PER_KV * ATT_TILE
SCALE = HEAD_DIM ** -0.5
_NT = (((1,), (1,)), ((), ()))


def _bucket_upper_bounds():
    n = np.arange(0, 2 * REL_MAX_DIST)
    ratio = np.log(np.maximum(n, 1).astype(np.float32) / np.float32(REL_MAX_EXACT)) / np.float32(
        math.log(REL_MAX_DIST / REL_MAX_EXACT))
    large = np.minimum(REL_MAX_EXACT + (ratio * np.float32(REL_BUCKETS - REL_MAX_EXACT)).astype(np.int32),
                       REL_BUCKETS - 1)
    bucket = np.where(n < REL_MAX_EXACT, n, large)
    return [int(n[bucket > b].min()) for b in range(REL_BUCKETS - 1)]


_BUCKET_HI = _bucket_upper_bounds()


def _bias_from_dist(dist, head_vals):
    accs = [jnp.zeros(dist.shape, jnp.float32) + vals[REL_BUCKETS - 1] for vals in head_vals]
    for b in range(REL_BUCKETS - 2, -1, -1):
        near = dist < _BUCKET_HI[b]
        accs = [jnp.where(near, vals[b], acc) for vals, acc in zip(head_vals, accs)]
    return accs


def _head_vals(rb_ref, g):
    return [[rb_ref[b, g * Q_PER_KV + r] for b in range(REL_BUCKETS)] for r in range(Q_PER_KV)]


def _build_bias_tiles(rb_ref, g, bias_ref):
    row = lax.broadcasted_iota(jnp.int32, (ATT_TILE, ATT_TILE), 0)
    col = lax.broadcasted_iota(jnp.int32, (ATT_TILE, ATT_TILE), 1)
    vals = _head_vals(rb_ref, g)
    for kind in range(2):
        tiles = _bias_from_dist(row - col + kind * ATT_TILE, vals)
        for r in range(Q_PER_KV):
            bias_ref[kind, r * ATT_TILE:(r + 1) * ATT_TILE, :] = tiles[r]
    for r in range(Q_PER_KV):
        bias_ref[2, r * ATT_TILE:(r + 1) * ATT_TILE, :] = (
            jnp.zeros((ATT_TILE, ATT_TILE), jnp.float32) + vals[r][REL_BUCKETS - 1])


def _stack_heads(q_ref):
    return jnp.concatenate(
        [q_ref[:, r * HEAD_DIM:(r + 1) * HEAD_DIM] for r in range(Q_PER_KV)], axis=0).astype(jnp.bfloat16)


def _tile4(x):
    return jnp.concatenate([x] * Q_PER_KV, axis=0)


def _flash_first(qs, k_tile, v_tile, bias, mask):
    s = lax.dot_general(qs, k_tile, _NT, preferred_element_type=jnp.float32) * SCALE + bias
    s = jnp.where(mask, s, NEG_INF)
    m = jnp.max(s, axis=-1, keepdims=True)
    p = jnp.exp(s - m)
    l = jnp.sum(p, axis=-1, keepdims=True)
    acc = jnp.dot(p.astype(jnp.bfloat16), v_tile, preferred_element_type=jnp.float32)
    return m, l, acc


def _flash_next(qs, k_tile, v_tile, bias, mask, m, l, acc):
    s = lax.dot_general(qs, k_tile, _NT, preferred_element_type=jnp.float32) * SCALE + bias
    s = jnp.where(mask, s, NEG_INF)
    m_new = jnp.maximum(m, jnp.max(s, axis=-1, keepdims=True))
    alpha = jnp.exp(m - m_new)
    p = jnp.exp(s - m_new)
    l = alpha * l + jnp.sum(p, axis=-1, keepdims=True)
    acc = alpha * acc + jnp.dot(p.astype(jnp.bfloat16), v_tile, preferred_element_type=jnp.float32)
    return m_new, l, acc


def _kv_tile(ref, j):
    return ref[pl.ds(pl.multiple_of(j * ATT_TILE, ATT_TILE), ATT_TILE), :].astype(jnp.bfloat16)


def _rank_select(score, blk, top_k):
    cnt = jnp.zeros(score.shape, jnp.float32)
    for m in range(score.shape[0]):
        row = score[m:m + 1, :]
        beats = jnp.where(row > score, 1.0, jnp.where((row == score) & (blk > m), 1.0, 0.0))
        cnt = cnt + beats
    return cnt < top_k


def _moba_kernel(rb_ref, q_ref, k_ref, v_ref, o_ref, bias_ref, kmean_ref, *, n_tiles):
    g = pl.program_id(1)
    t = pl.program_id(2)
    n_blk = kmean_ref.shape[0]

    @pl.when(t == 0)
    def _():
        _build_bias_tiles(rb_ref, g, bias_ref)
        kmean_ref[...] = jnp.zeros_like(kmean_ref)
        for n in range(n_tiles):
            kmean_ref[n:n + 1, :] = jnp.mean(k_ref[n * ATT_TILE:(n + 1) * ATT_TILE, :], axis=0, keepdims=True)

    qs = _stack_heads(q_ref)
    gate = lax.dot_general(kmean_ref[...].astype(jnp.bfloat16), qs, _NT, preferred_element_type=jnp.float32)
    blk = lax.broadcasted_iota(jnp.int32, gate.shape, 0)
    past = blk < t
    gate = jnp.where(past, gate, NEG_INF)
    chosen = _rank_select(gate, blk, MOBA_TOPK) & past
    sel = jnp.concatenate(
        [jnp.where(chosen, 1.0, 0.0), jnp.zeros((LANE - n_blk, HEAD_ROWS), jnp.float32)], axis=0)
    sel = sel.T.astype(jnp.bfloat16)

    row = lax.broadcasted_iota(jnp.int32, (HEAD_ROWS, ATT_TILE), 0) & (ATT_TILE - 1)
    col = lax.broadcasted_iota(jnp.int32, (HEAD_ROWS, ATT_TILE), 1)
    m, l, acc = _flash_first(qs, _kv_tile(k_ref, t), _kv_tile(v_ref, t), bias_ref[0], col <= row)

    def body(j, carry):
        pick = jnp.where(lax.broadcasted_iota(jnp.int32, (LANE, ATT_TILE), 0) == j, 1.0, 0.0).astype(jnp.bfloat16)
        mask = jnp.dot(sel, pick, preferred_element_type=jnp.float32) > 0.5
        bias = bias_ref[jnp.where(j == t - 1, 1, 2)]
        return _flash_next(qs, _kv_tile(k_ref, j), _kv_tile(v_ref, j), bias, mask, *carry)

    m, l, acc = lax.fori_loop(0, t, body, (m, l, acc))
    out = acc / jnp.maximum(l, 1e-30)
    for r in range(Q_PER_KV):
        o_ref[:, r * HEAD_DIM:(r + 1) * HEAD_DIM] = out[r * ATT_TILE:(r + 1) * ATT_TILE]


def _moba_attention(proj, rel_bias, n_batch, seq):
    n_tiles = seq // ATT_TILE
    k_col = Q_DIM // HEAD_DIM
    v_col = (Q_DIM + KV_DIM) // HEAD_DIM
    q_cols = Q_PER_KV * HEAD_DIM
    return pl.pallas_call(
        functools.partial(_moba_kernel, n_tiles=n_tiles),
        grid=(n_batch, N_KV_HEADS, n_tiles),
        in_specs=[
            pl.BlockSpec(memory_space=pltpu.SMEM),
            pl.BlockSpec((ATT_TILE, q_cols), lambda b, g, t: (b * n_tiles + t, g)),
            pl.BlockSpec((seq, HEAD_DIM), lambda b, g, t: (b, k_col + g)),
            pl.BlockSpec((seq, HEAD_DIM), lambda b, g, t: (b, v_col + g)),
        ],
        out_specs=pl.BlockSpec((ATT_TILE, q_cols), lambda b, g, t: (b * n_tiles + t, g)),
        out_shape=jax.ShapeDtypeStruct((n_batch * seq, Q_DIM), jnp.float32),
        scratch_shapes=[
            pltpu.VMEM((3, HEAD_ROWS, ATT_TILE), jnp.float32),
            pltpu.VMEM((max(n_tiles, 16), HEAD_DIM), jnp.float32),
        ],
        compiler_params=pltpu.CompilerParams(
            dimension_semantics=("parallel", "parallel", "arbitrary"),
            vmem_limit_bytes=VMEM_LIMIT),
    )(rel_bias, proj, proj, proj)


def _nsa_compress_kernel(k_ref, v_ref, pe_ref, w_ref, kc_ref, vc_ref, *, n_cmp):
    for c, (x_ref, o_ref) in enumerate(((k_ref, kc_ref), (v_ref, vc_ref))):
        top = jnp.zeros((n_cmp, HEAD_DIM), jnp.float32)
        bot = jnp.zeros((n_cmp, HEAD_DIM), jnp.float32)
        for r in range(CMP_STRIDE):
            x = x_ref[pl.ds(r, n_cmp, stride=CMP_STRIDE), :]
            xt = (x + pe_ref[c, r:r + 1, :]).astype(jnp.bfloat16)
            xb = (x + pe_ref[c, CMP_STRIDE + r:CMP_STRIDE + r + 1, :]).astype(jnp.bfloat16)
            top = top + jnp.dot(xt, w_ref[c, r * HEAD_DIM:(r + 1) * HEAD_DIM, :], preferred_element_type=jnp.float32)
            bot = bot + jnp.dot(xb, w_ref[c, (CMP_STRIDE + r) * HEAD_DIM:(CMP_STRIDE + r + 1) * HEAD_DIM, :],
                                preferred_element_type=jnp.float32)
        o_ref[0, 0] = top + pltpu.roll(bot, n_cmp - 1, axis=0)


def _nsa_compress_prompt(proj, cmp_pe, cmp_w, n_batch, seq):
    n_cmp = seq // CMP_STRIDE
    k_col = Q_DIM // HEAD_DIM
    v_col = (Q_DIM + KV_DIM) // HEAD_DIM
    out = jax.ShapeDtypeStruct((n_batch, N_KV_HEADS, n_cmp, HEAD_DIM), jnp.float32)
    return pl.pallas_call(
        functools.partial(_nsa_compress_kernel, n_cmp=n_cmp),
        grid=(n_batch, N_KV_HEADS),
        in_specs=[
            pl.BlockSpec((seq, HEAD_DIM), lambda b, g: (b, k_col + g)),
            pl.BlockSpec((seq, HEAD_DIM), lambda b, g: (b, v_col + g)),
            pl.BlockSpec((2, CMP_BLOCK, HEAD_DIM), lambda b, g: (0, 0, 0)),
            pl.BlockSpec((2, CMP_BLOCK * HEAD_DIM, HEAD_DIM), lambda b, g: (0, 0, 0)),
        ],
        out_specs=[pl.BlockSpec((1, 1, n_cmp, HEAD_DIM), lambda b, g: (b, g, 0, 0))] * 2,
        out_shape=[out, out],
        compiler_params=pltpu.CompilerParams(
            dimension_semantics=("parallel", "parallel"), vmem_limit_bytes=VMEM_LIMIT),
    )(proj, proj, cmp_pe, cmp_w)


def _nsa_kernel(rb_ref, q_ref, gate_ref, kc_ref, vc_ref, ks_ref, vs_ref, kw_ref, vw_ref, ovt_ref, o_ref,
                bias_ref, *, n_cmp):
    g = pl.program_id(1)
    t = pl.program_id(2)

    @pl.when(t == 0)
    def _():
        _build_bias_tiles(rb_ref, g, bias_ref)

    qs = _stack_heads(q_ref)
    row1 = lax.broadcasted_iota(jnp.int32, (ATT_TILE, ATT_TILE), 0)
    col1 = lax.broadcasted_iota(jnp.int32, (ATT_TILE, ATT_TILE), 1)

    q_pos = t * ATT_TILE + lax.broadcasted_iota(jnp.int32, (ATT_TILE, n_cmp), 0)
    c_end = lax.broadcasted_iota(jnp.int32, (ATT_TILE, n_cmp), 1) * CMP_STRIDE + (CMP_BLOCK - 1)
    ok_c = _tile4(c_end <= q_pos)
    bias_c = jnp.concatenate(_bias_from_dist(q_pos - c_end, _head_vals(rb_ref, g)), axis=0)
    lc = lax.dot_general(qs, kc_ref[0, 0].astype(jnp.bfloat16), _NT, preferred_element_type=jnp.float32) * SCALE + bias_c
    lc = jnp.where(ok_c, lc, NEG_INF)
    e = jnp.where(ok_c, jnp.exp(lc - jnp.max(lc, axis=-1, keepdims=True)), 0.0)
    pc = e / jnp.maximum(jnp.sum(e, axis=-1, keepdims=True), 1e-30)
    o_cmp = jnp.dot(pc.astype(jnp.bfloat16), vc_ref[0, 0].astype(jnp.bfloat16), preferred_element_type=jnp.float32)

    p_sum = pc[0:ATT_TILE]
    for r in range(1, Q_PER_KV):
        p_sum = p_sum + pc[r * ATT_TILE:(r + 1) * ATT_TILE]
    imp = lax.dot_general(ovt_ref[...], p_sum.astype(jnp.bfloat16), _NT, preferred_element_type=jnp.float32)
    blk = lax.broadcasted_iota(jnp.int32, (LANE, ATT_TILE), 0)
    own = t * (ATT_TILE // SEL_BLOCK) + (lax.broadcasted_iota(jnp.int32, (LANE, ATT_TILE), 1) // SEL_BLOCK)
    diff = own - blk
    valid = diff >= 0
    forced = valid & ((blk == 0) | (diff < N_LOCAL_SEL))
    score = jnp.where(forced, FORCE_SCORE, jnp.where(valid, imp, NEG_INF))
    chosen = _rank_select(score, blk, SEL_TOPK) & valid
    sel = jnp.where(chosen, 1.0, 0.0).T.astype(jnp.bfloat16)

    def sel_mask(j):
        pick = jnp.where(
            lax.broadcasted_iota(jnp.int32, (LANE, ATT_TILE), 0)
            == j * (ATT_TILE // SEL_BLOCK) + lax.broadcasted_iota(jnp.int32, (LANE, ATT_TILE), 1) // SEL_BLOCK,
            1.0, 0.0).astype(jnp.bfloat16)
        return jnp.dot(sel, pick, preferred_element_type=jnp.float32) > 0.5

    causal = col1 <= row1
    m, l, acc = _flash_first(qs, _kv_tile(ks_ref, t), _kv_tile(vs_ref, t), bias_ref[0],
                             _tile4(sel_mask(t) & causal))

    def body(j, carry):
        bias = bias_ref[jnp.where(j == t - 1, 1, 2)]
        return _flash_next(qs, _kv_tile(ks_ref, j), _kv_tile(vs_ref, j), bias, _tile4(sel_mask(j)), *carry)

    m, l, acc = lax.fori_loop(0, t, body, (m, l, acc))
    o_sel = acc / jnp.maximum(l, 1e-30)

    m, l, acc = _flash_first(qs, _kv_tile(kw_ref, t), _kv_tile(vw_ref, t), bias_ref[0], _tile4(causal))
    t1 = jnp.maximum(t - 1, 0)
    m, l, acc = _flash_next(qs, _kv_tile(kw_ref, t1), _kv_tile(vw_ref, t1), bias_ref[1],
                            _tile4((row1 >= 0) & (t >= 1)), m, l, acc)
    t2 = jnp.maximum(t - 2, 0)
    m, l, acc = _flash_next(qs, _kv_tile(kw_ref, t2), _kv_tile(vw_ref, t2), bias_ref[2],
                            _tile4((col1 > row1) & (t >= 2)), m, l, acc)
    o_win = acc / jnp.maximum(l, 1e-30)

    gates = jax.nn.sigmoid(gate_ref[...])
    lane = lax.broadcasted_iota(jnp.int32, gates.shape, 1)
    for r in range(Q_PER_KV):
        rows = slice(r * ATT_TILE, (r + 1) * ATT_TILE)
        out = jnp.zeros((ATT_TILE, HEAD_DIM), jnp.float32)
        for c, branch in enumerate((o_cmp, o_sel, o_win)):
            gcol = jnp.sum(jnp.where(lane == g * (3 * Q_PER_KV) + r * 3 + c, gates, 0.0), axis=-1, keepdims=True)
            out = out + gcol * branch[rows]
        o_ref[:, r * HEAD_DIM:(r + 1) * HEAD_DIM] = out


def _nsa_overlap_t(n_cmp):
    ci = np.arange(n_cmp)[None, :]
    sj = np.arange(LANE)[:, None]
    ov = (ci * CMP_STRIDE < (sj + 1) * SEL_BLOCK) & (ci * CMP_STRIDE + CMP_BLOCK > sj * SEL_BLOCK) & (ci < n_cmp - 1)
    return jnp.asarray(ov, jnp.bfloat16)


def _nsa_attention(proj, kc, vc, rel_bias, n_batch, seq):
    n_tiles = seq // ATT_TILE
    n_cmp = seq // CMP_STRIDE
    q_cols = Q_PER_KV * HEAD_DIM
    kv0 = Q_DIM // HEAD_DIM
    per = KV_DIM // HEAD_DIM
    gate_col = (Q_DIM + 6 * KV_DIM) // LANE

    def kv_spec(comp):
        return pl.BlockSpec((seq, HEAD_DIM), lambda b, g, t: (b, kv0 + comp * per + g))

    cmp_spec = pl.BlockSpec((1, 1, n_cmp, HEAD_DIM), lambda b, g, t: (b, g, 0, 0))
    return pl.pallas_call(
        functools.partial(_nsa_kernel, n_cmp=n_cmp),
        grid=(n_batch, N_KV_HEADS, n_tiles),
        in_specs=[
            pl.BlockSpec(memory_space=pltpu.SMEM),
            pl.BlockSpec((ATT_TILE, q_cols), lambda b, g, t: (b * n_tiles + t, g)),
            pl.BlockSpec((ATT_TILE, LANE), lambda b, g, t: (b * n_tiles + t, gate_col)),
            cmp_spec, cmp_spec,
            kv_spec(2), kv_spec(3), kv_spec(4), kv_spec(5),
            pl.BlockSpec((LANE, n_cmp), lambda b, g, t: (0, 0)),
        ],
        out_specs=pl.BlockSpec((ATT_TILE, q_cols), lambda b, g, t: (b * n_tiles + t, g)),
        out_shape=jax.ShapeDtypeStruct((n_batch * seq, Q_DIM), jnp.float32),
        scratch_shapes=[pltpu.VMEM((3, HEAD_ROWS, ATT_TILE), jnp.float32)],
        compiler_params=pltpu.CompilerParams(
            dimension_semantics=("parallel", "parallel", "arbitrary"),
            vmem_limit_bytes=VMEM_LIMIT),
    )(rel_bias, proj, proj, kc, vc, proj, proj, proj, proj, _nsa_overlap_t(n_cmp))


NEW_ROWS = Q_PER_KV * DEC_SEQ


def _rank_select_rows(score, n_cand, top_k):
    lane = lax.broadcasted_iota(jnp.int32, score.shape, 1)
    cnt = jnp.zeros(score.shape, jnp.float32)
    for m in range(n_cand):
        col = score[:, m:m + 1]
        cnt = cnt + jnp.where(col > score, 1.0, jnp.where((col == score) & (lane > m), 1.0, 0.0))
    return cnt < top_k


def _stack_group_heads(q_ref, g):
    return jnp.concatenate(
        [q_ref[:, (g * Q_PER_KV + r) * HEAD_DIM:(g * Q_PER_KV + r + 1) * HEAD_DIM] for r in range(Q_PER_KV)],
        axis=0).astype(jnp.bfloat16)


def _group_bias(rb_ref, g, dist):
    return jnp.concatenate(_bias_from_dist(dist, _head_vals(rb_ref, g)), axis=0)


def _flash_update(refs, g, qs, k_tile, v_tile, bias, mask, first):
    m_ref, l_ref, acc_ref = refs
    if first:
        m, l, acc = _flash_first(qs, k_tile, v_tile, bias, mask)
    else:
        m, l, acc = _flash_next(qs, k_tile, v_tile, bias, mask, m_ref[g], l_ref[g], acc_ref[g])
    m_ref[g] = m
    l_ref[g] = l
    acc_ref[g] = acc


def _moba_kmean_kernel(pt_ref, ka_ref, kb_ref, o_ref):
    o_ref[...] = (jnp.sum(ka_ref[...], axis=0) + jnp.sum(kb_ref[...], axis=0)) * (1.0 / MOBA_BLOCK)


def _page_spec(comp, which, per_step):
    return pl.BlockSpec((None, PAGE_SIZE, None, N_KV_HEADS, HEAD_DIM),
                        lambda b, n, pt: (pt[b, n * per_step + which], 0, comp, 0, 0))


def _moba_sample_kmean(cache, page_table):
    n_seq, n_pages = page_table.shape
    n_blk = n_pages * PAGE_SIZE // MOBA_BLOCK
    return pl.pallas_call(
        _moba_kmean_kernel,
        grid_spec=pltpu.PrefetchScalarGridSpec(
            num_scalar_prefetch=1,
            grid=(n_seq, n_blk),
            in_specs=[_page_spec(0, 0, 2), _page_spec(0, 1, 2)],
            out_specs=pl.BlockSpec((None, None, N_KV_HEADS, HEAD_DIM), lambda b, n, pt: (b, n, 0, 0)),
        ),
        out_shape=jax.ShapeDtypeStruct((n_seq, n_blk, N_KV_HEADS, HEAD_DIM), jnp.float32),
        compiler_params=pltpu.CompilerParams(dimension_semantics=("parallel", "arbitrary")),
    )(page_table, cache, cache)


def _moba_sample_kernel(pt_ref, rb_ref, q_ref, kn_ref, vn_ref, kmean_ref, ka_ref, kb_ref, va_ref, vb_ref, o_ref,
                        m_ref, l_ref, acc_ref, sel_ref, *, n_blk, past_len):
    n = pl.program_id(1)
    refs = (m_ref, l_ref, acc_ref)
    tok = lax.broadcasted_iota(jnp.int32, (DEC_SEQ, DEC_SEQ), 0)
    new = lax.broadcasted_iota(jnp.int32, (DEC_SEQ, DEC_SEQ), 1)

    @pl.when(n == 0)
    def _():
        for g in range(N_KV_HEADS):
            qs = _stack_group_heads(q_ref, g)
            gate = lax.dot_general(qs, kmean_ref[g].astype(jnp.bfloat16), _NT, preferred_element_type=jnp.float32)
            gate = jnp.concatenate([gate, jnp.full((NEW_ROWS, LANE - n_blk), NEG_INF, jnp.float32)], axis=1)
            lane = lax.broadcasted_iota(jnp.int32, gate.shape, 1)
            chosen = _rank_select_rows(gate, n_blk, MOBA_TOPK) & (lane < n_blk)
            sel_ref[g] = jnp.where(chosen, 1.0, 0.0)
            _flash_update(refs, g, qs, kn_ref[:, g * HEAD_DIM:(g + 1) * HEAD_DIM].astype(jnp.bfloat16),
                          vn_ref[:, g * HEAD_DIM:(g + 1) * HEAD_DIM].astype(jnp.bfloat16),
                          _group_bias(rb_ref, g, tok - new), _tile4(new <= tok), True)

    q_pos = past_len + lax.broadcasted_iota(jnp.int32, (DEC_SEQ, MOBA_BLOCK), 0)
    k_pos = n * MOBA_BLOCK + lax.broadcasted_iota(jnp.int32, (DEC_SEQ, MOBA_BLOCK), 1)
    pick = jnp.where(lax.broadcasted_iota(jnp.int32, (LANE, MOBA_BLOCK), 0) == n, 1.0, 0.0).astype(jnp.bfloat16)
    for g in range(N_KV_HEADS):
        qs = _stack_group_heads(q_ref, g)
        k_tile = jnp.concatenate([ka_ref[:, g, :], kb_ref[:, g, :]], axis=0).astype(jnp.bfloat16)
        v_tile = jnp.concatenate([va_ref[:, g, :], vb_ref[:, g, :]], axis=0).astype(jnp.bfloat16)
        mask = jnp.dot(sel_ref[g].astype(jnp.bfloat16), pick, preferred_element_type=jnp.float32) > 0.5
        _flash_update(refs, g, qs, k_tile, v_tile, _group_bias(rb_ref, g, q_pos - k_pos), mask, False)

    @pl.when(n == n_blk - 1)
    def _():
        for g in range(N_KV_HEADS):
            out = acc_ref[g] / jnp.maximum(l_ref[g], 1e-30)
            for r in range(Q_PER_KV):
                h = g * Q_PER_KV + r
                o_ref[:, h * HEAD_DIM:(h + 1) * HEAD_DIM] = out[r * DEC_SEQ:(r + 1) * DEC_SEQ]


def _moba_sample_attention(proj, kmean, cache, page_table, rel_bias, row_block0, past_len):
    n_seq, n_pages = page_table.shape
    n_blk = n_pages * PAGE_SIZE // MOBA_BLOCK
    kv_cols = KV_DIM
    return pl.pallas_call(
        functools.partial(_moba_sample_kernel, n_blk=n_blk, past_len=past_len),
        grid_spec=pltpu.PrefetchScalarGridSpec(
            num_scalar_prefetch=1,
            grid=(n_seq, n_blk),
            in_specs=[
                pl.BlockSpec(memory_space=pltpu.SMEM),
                pl.BlockSpec((DEC_SEQ, Q_DIM), lambda b, n, pt: (row_block0 + b, 0)),
                pl.BlockSpec((DEC_SEQ, kv_cols), lambda b, n, pt: (row_block0 + b, Q_DIM // kv_cols)),
                pl.BlockSpec((DEC_SEQ, kv_cols), lambda b, n, pt: (row_block0 + b, Q_DIM // kv_cols + 1)),
                pl.BlockSpec((None, N_KV_HEADS, n_blk, HEAD_DIM), lambda b, n, pt: (b, 0, 0, 0)),
                _page_spec(0, 0, 2), _page_spec(0, 1, 2), _page_spec(1, 0, 2), _page_spec(1, 1, 2),
            ],
            out_specs=pl.BlockSpec((DEC_SEQ, Q_DIM), lambda b, n, pt: (b, 0)),
            scratch_shapes=[
                pltpu.VMEM((N_KV_HEADS, NEW_ROWS, 1), jnp.float32),
                pltpu.VMEM((N_KV_HEADS, NEW_ROWS, 1), jnp.float32),
                pltpu.VMEM((N_KV_HEADS, NEW_ROWS, HEAD_DIM), jnp.float32),
                pltpu.VMEM((N_KV_HEADS, NEW_ROWS, LANE), jnp.float32),
            ],
        ),
        out_shape=jax.ShapeDtypeStruct((n_seq * DEC_SEQ, Q_DIM), jnp.float32),
        compiler_params=pltpu.CompilerParams(dimension_semantics=("parallel", "arbitrary")),
    )(page_table, rel_bias, proj, proj, proj, kmean, cache, cache, cache, cache)


CMP_PAGES = 8
CHUNKS_PER_PAGE = PAGE_SIZE // CMP_STRIDE


def _nsa_sample_compress_kernel(pt_ref, *refs):
    page_refs = refs[:CMP_PAGES]
    pe_ref, w_ref, top_ref, bot_ref = refs[CMP_PAGES:]
    rows = CMP_PAGES * CHUNKS_PER_PAGE
    for c in range(2):
        top = jnp.zeros((N_KV_HEADS * rows, HEAD_DIM), jnp.float32)
        bot = jnp.zeros((N_KV_HEADS * rows, HEAD_DIM), jnp.float32)
        for r in range(CMP_STRIDE):
            x = jnp.concatenate(
                [ref[pl.ds(r, CHUNKS_PER_PAGE, stride=CMP_STRIDE), c, g, :]
                 for g in range(N_KV_HEADS) for ref in page_refs], axis=0)
            xt = (x + pe_ref[c, r:r + 1, :]).astype(jnp.bfloat16)
            xb = (x + pe_ref[c, CMP_STRIDE + r:CMP_STRIDE + r + 1, :]).astype(jnp.bfloat16)
            top = top + jnp.dot(xt, w_ref[c, r * HEAD_DIM:(r + 1) * HEAD_DIM, :], preferred_element_type=jnp.float32)
            bot = bot + jnp.dot(xb, w_ref[c, (CMP_STRIDE + r) * HEAD_DIM:(CMP_STRIDE + r + 1) * HEAD_DIM, :],
                                preferred_element_type=jnp.float32)
        for g in range(N_KV_HEADS):
            top_ref[c, g] = top[g * rows:(g + 1) * rows]
            bot_ref[c, g] = bot[g * rows:(g + 1) * rows]


def _nsa_sample_compress(cache, page_table, cmp_pe, cmp_w):
    n_seq, n_pages = page_table.shape
    n_chunks = n_pages * CHUNKS_PER_PAGE
    rows = CMP_PAGES * CHUNKS_PER_PAGE

    def page_spec(k):
        return pl.BlockSpec((None, PAGE_SIZE, 2, N_KV_HEADS, HEAD_DIM),
                            lambda b, s, pt: (pt[b, s * CMP_PAGES + k], 0, 0, 0, 0))

    out = jax.ShapeDtypeStruct((n_seq, 2, N_KV_HEADS, n_chunks, HEAD_DIM), jnp.float32)
    out_spec = pl.BlockSpec((None, 2, N_KV_HEADS, rows, HEAD_DIM), lambda b, s, pt: (b, 0, 0, s, 0))
    return pl.pallas_call(
        _nsa_sample_compress_kernel,
        grid_spec=pltpu.PrefetchScalarGridSpec(
            num_scalar_prefetch=1,
            grid=(n_seq, n_pages // CMP_PAGES),
            in_specs=[page_spec(k) for k in range(CMP_PAGES)] + [
                pl.BlockSpec((2, CMP_BLOCK, HEAD_DIM), lambda b, s, pt: (0, 0, 0)),
                pl.BlockSpec((2, CMP_BLOCK * HEAD_DIM, HEAD_DIM), lambda b, s, pt: (0, 0, 0)),
            ],
            out_specs=[out_spec, out_spec],
        ),
        out_shape=[out, out],
        compiler_params=pltpu.CompilerParams(
            dimension_semantics=("parallel", "parallel"), vmem_limit_bytes=VMEM_LIMIT),
    )(page_table, *([cache] * CMP_PAGES), cmp_pe, cmp_w)


def _nsa_sample_kernel(pt_ref, rb_ref, q_ref, gate_ref, ksn_ref, vsn_ref, kwn_ref, vwn_ref, top_ref, bot_ref,
                       ov_ref, win_ref, page_ref, o_ref,
                       m_ref, l_ref, acc_ref, sel_ref, ocmp_ref, owin_ref, *, n_pages, past_len):
    p = pl.program_id(1)
    refs = (m_ref, l_ref, acc_ref)
    n_chunks = n_pages * CHUNKS_PER_PAGE
    n_sel_pad = ov_ref.shape[1]
    n_sel = past_len // SEL_BLOCK + 1
    win_len = win_ref.shape[0]
    tok = lax.broadcasted_iota(jnp.int32, (DEC_SEQ, DEC_SEQ), 0)
    new = lax.broadcasted_iota(jnp.int32, (DEC_SEQ, DEC_SEQ), 1)

    def group_cols(ref, g):
        return ref[:, g * HEAD_DIM:(g + 1) * HEAD_DIM].astype(jnp.bfloat16)

    @pl.when(p == 0)
    def _():
        for g in range(N_KV_HEADS):
            qs = _stack_group_heads(q_ref, g)
            kc = top_ref[0, g] + pltpu.roll(bot_ref[0, g], n_chunks - 1, axis=0)
            vc = top_ref[1, g] + pltpu.roll(bot_ref[1, g], n_chunks - 1, axis=0)
            q_pos = past_len + lax.broadcasted_iota(jnp.int32, (DEC_SEQ, n_chunks), 0)
            tok_id = lax.broadcasted_iota(jnp.int32, (DEC_SEQ, n_chunks), 1)
            c_end = tok_id * CMP_STRIDE + (CMP_BLOCK - 1)
            ok_c = _tile4((tok_id < n_chunks - 1) & (c_end <= q_pos))
            lc = lax.dot_general(qs, kc.astype(jnp.bfloat16), _NT, preferred_element_type=jnp.float32) * SCALE
            lc = jnp.where(ok_c, lc + _group_bias(rb_ref, g, q_pos - c_end), NEG_INF)
            e = jnp.where(ok_c, jnp.exp(lc - jnp.max(lc, axis=-1, keepdims=True)), 0.0)
            pc = e / jnp.maximum(jnp.sum(e, axis=-1, keepdims=True), 1e-30)
            ocmp_ref[g] = jnp.dot(pc.astype(jnp.bfloat16), vc.astype(jnp.bfloat16), preferred_element_type=jnp.float32)
            p_sum = pc[0:DEC_SEQ]
            for r in range(1, Q_PER_KV):
                p_sum = p_sum + pc[r * DEC_SEQ:(r + 1) * DEC_SEQ]
            imp = jnp.dot(p_sum.astype(jnp.bfloat16), ov_ref[...], preferred_element_type=jnp.float32)
            blk = lax.broadcasted_iota(jnp.int32, (DEC_SEQ, n_sel_pad), 1)
            own = (past_len + lax.broadcasted_iota(jnp.int32, (DEC_SEQ, n_sel_pad), 0)) // SEL_BLOCK
            diff = own - blk
            valid = diff >= 0
            forced = valid & ((blk == 0) | (diff < N_LOCAL_SEL))
            score = jnp.where(forced, FORCE_SCORE, jnp.where(valid, imp, NEG_INF))
            chosen = _rank_select_rows(score, n_sel, SEL_TOPK) & valid
            sel_ref[g] = _tile4(jnp.where(chosen, 1.0, 0.0))
            causal = _tile4(new <= tok)
            near = _group_bias(rb_ref, g, tok - new)
            _flash_update(refs, g, qs, group_cols(ksn_ref, g), group_cols(vsn_ref, g), near, causal, True)
            mw, lw, aw = _flash_first(qs, group_cols(kwn_ref, g), group_cols(vwn_ref, g), near, causal)
            tok_w = lax.broadcasted_iota(jnp.int32, (DEC_SEQ, win_len), 0)
            dist_w = win_len + tok_w - lax.broadcasted_iota(jnp.int32, (DEC_SEQ, win_len), 1)
            mw, lw, aw = _flash_next(qs, win_ref[:, 0, g, :].astype(jnp.bfloat16), win_ref[:, 1, g, :].astype(jnp.bfloat16),
                                     _group_bias(rb_ref, g, dist_w), _tile4(dist_w < WINDOW), mw, lw, aw)
            owin_ref[g] = aw / jnp.maximum(lw, 1e-30)

    q_pos = past_len + lax.broadcasted_iota(jnp.int32, (DEC_SEQ, PAGE_SIZE), 0)
    k_pos = p * PAGE_SIZE + lax.broadcasted_iota(jnp.int32, (DEC_SEQ, PAGE_SIZE), 1)
    pick = jnp.where(
        lax.broadcasted_iota(jnp.int32, (n_sel_pad, PAGE_SIZE), 0)
        == p * (PAGE_SIZE // SEL_BLOCK) + lax.broadcasted_iota(jnp.int32, (n_sel_pad, PAGE_SIZE), 1) // SEL_BLOCK,
        1.0, 0.0).astype(jnp.bfloat16)
    for g in range(N_KV_HEADS):
        qs = _stack_group_heads(q_ref, g)
        mask = jnp.dot(sel_ref[g].astype(jnp.bfloat16), pick, preferred_element_type=jnp.float32) > 0.5
        _flash_update(refs, g, qs, page_ref[:, 0, g, :].astype(jnp.bfloat16), page_ref[:, 1, g, :].astype(jnp.bfloat16),
                      _group_bias(rb_ref, g, q_pos - k_pos), mask, False)

    @pl.when(p == n_pages - 1)
    def _():
        gates = jax.nn.sigmoid(gate_ref[...])
        for g in range(N_KV_HEADS):
            o_sel = acc_ref[g] / jnp.maximum(l_ref[g], 1e-30)
            for r in range(Q_PER_KV):
                rows = slice(r * DEC_SEQ, (r + 1) * DEC_SEQ)
                col = (g * Q_PER_KV + r) * 3
                out = (gates[:, col:col + 1] * ocmp_ref[g][rows] + gates[:, col + 1:col + 2] * o_sel[rows]
                       + gates[:, col + 2:col + 3] * owin_ref[g][rows])
                h = g * Q_PER_KV + r
                o_ref[:, h * HEAD_DIM:(h + 1) * HEAD_DIM] = out


def _nsa_sample_overlap(n_chunks, n_sel_pad):
    ci = np.arange(n_chunks)[:, None]
    sj = np.arange(n_sel_pad)[None, :]
    ov = (ci * CMP_STRIDE < (sj + 1) * SEL_BLOCK) & (ci * CMP_STRIDE + CMP_BLOCK > sj * SEL_BLOCK) & (ci < n_chunks - 1)
    return jnp.asarray(ov, jnp.bfloat16)


def _nsa_sample(proj, cache, win_buf, page_table, cmp_pe, cmp_w, rel_bias, row_block0, past_len):
    n_seq, n_pages = page_table.shape
    n_chunks = n_pages * CHUNKS_PER_PAGE
    n_sel_pad = -(-(past_len // SEL_BLOCK + 1) // LANE) * LANE
    top, bot = _nsa_sample_compress(cache, page_table, cmp_pe, cmp_w)
    kv0 = Q_DIM // KV_DIM
    gate_col = (Q_DIM + 6 * KV_DIM) // LANE

    def new_spec(comp):
        return pl.BlockSpec((DEC_SEQ, KV_DIM), lambda b, p, pt: (row_block0 + b, kv0 + comp))

    cmp_spec = pl.BlockSpec((None, 2, N_KV_HEADS, n_chunks, HEAD_DIM), lambda b, p, pt: (b, 0, 0, 0, 0))
    scratch = [
        pltpu.VMEM((N_KV_HEADS, NEW_ROWS, 1), jnp.float32),
        pltpu.VMEM((N_KV_HEADS, NEW_ROWS, 1), jnp.float32),
        pltpu.VMEM((N_KV_HEADS, NEW_ROWS, HEAD_DIM), jnp.float32),
        pltpu.VMEM((N_KV_HEADS, NEW_ROWS, n_sel_pad), jnp.float32),
        pltpu.VMEM((N_KV_HEADS, NEW_ROWS, HEAD_DIM), jnp.float32),
        pltpu.VMEM((N_KV_HEADS, NEW_ROWS, HEAD_DIM), jnp.float32),
    ]
    return pl.pallas_call(
        functools.partial(_nsa_sample_kernel, n_pages=n_pages, past_len=past_len),
        grid_spec=pltpu.PrefetchScalarGridSpec(
            num_scalar_prefetch=1,
            grid=(n_seq, n_pages),
            in_specs=[
                pl.BlockSpec(memory_space=pltpu.SMEM),
                pl.BlockSpec((DEC_SEQ, Q_DIM), lambda b, p, pt: (row_block0 + b, 0)),
                pl.BlockSpec((DEC_SEQ, LANE), lambda b, p, pt: (row_block0 + b, gate_col)),
                new_spec(2), new_spec(3), new_spec(4), new_spec(5),
                cmp_spec, cmp_spec,
                pl.BlockSpec((n_chunks, n_sel_pad), lambda b, p, pt: (0, 0)),
                pl.BlockSpec((None,) + win_buf.shape[1:], lambda b, p, pt: (b, 0, 0, 0, 0)),
                pl.BlockSpec((None, PAGE_SIZE, 2, N_KV_HEADS, HEAD_DIM), lambda b, p, pt: (pt[b, p], 0, 1, 0, 0)),
            ],
            out_specs=pl.BlockSpec((DEC_SEQ, Q_DIM), lambda b, p, pt: (b, 0)),
            scratch_shapes=scratch,
        ),
        out_shape=jax.ShapeDtypeStruct((n_seq * DEC_SEQ, Q_DIM), jnp.float32),
        compiler_params=pltpu.CompilerParams(
            dimension_semantics=("parallel", "arbitrary"), vmem_limit_bytes=VMEM_LIMIT),
    )(page_table, rel_bias, proj, proj, proj, proj, proj, proj, top, bot,
      _nsa_sample_overlap(n_chunks, n_sel_pad), win_buf, cache)


SSD_PAIR = 2 * SSD_HEAD_DIM
GROUP_COLS = HEADS_PER_GROUP * SSD_HEAD_DIM
CARRY_ROWS = 8


def _pad_rows(x, rows):
    if x.shape[0] == rows:
        return x
    return jnp.concatenate([x, jnp.zeros((rows - x.shape[0],) + x.shape[1:], x.dtype)], axis=0)


def _ssd_kernel(z_ref, x_ref, b_ref, c_ref, dt_ref, prev_ref, s0_ref, cw_ref, cb_ref, dtb_ref, alog_ref, dsk_ref,
                nw_ref, y_ref, conv_ref, state_ref, carry_ref, s_ref, act_ref, *, rows_in, n_chunks):
    c = pl.program_id(1)
    L = SSD_CHUNK

    @pl.when(c == 0)
    def _():
        carry_ref[...] = prev_ref[...]
        s_ref[...] = s0_ref[...]

    row8 = lax.broadcasted_iota(jnp.int32, (CARRY_ROWS, 1), 0)
    off = 0
    for ref in (x_ref, b_ref, c_ref):
        width = ref.shape[1]
        cols = slice(off, off + width)
        raw_in = ref[...]
        raw = _pad_rows(raw_in, L)
        prev = carry_ref[:, cols]
        acc = cb_ref[:, cols] + cw_ref[CONV_W - 1:CONV_W, cols] * raw
        for k in range(1, CONV_W):
            rolled = pltpu.roll(raw, k, axis=0)
            head = jnp.where(row8 < k, pltpu.roll(prev, k, axis=0), rolled[:CARRY_ROWS])
            shifted = jnp.concatenate([head, rolled[CARRY_ROWS:]], axis=0)
            acc = acc + cw_ref[CONV_W - 1 - k:CONV_W - k, cols] * shifted
        act_ref[:, cols] = acc * jax.nn.sigmoid(acc)
        carry_ref[:, cols] = raw_in[rows_in - CARRY_ROWS:rows_in]

        @pl.when(c == n_chunks - 1)
        def _():
            conv_ref[:, cols] = raw_in[rows_in - (CONV_W - 1):rows_in]

        off += width

    row = lax.broadcasted_iota(jnp.int32, (L, LANE), 0)
    pre = _pad_rows(dt_ref[...], L) + dtb_ref[...]
    dt = jnp.maximum(pre, 0.0) + jnp.log1p(jnp.exp(-jnp.abs(pre)))
    dt = jnp.where(row < rows_in, dt, 0.0)
    acum = dt * (-jnp.exp(alog_ref[...]))
    shift = 1
    while shift < L:
        acum = acum + jnp.where(row >= shift, pltpu.roll(acum, shift, axis=0), 0.0)
        shift *= 2
    acum_t = acum.T
    exp_a = jnp.exp(acum)
    to_end = jnp.exp(acum[L - 1:L, :] - acum)
    li = lax.broadcasted_iota(jnp.int32, (L, L), 0)
    si = lax.broadcasted_iota(jnp.int32, (L, L), 1)
    causal = li >= si
    first_half = lax.broadcasted_iota(jnp.int32, (L, SSD_PAIR), 1) < SSD_HEAD_DIM

    def pair_cols(v, h):
        return jnp.where(first_half, v[:, h:h + 1], v[:, h + 1:h + 2])

    for g in range(SSD_GROUPS):
        bg = act_ref[:, D_INNER + g * D_STATE:D_INNER + (g + 1) * D_STATE].astype(jnp.bfloat16)
        cg = act_ref[:, D_INNER + GN + g * D_STATE:D_INNER + GN + (g + 1) * D_STATE].astype(jnp.bfloat16)
        cb = lax.dot_general(cg, bg, _NT, preferred_element_type=jnp.float32)
        s_prev = s_ref[g * GROUP_COLS:(g + 1) * GROUP_COLS, :]
        y_off = lax.dot_general(cg, s_prev.astype(jnp.bfloat16), _NT, preferred_element_type=jnp.float32)
        xw_parts = []
        for pr in range(HEADS_PER_GROUP // 2):
            h = g * HEADS_PER_GROUP + 2 * pr
            cols = slice(h * SSD_HEAD_DIM, (h + 2) * SSD_HEAD_DIM)
            xa = act_ref[:, cols]
            xdt = xa * pair_cols(dt, h)
            y = y_off[:, 2 * pr * SSD_HEAD_DIM:(2 * pr + 2) * SSD_HEAD_DIM] * pair_cols(exp_a, h)
            for half in range(2):
                hh = h + half
                decay = jnp.where(causal, jnp.exp(acum[:, hh:hh + 1] - acum_t[hh:hh + 1, :]), 0.0)
                mine = first_half if half == 0 else jnp.logical_not(first_half)
                y = y + jnp.dot((cb * decay).astype(jnp.bfloat16), jnp.where(mine, xdt, 0.0).astype(jnp.bfloat16),
                                preferred_element_type=jnp.float32)
            zz = z_ref[:, cols]
            y = y[:rows_in] + xa[:rows_in] * dsk_ref[:, cols]
            y_ref[:, cols] = y * (zz * jax.nn.sigmoid(zz))
            xw_parts.append(xdt * pair_cols(to_end, h))
        xw = jnp.concatenate(xw_parts, axis=1)
        chunk_state = jnp.dot(xw.T.astype(jnp.bfloat16), bg, preferred_element_type=jnp.float32)
        for r in range(HEADS_PER_GROUP):
            h = g * HEADS_PER_GROUP + r
            rows = slice(h * SSD_HEAD_DIM, (h + 1) * SSD_HEAD_DIM)
            s_ref[rows, :] = (s_ref[rows, :] * jnp.exp(acum_t[h:h + 1, L - 1:L])
                              + chunk_state[r * SSD_HEAD_DIM:(r + 1) * SSD_HEAD_DIM])
        gcols = slice(g * GROUP_COLS, (g + 1) * GROUP_COLS)
        yg = y_ref[:, gcols]
        y_ref[:, gcols] = yg * lax.rsqrt(jnp.mean(yg * yg, axis=-1, keepdims=True) + RMS_EPS) * nw_ref[:, gcols]

    @pl.when(c == n_chunks - 1)
    def _():
        state_ref[...] = s_ref[...]


def _ssd_mix(proj, conv_prev, ssm_prev, conv_w, conv_b, dt_bias, a_log, d_skip, norm_w, row_block0, rows_in, n_chunks):
    n_seq = conv_prev.shape[0]

    def rows_spec(width, col_block):
        return pl.BlockSpec((rows_in, width), lambda b, c: (row_block0 + b * n_chunks + c, col_block))

    def const_spec(shape):
        return pl.BlockSpec(shape, lambda b, c: (0,) * len(shape))

    pad_lanes = lambda v: jnp.pad(v, (0, LANE - v.shape[0])).reshape(1, LANE)
    return pl.pallas_call(
        functools.partial(_ssd_kernel, rows_in=rows_in, n_chunks=n_chunks),
        grid=(n_seq, n_chunks),
        in_specs=[
            rows_spec(D_INNER, 0),
            rows_spec(D_INNER, 1),
            rows_spec(GN, 2 * D_INNER // GN),
            rows_spec(GN, 2 * D_INNER // GN + 1),
            rows_spec(LANE, (D_INNER + CONV_DIM) // LANE),
            pl.BlockSpec((None, CARRY_ROWS, CONV_DIM), lambda b, c: (b, 0, 0)),
            pl.BlockSpec((None, D_INNER, D_STATE), lambda b, c: (b, 0, 0)),
            const_spec((CONV_W, CONV_DIM)),
            const_spec((1, CONV_DIM)),
            const_spec((1, LANE)),
            const_spec((1, LANE)),
            const_spec((1, D_INNER)),
            const_spec((1, D_INNER)),
        ],
        out_specs=[
            pl.BlockSpec((rows_in, D_INNER), lambda b, c: (b * n_chunks + c, 0)),
            pl.BlockSpec((None, CONV_W - 1, CONV_DIM), lambda b, c: (b, 0, 0)),
            pl.BlockSpec((None, D_INNER, D_STATE), lambda b, c: (b, 0, 0)),
        ],
        out_shape=[
            jax.ShapeDtypeStruct((n_seq * n_chunks * rows_in, D_INNER), jnp.float32),
            jax.ShapeDtypeStruct((n_seq, CONV_W - 1, CONV_DIM), jnp.float32),
            jax.ShapeDtypeStruct((n_seq, D_INNER, D_STATE), jnp.float32),
        ],
        scratch_shapes=[
            pltpu.VMEM((CARRY_ROWS, CONV_DIM), jnp.float32),
            pltpu.VMEM((D_INNER, D_STATE), jnp.float32),
            pltpu.VMEM((SSD_CHUNK, CONV_DIM), jnp.float32),
        ],
        compiler_params=pltpu.CompilerParams(
            dimension_semantics=("parallel", "arbitrary"), vmem_limit_bytes=VMEM_LIMIT),
    )(proj, proj, proj, proj, proj, conv_prev, ssm_prev, conv_w, conv_b.reshape(1, CONV_DIM),
      pad_lanes(dt_bias), pad_lanes(a_log), jnp.repeat(d_skip, SSD_HEAD_DIM).reshape(1, D_INNER),
      norm_w.reshape(1, D_INNER))


def _masked_softmax(logits, mask):
    logits = jnp.where(mask, logits.astype(jnp.float32), NEG_INF)
    m = jnp.max(logits, axis=-1, keepdims=True)
    e = jnp.where(mask, jnp.exp(logits - m), 0.0)
    return e / jnp.maximum(jnp.sum(e, axis=-1, keepdims=True), 1e-30)


def _rel_bucket(dist):
    n = jnp.maximum(dist, 0)
    log_ratio = jnp.log(jnp.maximum(n, 1).astype(jnp.float32) / REL_MAX_EXACT) / math.log(REL_MAX_DIST / REL_MAX_EXACT)
    large = jnp.minimum(REL_MAX_EXACT + (log_ratio * (REL_BUCKETS - REL_MAX_EXACT)).astype(jnp.int32), REL_BUCKETS - 1)
    return jnp.where(n < REL_MAX_EXACT, n, large)


def _shared_bias(rel_bias, q_pos, k_pos):
    bias = rel_bias[_rel_bucket(q_pos[:, None] - k_pos[None, :])]
    nq, nk = bias.shape[:2]
    return bias.reshape(nq, nk, N_KV_HEADS, Q_PER_KV).transpose(0, 2, 3, 1).astype(jnp.float32)


def _gathered_bias(rel_bias, dist):
    table = rel_bias.T.reshape(N_KV_HEADS, Q_PER_KV, REL_BUCKETS)
    g_idx = jnp.arange(N_KV_HEADS).reshape(1, 1, -1, 1, 1, 1)
    r_idx = jnp.arange(Q_PER_KV).reshape(1, 1, 1, -1, 1, 1)
    return table[g_idx, r_idx, _rel_bucket(dist)].astype(jnp.float32)


def _gather_pages(pool, page_table, comp):
    rows = pool[page_table, :, comp]
    return rows.reshape(page_table.shape[0], page_table.shape[1] * pool.shape[1], N_KV_HEADS, HEAD_DIM)


def _ssd_scan(x, dt, a, bmat, cmat, init_state):
    b, t, nh, p = x.shape
    l = min(SSD_CHUNK, t)
    nc = -(-t // l)
    pad = nc * l - t

    def padt(arr):
        return jnp.pad(arr.astype(jnp.float32), [(0, 0), (0, pad)] + [(0, 0)] * (arr.ndim - 2))

    xr = padt(x).reshape(b, nc, l, SSD_GROUPS, HEADS_PER_GROUP, p)
    dtr = padt(dt).reshape(b, nc, l, SSD_GROUPS, HEADS_PER_GROUP)
    br = padt(bmat).reshape(b, nc, l, SSD_GROUPS, D_STATE)
    cr = padt(cmat).reshape(b, nc, l, SSD_GROUPS, D_STATE)
    acum = jnp.cumsum(dtr * a.reshape(SSD_GROUPS, HEADS_PER_GROUP), axis=2)
    xdt = xr * dtr[..., None]
    causal = jnp.tril(jnp.ones((l, l), dtype=bool))[None, None, :, :, None, None]
    decay = jnp.exp(jnp.where(causal, acum[:, :, :, None] - acum[:, :, None, :], -jnp.inf))
    cb = jnp.einsum('bclgn,bcsgn->bclsg', cr, br)
    y_diag = jnp.einsum('bclsg,bclsgr,bcsgrp->bclgrp', cb, decay, xdt)
    chunk_states = jnp.einsum('bclgn,bclgr,bclgrp->bcgrpn', br, jnp.exp(acum[:, :, -1:] - acum), xdt)
    chunk_decay = jnp.exp(acum[:, :, -1])

    def step(s, inp):
        st, dc = inp
        return s * dc[..., None, None] + st, s

    s0 = init_state.astype(jnp.float32).reshape(b, SSD_GROUPS, HEADS_PER_GROUP, p, D_STATE)
    final, prev = lax.scan(step, s0, (jnp.moveaxis(chunk_states, 1, 0), jnp.moveaxis(chunk_decay, 1, 0)))
    prev = jnp.moveaxis(prev, 0, 1)
    y_off = jnp.einsum('bclgn,bcgrpn,bclgr->bclgrp', cr, prev, jnp.exp(acum))
    y = (y_diag + y_off).reshape(b, nc * l, nh, p)[:, :t]
    return y.astype(x.dtype), final.reshape(b, nh, p, D_STATE).astype(init_state.dtype)


def _ssd_core(proj, conv_state, ssm_state, conv_w, conv_b, dt_bias, a_log, d_skip, norm_w):
    b, t, _ = proj.shape
    z = proj[..., :D_INNER]
    xbc = proj[..., D_INNER:D_INNER + CONV_DIM]
    dt_raw = proj[..., D_INNER + CONV_DIM:D_INNER + CONV_DIM + SSD_HEADS]
    xbc_ext = jnp.concatenate([conv_state.astype(xbc.dtype), xbc], axis=1)
    conv = conv_b + sum(xbc_ext[:, k:k + t] * conv_w[k] for k in range(CONV_W))
    xbc = jax.nn.silu(conv)
    x = xbc[..., :D_INNER].reshape(b, t, SSD_HEADS, SSD_HEAD_DIM)
    bmat = xbc[..., D_INNER:D_INNER + GN].reshape(b, t, SSD_GROUPS, D_STATE)
    cmat = xbc[..., D_INNER + GN:].reshape(b, t, SSD_GROUPS, D_STATE)
    dt = jax.nn.softplus((dt_raw + dt_bias).astype(jnp.float32))
    a = -jnp.exp(a_log.astype(jnp.float32))
    y, new_ssm = _ssd_scan(x, dt, a, bmat, cmat, ssm_state)
    y = (y + x * d_skip[:, None]).reshape(b, t, D_INNER) * jax.nn.silu(z)
    yg = y.reshape(b, t, SSD_GROUPS, D_INNER // SSD_GROUPS).astype(jnp.float32)
    yg = yg * lax.rsqrt(jnp.mean(yg * yg, axis=-1, keepdims=True) + RMS_EPS)
    y = yg.reshape(b, t, D_INNER) * norm_w
    return y, xbc_ext[:, xbc_ext.shape[1] - (CONV_W - 1):], new_ssm


def _nsa_split(proj):
    b, t, _ = proj.shape
    q = proj[..., :Q_DIM].reshape(b, t, N_KV_HEADS, Q_PER_KV, HEAD_DIM)
    kv = proj[..., Q_DIM:Q_DIM + 6 * KV_DIM].reshape(b, t, 6, N_KV_HEADS, HEAD_DIM)
    gates = jax.nn.sigmoid(proj[..., Q_DIM + 6 * KV_DIM:NSA_IN_DIM]).reshape(b, t, N_KV_HEADS, Q_PER_KV, 3)
    return q, kv, gates


def _nsa_compress(k, pe, w):
    b, t = k.shape[:2]
    n_str = t // CMP_STRIDE
    ch = k[:, :n_str * CMP_STRIDE].reshape(b, n_str, CMP_STRIDE, N_KV_HEADS, HEAD_DIM)
    blk = jnp.concatenate([ch[:, :-1], ch[:, 1:]], axis=2) + pe[:, None, :]
    flat = blk.transpose(0, 1, 3, 2, 4).reshape(b, n_str - 1, N_KV_HEADS, CMP_BLOCK * HEAD_DIM)
    return flat @ w


def _nsa_branch_keys(k_cmp, v_cmp, k_sel, v_sel, cmp_pe, cmp_w):
    b, t = k_cmp.shape[:2]
    kc = _nsa_compress(k_cmp, cmp_pe[0], cmp_w[0])
    vc = _nsa_compress(v_cmp, cmp_pe[1], cmp_w[1])
    c_end = jnp.arange(kc.shape[1], dtype=jnp.int32) * CMP_STRIDE + (CMP_BLOCK - 1)
    n_sel = -(-t // SEL_BLOCK)
    pad = ((0, 0), (0, n_sel * SEL_BLOCK - t), (0, 0), (0, 0))
    ks = jnp.pad(k_sel, pad).reshape(b, n_sel, SEL_BLOCK, N_KV_HEADS, HEAD_DIM)
    vs = jnp.pad(v_sel, pad).reshape(b, n_sel, SEL_BLOCK, N_KV_HEADS, HEAD_DIM)
    return kc, vc, c_end, ks, vs


def _nsa_core(q, q_pos, kc, vc, c_end, ks_blk, vs_blk, kw, vw, w_pos, gates, rel_bias):
    bsz, nq = q.shape[:2]
    scale = HEAD_DIM ** -0.5
    tq = q_pos[:, None]
    lc = jnp.einsum('bqgrd,bkgd->bqgrk', q, kc) * scale + _shared_bias(rel_bias, q_pos, c_end)
    pc = _masked_softmax(lc, (c_end[None, :] <= tq)[None, :, None, None, :])
    o_cmp = jnp.einsum('bqgrk,bkgd->bqgrd', pc.astype(vc.dtype), vc)
    nc, ns = kc.shape[1], ks_blk.shape[1]
    ci = jnp.arange(nc)[:, None]
    sj = jnp.arange(ns)[None, :]
    overlap = ((ci * CMP_STRIDE < (sj + 1) * SEL_BLOCK) & (ci * CMP_STRIDE + CMP_BLOCK > sj * SEL_BLOCK)).astype(jnp.float32)
    imp = jnp.einsum('bqgk,kj->bqgj', jnp.sum(pc, axis=3), overlap)
    own = q_pos // SEL_BLOCK
    j = jnp.arange(ns)
    diff = own[:, None] - j[None, :]
    valid = diff >= 0
    forced = (j[None, :] == 0) | (valid & (diff < N_LOCAL_SEL))
    score = jnp.where((valid & forced)[None, :, None, :], FORCE_SCORE, jnp.where(valid[None, :, None, :], imp, NEG_INF))
    k_top = min(SEL_TOPK, ns)
    _, idx = lax.top_k(score, k_top)
    sel_ok = idx <= own[None, :, None, None]
    bi = jnp.arange(bsz).reshape(-1, 1, 1, 1)
    gi = jnp.arange(N_KV_HEADS).reshape(1, 1, -1, 1)
    kg = ks_blk[bi, idx, :, gi]
    vg = vs_blk[bi, idx, :, gi]
    pos = idx[..., None] * SEL_BLOCK + jnp.arange(SEL_BLOCK)
    tq5 = q_pos[None, :, None, None, None]
    ms = sel_ok[..., None] & (pos <= tq5)
    ls = jnp.einsum('bqgrd,bqgkcd->bqgrkc', q, kg) * scale + _gathered_bias(rel_bias, (tq5 - pos)[:, :, :, None])
    ps = _masked_softmax(ls.reshape(bsz, nq, N_KV_HEADS, Q_PER_KV, k_top * SEL_BLOCK),
                         ms[:, :, :, None].reshape(bsz, nq, N_KV_HEADS, 1, k_top * SEL_BLOCK))
    ps = ps.reshape(bsz, nq, N_KV_HEADS, Q_PER_KV, k_top, SEL_BLOCK).astype(vg.dtype)
    o_sel = jnp.einsum('bqgrkc,bqgkcd->bqgrd', ps, vg)
    lw = jnp.einsum('bqgrd,bkgd->bqgrk', q, kw) * scale + _shared_bias(rel_bias, q_pos, w_pos)
    dw = tq - w_pos[None, :]
    mw = ((dw >= 0) & (dw < WINDOW) & (w_pos[None, :] >= 0))[None, :, None, None, :]
    o_win = jnp.einsum('bqgrk,bkgd->bqgrd', _masked_softmax(lw, mw).astype(vw.dtype), vw)
    return gates[..., 0:1] * o_cmp + gates[..., 1:2] * o_sel + gates[..., 2:3] * o_win


def _nsa_prompt_core(proj, cmp_pe, cmp_w, rel_bias):
    b, t, _ = proj.shape
    q, kv, gates = _nsa_split(proj)
    kc, vc, c_end, ks, vs = _nsa_branch_keys(kv[:, :, 0], kv[:, :, 1], kv[:, :, 2], kv[:, :, 3], cmp_pe, cmp_w)
    wpad = ((0, 0), (WINDOW, 0), (0, 0), (0, 0))
    kw_pad, vw_pad = jnp.pad(kv[:, :, 4], wpad), jnp.pad(kv[:, :, 5], wpad)

    def block(i):
        s0 = i * Q_BLOCK
        q_pos = s0 + jnp.arange(Q_BLOCK, dtype=jnp.int32)
        w_pos = s0 - WINDOW + jnp.arange(WINDOW + Q_BLOCK, dtype=jnp.int32)
        qb = lax.dynamic_slice_in_dim(q, s0, Q_BLOCK, axis=1)
        gb = lax.dynamic_slice_in_dim(gates, s0, Q_BLOCK, axis=1)
        kw = lax.dynamic_slice_in_dim(kw_pad, s0, WINDOW + Q_BLOCK, axis=1)
        vw = lax.dynamic_slice_in_dim(vw_pad, s0, WINDOW + Q_BLOCK, axis=1)
        return _nsa_core(qb, q_pos, kc, vc, c_end, ks, vs, kw, vw, w_pos, gb, rel_bias)

    o = lax.map(block, jnp.arange(t // Q_BLOCK, dtype=jnp.int32))
    o = jnp.moveaxis(o, 0, 1).reshape(b, t, Q_DIM)
    wb = min(WINDOW, t)
    return o, kv[:, :, :4], kv[:, t - wb:, 4:]


def _nsa_sample_core(proj, cache_kv, win_buf, page_table, cmp_pe, cmp_w, rel_bias):
    b, t, _ = proj.shape
    q, kv, gates = _nsa_split(proj)
    full = [jnp.concatenate([_gather_pages(cache_kv, page_table, c).astype(kv.dtype), kv[:, :, c]], axis=1) for c in range(4)]
    kc, vc, c_end, ks, vs = _nsa_branch_keys(full[0], full[1], full[2], full[3], cmp_pe, cmp_w)
    wb = win_buf.shape[1]
    win = jnp.concatenate([win_buf.astype(kv.dtype), kv[:, :, 4:]], axis=1)
    w_pos = PAST_LEN - wb + jnp.arange(wb + t, dtype=jnp.int32)
    q_pos = PAST_LEN + jnp.arange(t, dtype=jnp.int32)
    o = _nsa_core(q, q_pos, kc, vc, c_end, ks, vs, win[:, :, 0], win[:, :, 1], w_pos, gates, rel_bias)
    return o.reshape(b, t, Q_DIM), kv[:, :, :4], win[:, t:]


def _moba_split(proj):
    b, t, _ = proj.shape
    q = proj[..., :Q_DIM].reshape(b, t, N_KV_HEADS, Q_PER_KV, HEAD_DIM)
    kv = proj[..., Q_DIM:].reshape(b, t, 2, N_KV_HEADS, HEAD_DIM)
    return q, kv


def _moba_blocks(k, v):
    b, t = k.shape[:2]
    nf = t // MOBA_BLOCK
    kb = k[:, :nf * MOBA_BLOCK].reshape(b, nf, MOBA_BLOCK, N_KV_HEADS, HEAD_DIM)
    vb = v[:, :nf * MOBA_BLOCK].reshape(b, nf, MOBA_BLOCK, N_KV_HEADS, HEAD_DIM)
    return kb, vb, jnp.mean(kb.astype(jnp.float32), axis=2)


def _moba_core(q, q_pos, kb, vb, kmean, own_k, own_v, own_pos, rel_bias):
    bsz, nq = q.shape[:2]
    scale = HEAD_DIM ** -0.5
    q_blk = q_pos // MOBA_BLOCK
    lo = jnp.einsum('bqgrd,bkgd->bqgrk', q, own_k) * scale + _shared_bias(rel_bias, q_pos, own_pos)
    mo = (own_pos[None, :] <= q_pos[:, None]) & ((own_pos // MOBA_BLOCK)[None, :] == q_blk[:, None])
    mo = jnp.broadcast_to(mo[None, :, None, None, :], lo.shape)
    nf = kmean.shape[1]
    k_top = min(MOBA_TOPK, nf)
    gate = jnp.einsum('bqgrd,bngd->bqgrn', q.astype(jnp.float32), kmean)
    past_ok = (jnp.arange(nf)[None, :] < q_blk[:, None])[None, :, None, None, :]
    _, idx = lax.top_k(jnp.where(past_ok, gate, NEG_INF), k_top)
    sel_ok = idx < q_blk[None, :, None, None, None]
    bi = jnp.arange(bsz).reshape(-1, 1, 1, 1, 1)
    gi = jnp.arange(N_KV_HEADS).reshape(1, 1, -1, 1, 1)
    kg = kb[bi, idx, :, gi]
    vg = vb[bi, idx, :, gi]
    pos = idx[..., None] * MOBA_BLOCK + jnp.arange(MOBA_BLOCK)
    lp = jnp.einsum('bqgrd,bqgrkcd->bqgrkc', q, kg) * scale + _gathered_bias(rel_bias, q_pos[None, :, None, None, None, None] - pos)
    mp = jnp.broadcast_to(sel_ok[..., None], lp.shape)
    n_past = k_top * MOBA_BLOCK
    logits = jnp.concatenate([lp.reshape(bsz, nq, N_KV_HEADS, Q_PER_KV, n_past), lo], axis=-1)
    mask = jnp.concatenate([mp.reshape(bsz, nq, N_KV_HEADS, Q_PER_KV, n_past), mo], axis=-1)
    p = _masked_softmax(logits, mask).astype(vg.dtype)
    pp = p[..., :n_past].reshape(bsz, nq, N_KV_HEADS, Q_PER_KV, k_top, MOBA_BLOCK)
    return jnp.einsum('bqgrkc,bqgrkcd->bqgrd', pp, vg) + jnp.einsum('bqgrk,bkgd->bqgrd', p[..., n_past:], own_v)


def _moba_prompt_core(proj, rel_bias):
    b, t, _ = proj.shape
    q, kv = _moba_split(proj)
    k, v = kv[:, :, 0], kv[:, :, 1]
    kb, vb, kmean = _moba_blocks(k, v)

    def block(i):
        s0 = i * Q_BLOCK
        q_pos = s0 + jnp.arange(Q_BLOCK, dtype=jnp.int32)
        own0 = (s0 // MOBA_BLOCK) * MOBA_BLOCK
        own_pos = own0 + jnp.arange(MOBA_BLOCK, dtype=jnp.int32)
        qb = lax.dynamic_slice_in_dim(q, s0, Q_BLOCK, axis=1)
        ok = lax.dynamic_slice_in_dim(k, own0, MOBA_BLOCK, axis=1)
        ov = lax.dynamic_slice_in_dim(v, own0, MOBA_BLOCK, axis=1)
        return _moba_core(qb, q_pos, kb, vb, kmean, ok, ov, own_pos, rel_bias)

    o = lax.map(block, jnp.arange(t // Q_BLOCK, dtype=jnp.int32))
    o = jnp.moveaxis(o, 0, 1).reshape(b, t, Q_DIM)
    return o, kv


def _moba_sample_core(proj, cache_kv, page_table, rel_bias):
    b, t, _ = proj.shape
    q, kv = _moba_split(proj)
    k = jnp.concatenate([_gather_pages(cache_kv, page_table, 0).astype(kv.dtype), kv[:, :, 0]], axis=1)
    v = jnp.concatenate([_gather_pages(cache_kv, page_table, 1).astype(kv.dtype), kv[:, :, 1]], axis=1)
    kb, vb, kmean = _moba_blocks(k, v)
    own0 = (PAST_LEN // MOBA_BLOCK) * MOBA_BLOCK
    own_pos = own0 + jnp.arange(k.shape[1] - own0, dtype=jnp.int32)
    q_pos = PAST_LEN + jnp.arange(t, dtype=jnp.int32)
    o = _moba_core(q, q_pos, kb, vb, kmean, k[:, own0:], v[:, own0:], own_pos, rel_bias)
    return o.reshape(b, t, Q_DIM), kv


def _pad_cols(w, n):
    return jnp.pad(w, ((0, 0), (0, n - w.shape[1])))


def _split_rows(a):
    return (a[:M_PROMPT].reshape(BATCH, SEQ, a.shape[-1]),
            a[M_PROMPT:].reshape(DEC_BATCH, DEC_SEQ, a.shape[-1]))


def _join_rows(p, s):
    return jnp.concatenate([p.reshape(M_PROMPT, p.shape[-1]), s.reshape(M_SAMPLE, s.shape[-1])], axis=0)


def kernel(x_prompt, x_sample, state_ssm, state_conv, cache_nsa_kv, cache_nsa_win, cache_moba_kv, page_table, rel_bias, norm_w, final_norm_w, ffn_w_gate, ffn_w_up, ffn_w_down, ssd_w_in, ssd_conv_w, ssd_conv_b, ssd_dt_bias, ssd_a_log, ssd_d, ssd_norm_w, ssd_w_out, nsa_w_in, nsa_cmp_pe, nsa_cmp_w, nsa_w_out, moba_w_in, moba_w_out):
    bf = jnp.bfloat16
    h = _join_rows(x_prompt, x_sample)
    ssm_p, ssm_s, conv_p, conv_s = [], [], [], []
    nsa_kv_p, nsa_kv_s, nsa_win_p, nsa_win_s = [], [], [], []
    moba_kv_p, moba_kv_s = [], []
    for i in range(DEPTH):
        kind, j = i % N_MIXERS, i // N_MIXERS
        h = _ffn_half(h, norm_w[i, 0], ffn_w_gate[i, 0].astype(bf), ffn_w_up[i, 0].astype(bf), ffn_w_down[i, 0].astype(bf))
        if kind == 0:
            n_pad = -(-SSD_IN_DIM // 512) * 512
            proj = _norm_proj(h, norm_w[i, 1], _pad_cols(ssd_w_in[j], n_pad).astype(bf))
            w = (ssd_conv_w[j], ssd_conv_b[j], ssd_dt_bias[j], ssd_a_log[j], ssd_d[j], ssd_norm_w[j])
            conv0 = jnp.zeros((BATCH, CARRY_ROWS, CONV_DIM), jnp.float32)
            ssm0 = jnp.zeros((BATCH, D_INNER, D_STATE), jnp.float32)
            y_p, c_p, s_p = _ssd_mix(proj, conv0, ssm0, *w, 0, SSD_CHUNK, SEQ // SSD_CHUNK)
            conv1 = jnp.pad(state_conv[j], ((0, 0), (CARRY_ROWS - (CONV_W - 1), 0), (0, 0)))
            ssm1 = state_ssm[j].reshape(DEC_BATCH, D_INNER, D_STATE)
            y_s, c_s, s_s = _ssd_mix(proj, conv1, ssm1, *w, M_PROMPT // DEC_SEQ, DEC_SEQ, 1)
            conv_p.append(c_p)
            conv_s.append(c_s)
            ssm_p.append(s_p.reshape(BATCH, SSD_HEADS, SSD_HEAD_DIM, D_STATE))
            ssm_s.append(s_s.reshape(DEC_BATCH, SSD_HEADS, SSD_HEAD_DIM, D_STATE))
            w_out = ssd_w_out[j]
        elif kind == 1:
            n_pad = -(-NSA_IN_DIM // 512) * 512
            proj = _norm_proj(h, norm_w[i, 1], _pad_cols(nsa_w_in[j], n_pad).astype(bf))
            proj_p, proj_s = _split_rows(proj)
            kc, vc = _nsa_compress_prompt(proj, nsa_cmp_pe[j], nsa_cmp_w[j].astype(bf), BATCH, SEQ)
            y_p = _nsa_attention(proj, kc, vc, rel_bias, BATCH, SEQ)
            kv_all = proj_p[..., Q_DIM:Q_DIM + 6 * KV_DIM].reshape(BATCH, SEQ, 6, N_KV_HEADS, HEAD_DIM)
            kv_p, win_p = kv_all[:, :, :4], kv_all[:, SEQ - WINDOW:, 4:]
            cmp_w_bf = nsa_cmp_w[j].astype(bf)
            y_s = _nsa_sample(proj, cache_nsa_kv[j], cache_nsa_win[j], page_table, nsa_cmp_pe[j], cmp_w_bf,
                              rel_bias, M_PROMPT // DEC_SEQ, PAST_LEN)
            kv_new = proj_s[..., Q_DIM:Q_DIM + 6 * KV_DIM].reshape(DEC_BATCH, DEC_SEQ, 6, N_KV_HEADS, HEAD_DIM)
            kv_s = kv_new[:, :, :4]
            win_s = jnp.concatenate([cache_nsa_win[j][:, DEC_SEQ:], kv_new[:, :, 4:]], axis=1)
            nsa_kv_p.append(kv_p)
            nsa_kv_s.append(kv_s)
            nsa_win_p.append(win_p)
            nsa_win_s.append(win_s)
            w_out = nsa_w_out[j]
        else:
            proj = _norm_proj(h, norm_w[i, 1], moba_w_in[j].astype(bf))
            proj_p, proj_s = _split_rows(proj)
            y_p = _moba_attention(proj, rel_bias, BATCH, SEQ)
            kv_p = proj_p[..., Q_DIM:].reshape(BATCH, SEQ, 2, N_KV_HEADS, HEAD_DIM)
            kmean = _moba_sample_kmean(cache_moba_kv[j], page_table)
            y_s = _moba_sample_attention(proj, jnp.swapaxes(kmean, 1, 2), cache_moba_kv[j], page_table, rel_bias,
                                         M_PROMPT // DEC_SEQ, PAST_LEN)
            kv_s = proj_s[..., Q_DIM:].reshape(DEC_BATCH, DEC_SEQ, 2, N_KV_HEADS, HEAD_DIM)
            moba_kv_p.append(kv_p)
            moba_kv_s.append(kv_s)
            w_out = moba_w_out[j]
        h = _proj_residual(_join_rows(y_p, y_s), w_out.astype(bf), h)
        h = _ffn_half(h, norm_w[i, 2], ffn_w_gate[i, 1].astype(bf), ffn_w_up[i, 1].astype(bf), ffn_w_down[i, 1].astype(bf))
    y = _final_norm(h, final_norm_w)
    y_prompt, y_sample = _split_rows(y)
    return (y_prompt, y_sample,
            jnp.stack(ssm_p), jnp.stack(ssm_s), jnp.stack(conv_p), jnp.stack(conv_s),
            jnp.stack(nsa_kv_p), jnp.stack(nsa_kv_s), jnp.stack(nsa_win_p), jnp.stack(nsa_win_s),
            jnp.stack(moba_kv_p), jnp.stack(moba_kv_s))
```

```python
import functools
import math

import jax
import jax.numpy as jnp
import numpy as np
from jax import lax
from jax.experimental import pallas as pl
from jax.experimental.pallas import tpu as pltpu

D_MODEL = 2048
BATCH = 2
SEQ = 4096
DEPTH = 4
DEC_BATCH = 8
DEC_SEQ = 8
PAST_LEN = 16384
PAGE_SIZE = 128
N_MIXERS = 3
D_FF = 5632
RMS_EPS = 1e-6
D_INNER = 2 * D_MODEL
SSD_HEAD_DIM = 64
SSD_HEADS = D_INNER // SSD_HEAD_DIM
D_STATE = 128
SSD_GROUPS = 8
HEADS_PER_GROUP = SSD_HEADS // SSD_GROUPS
GN = SSD_GROUPS * D_STATE
CONV_W = 4
CONV_DIM = D_INNER + 2 * GN
SSD_IN_DIM = D_INNER + CONV_DIM + SSD_HEADS
SSD_CHUNK = 128
HEAD_DIM = 128
N_HEADS = D_MODEL // HEAD_DIM
N_KV_HEADS = 4
Q_PER_KV = N_HEADS // N_KV_HEADS
Q_DIM = N_HEADS * HEAD_DIM
KV_DIM = N_KV_HEADS * HEAD_DIM
Q_BLOCK = 64
CMP_BLOCK = 32
CMP_STRIDE = 16
SEL_BLOCK = 64
SEL_TOPK = 16
N_LOCAL_SEL = 2
WINDOW = 512
NSA_IN_DIM = Q_DIM + 6 * KV_DIM + 3 * N_HEADS
MOBA_BLOCK = 256
MOBA_TOPK = 3
MOBA_IN_DIM = Q_DIM + 2 * KV_DIM
REL_BUCKETS = 32
REL_MAX_EXACT = REL_BUCKETS // 2
REL_MAX_DIST = 128
NEG_INF = -1e30
FORCE_SCORE = 1e9

M_PROMPT = BATCH * SEQ
M_SAMPLE = DEC_BATCH * DEC_SEQ
M_TOTAL = M_PROMPT + M_SAMPLE
ROW_TILE = 688
LANE = 128
VMEM_LIMIT = 56 * 1024 * 1024


def _rms_rows(x, g):
    ms = jnp.mean(x * x, axis=-1, keepdims=True)
    return x * lax.rsqrt(ms + RMS_EPS) * g


def _ffn_kernel(h_ref, g_ref, wg_ref, wu_ref, wd_ref, o_ref, xn_ref, acc_ref):
    f = pl.program_id(1)

    @pl.when(f == 0)
    def _():
        xn_ref[...] = _rms_rows(h_ref[...], g_ref[...]).astype(jnp.bfloat16)
        acc_ref[...] = jnp.zeros_like(acc_ref)

    xn = xn_ref[...]
    gate = jnp.dot(xn, wg_ref[...], preferred_element_type=jnp.float32)
    up = jnp.dot(xn, wu_ref[...], preferred_element_type=jnp.float32)
    act = (gate * jax.nn.sigmoid(gate) * up).astype(jnp.bfloat16)
    acc_ref[...] += jnp.dot(act, wd_ref[...], preferred_element_type=jnp.float32)

    @pl.when(f == pl.num_programs(1) - 1)
    def _():
        o_ref[...] = h_ref[...] + 0.5 * acc_ref[...]


def _ffn_half(h, g, wg, wu, wd, *, ff_tile=512):
    m, d = h.shape
    dff = wg.shape[1]
    grid = (m // ROW_TILE, dff // ff_tile)
    return pl.pallas_call(
        _ffn_kernel,
        grid=grid,
        in_specs=[
            pl.BlockSpec((ROW_TILE, d), lambda i, f: (i, 0)),
            pl.BlockSpec((1, d), lambda i, f: (0, 0)),
            pl.BlockSpec((d, ff_tile), lambda i, f: (0, f)),
            pl.BlockSpec((d, ff_tile), lambda i, f: (0, f)),
            pl.BlockSpec((ff_tile, d), lambda i, f: (f, 0)),
        ],
        out_specs=pl.BlockSpec((ROW_TILE, d), lambda i, f: (i, 0)),
        out_shape=jax.ShapeDtypeStruct((m, d), jnp.float32),
        scratch_shapes=[
            pltpu.VMEM((ROW_TILE, d), jnp.bfloat16),
            pltpu.VMEM((ROW_TILE, d), jnp.float32),
        ],
        compiler_params=pltpu.CompilerParams(
            dimension_semantics=("parallel", "arbitrary"),
            vmem_limit_bytes=VMEM_LIMIT),
    )(h, g.reshape(1, d), wg, wu, wd)


def _norm_proj_kernel(h_ref, g_ref, w_ref, o_ref, xn_ref):
    @pl.when(pl.program_id(1) == 0)
    def _():
        xn_ref[...] = _rms_rows(h_ref[...], g_ref[...]).astype(jnp.bfloat16)

    o_ref[...] = jnp.dot(xn_ref[...], w_ref[...], preferred_element_type=jnp.float32)


PROJ_TILE_MAX = 1536


def _norm_proj(h, g, w):
    m, d = h.shape
    n = w.shape[1]
    n_tile = max(t for t in range(LANE, PROJ_TILE_MAX + 1, LANE) if n % t == 0)
    grid = (m // ROW_TILE, n // n_tile)
    return pl.pallas_call(
        _norm_proj_kernel,
        grid=grid,
        in_specs=[
            pl.BlockSpec((ROW_TILE, d), lambda i, j: (i, 0)),
            pl.BlockSpec((1, d), lambda i, j: (0, 0)),
            pl.BlockSpec((d, n_tile), lambda i, j: (0, j)),
        ],
        out_specs=pl.BlockSpec((ROW_TILE, n_tile), lambda i, j: (i, j)),
        out_shape=jax.ShapeDtypeStruct((m, n), jnp.float32),
        scratch_shapes=[pltpu.VMEM((ROW_TILE, d), jnp.bfloat16)],
        compiler_params=pltpu.CompilerParams(
            dimension_semantics=("parallel", "arbitrary"),
            vmem_limit_bytes=VMEM_LIMIT),
    )(h, g.reshape(1, d), w)


def _proj_residual_kernel(y_ref, w_ref, h_ref, o_ref):
    o_ref[...] = h_ref[...] + jnp.dot(
        y_ref[...].astype(jnp.bfloat16), w_ref[...], preferred_element_type=jnp.float32)


def _proj_residual(y, w, h, *, n_tile=512):
    m, k = y.shape
    d = w.shape[1]
    grid = (m // ROW_TILE, d // n_tile)
    return pl.pallas_call(
        _proj_residual_kernel,
        grid=grid,
        in_specs=[
            pl.BlockSpec((ROW_TILE, k), lambda i, j: (i, 0)),
            pl.BlockSpec((k, n_tile), lambda i, j: (0, j)),
            pl.BlockSpec((ROW_TILE, n_tile), lambda i, j: (i, j)),
        ],
        out_specs=pl.BlockSpec((ROW_TILE, n_tile), lambda i, j: (i, j)),
        out_shape=jax.ShapeDtypeStruct((m, d), jnp.float32),
        compiler_params=pltpu.CompilerParams(
            dimension_semantics=("parallel", "arbitrary"),
            vmem_limit_bytes=VMEM_LIMIT),
    )(y, w, h)


def _final_norm_kernel(h_ref, g_ref, o_ref):
    o_ref[...] = _rms_rows(h_ref[...], g_ref[...])


def _final_norm(h, g):
    m, d = h.shape
    return pl.pallas_call(
        _final_norm_kernel,
        grid=(m // ROW_TILE,),
        in_specs=[pl.BlockSpec((ROW_TILE, d), lambda i: (i, 0)),
                  pl.BlockSpec((1, d), lambda i: (0, 0))],
        out_specs=pl.BlockSpec((ROW_TILE, d), lambda i: (i, 0)),
        out_shape=jax.ShapeDtypeStruct((m, d), jnp.float32),
        compiler_params=pltpu.CompilerParams(dimension_semantics=("parallel",)),
    )(h, g.reshape(1, d))


ATT_TILE = 256
HEAD_ROWS = Q_PER_KV * ATT_TILE
SCALE = HEAD_DIM ** -0.5
CMP_BAND_START = LANE - ATT_TILE // CMP_STRIDE
_NT = (((1,), (1,)), ((), ()))


def _bucket_upper_bounds():
    n = np.arange(0, 2 * REL_MAX_DIST)
    ratio = np.log(np.maximum(n, 1).astype(np.float32) / np.float32(REL_MAX_EXACT)) / np.float32(
        math.log(REL_MAX_DIST / REL_MAX_EXACT))
    large = np.minimum(REL_MAX_EXACT + (ratio * np.float32(REL_BUCKETS - REL_MAX_EXACT)).astype(np.int32),
                       REL_BUCKETS - 1)
    bucket = np.where(n < REL_MAX_EXACT, n, large)
    return [int(n[bucket > b].min()) for b in range(REL_BUCKETS - 1)]


_BUCKET_HI = _bucket_upper_bounds()


def _bias_from_dist(dist, head_vals):
    accs = [jnp.zeros(dist.shape, jnp.float32) + vals[REL_BUCKETS - 1] for vals in head_vals]
    for b in range(REL_BUCKETS - 2, -1, -1):
        near = dist < _BUCKET_HI[b]
        accs = [jnp.where(near, vals[b], acc) for vals, acc in zip(head_vals, accs)]
    return accs


def _head_vals(rb_ref, g):
    return [[rb_ref[b, g * Q_PER_KV + r] for b in range(REL_BUCKETS)] for r in range(Q_PER_KV)]


def _build_bias_tiles(rb_ref, g, bias_ref):
    row = lax.broadcasted_iota(jnp.int32, (ATT_TILE, ATT_TILE), 0)
    col = lax.broadcasted_iota(jnp.int32, (ATT_TILE, ATT_TILE), 1)
    vals = _head_vals(rb_ref, g)
    for kind in range(2):
        tiles = _bias_from_dist(row - col + kind * ATT_TILE, vals)
        for r in range(Q_PER_KV):
            bias_ref[kind, r * ATT_TILE:(r + 1) * ATT_TILE, :] = tiles[r]
    for r in range(Q_PER_KV):
        bias_ref[2, r * ATT_TILE:(r + 1) * ATT_TILE, :] = (
            jnp.zeros((ATT_TILE, ATT_TILE), jnp.float32) + vals[r][REL_BUCKETS - 1])


def _stack_heads(q_ref):
    return jnp.concatenate(
        [q_ref[:, r * HEAD_DIM:(r + 1) * HEAD_DIM] for r in range(Q_PER_KV)], axis=0).astype(jnp.bfloat16)


def _tile4(x):
    return jnp.concatenate([x] * Q_PER_KV, axis=0)


def _flash_first(qs, k_tile, v_tile, bias, mask):
    s = lax.dot_general(qs, k_tile, _NT, preferred_element_type=jnp.float32) * SCALE + bias
    s = jnp.where(mask, s, NEG_INF)
    m = jnp.max(s, axis=-1, keepdims=True)
    p = jnp.exp(s - m)
    l = jnp.sum(p, axis=-1, keepdims=True)
    acc = jnp.dot(p.astype(jnp.bfloat16), v_tile, preferred_element_type=jnp.float32)
    return m, l, acc


def _flash_next(qs, k_tile, v_tile, bias, mask, m, l, acc):
    s = lax.dot_general(qs, k_tile, _NT, preferred_element_type=jnp.float32) * SCALE + bias
    s = jnp.where(mask, s, NEG_INF)
    m_new = jnp.maximum(m, jnp.max(s, axis=-1, keepdims=True))
    alpha = jnp.exp(m - m_new)
    p = jnp.exp(s - m_new)
    l = alpha * l + jnp.sum(p, axis=-1, keepdims=True)
    acc = alpha * acc + jnp.dot(p.astype(jnp.bfloat16), v_tile, preferred_element_type=jnp.float32)
    return m_new, l, acc


def _kv_tile(ref, j):
    return ref[pl.ds(pl.multiple_of(j * ATT_TILE, ATT_TILE), ATT_TILE), :].astype(jnp.bfloat16)


def _rank_select(score, blk, top_k):
    cnt = jnp.zeros(score.shape, jnp.float32)
    for m in range(score.shape[0]):
        row = score[m:m + 1, :]
        beats = jnp.where(row > score, 1.0, jnp.where((row == score) & (blk > m), 1.0, 0.0))
        cnt = cnt + beats
    return cnt < top_k


def _moba_kernel(rb_ref, q_ref, k_ref, v_ref, o_ref, bias_ref, kmean_ref, *, n_tiles):
    g = pl.program_id(1)
    t = pl.program_id(2)
    n_blk = kmean_ref.shape[0]

    @pl.when(t == 0)
    def _():
        _build_bias_tiles(rb_ref, g, bias_ref)
        kmean_ref[...] = jnp.zeros_like(kmean_ref)
        for n in range(n_tiles):
            kmean_ref[n:n + 1, :] = jnp.mean(k_ref[n * ATT_TILE:(n + 1) * ATT_TILE, :], axis=0, keepdims=True)

    qs = _stack_heads(q_ref)
    gate = lax.dot_general(kmean_ref[...].astype(jnp.bfloat16), qs, _NT, preferred_element_type=jnp.float32)
    blk = lax.broadcasted_iota(jnp.int32, gate.shape, 0)
    past = blk < t
    gate = jnp.where(past, gate, NEG_INF)
    chosen = _rank_select(gate, blk, MOBA_TOPK) & past
    sel = jnp.concatenate(
        [jnp.where(chosen, 1.0, 0.0), jnp.zeros((LANE - n_blk, HEAD_ROWS), jnp.float32)], axis=0)
    sel = sel.T.astype(jnp.bfloat16)

    row = lax.broadcasted_iota(jnp.int32, (HEAD_ROWS, ATT_TILE), 0) & (ATT_TILE - 1)
    col = lax.broadcasted_iota(jnp.int32, (HEAD_ROWS, ATT_TILE), 1)
    m, l, acc = _flash_first(qs, _kv_tile(k_ref, t), _kv_tile(v_ref, t), bias_ref[0], col <= row)

    def body(j, carry):
        pick = jnp.where(lax.broadcasted_iota(jnp.int32, (LANE, ATT_TILE), 0) == j, 1.0, 0.0).astype(jnp.bfloat16)
        mask = jnp.dot(sel, pick, preferred_element_type=jnp.float32) > 0.5
        bias = bias_ref[jnp.where(j == t - 1, 1, 2)]
        return _flash_next(qs, _kv_tile(k_ref, j), _kv_tile(v_ref, j), bias, mask, *carry)

    m, l, acc = lax.fori_loop(0, t, body, (m, l, acc))
    out = acc / jnp.maximum(l, 1e-30)
    for r in range(Q_PER_KV):
        o_ref[:, r * HEAD_DIM:(r + 1) * HEAD_DIM] = out[r * ATT_TILE:(r + 1) * ATT_TILE]


def _moba_attention(proj, rel_bias, n_batch, seq):
    n_tiles = seq // ATT_TILE
    k_col = Q_DIM // HEAD_DIM
    v_col = (Q_DIM + KV_DIM) // HEAD_DIM
    q_cols = Q_PER_KV * HEAD_DIM
    return pl.pallas_call(
        functools.partial(_moba_kernel, n_tiles=n_tiles),
        grid=(n_batch, N_KV_HEADS, n_tiles),
        in_specs=[
            pl.BlockSpec(memory_space=pltpu.SMEM),
            pl.BlockSpec((ATT_TILE, q_cols), lambda b, g, t: (b * n_tiles + t, g)),
            pl.BlockSpec((seq, HEAD_DIM), lambda b, g, t: (b, k_col + g)),
            pl.BlockSpec((seq, HEAD_DIM), lambda b, g, t: (b, v_col + g)),
        ],
        out_specs=pl.BlockSpec((ATT_TILE, q_cols), lambda b, g, t: (b * n_tiles + t, g)),
        out_shape=jax.ShapeDtypeStruct((n_batch * seq, Q_DIM), jnp.float32),
        scratch_shapes=[
            pltpu.VMEM((3, HEAD_ROWS, ATT_TILE), jnp.float32),
            pltpu.VMEM((max(n_tiles, 16), HEAD_DIM), jnp.float32),
        ],
        compiler_params=pltpu.CompilerParams(
            dimension_semantics=("parallel", "parallel", "arbitrary"),
            vmem_limit_bytes=VMEM_LIMIT),
    )(rel_bias, proj, proj, proj)


def _nsa_compress_kernel(k_ref, v_ref, pe_ref, w_ref, kc_ref, vc_ref, *, n_cmp):
    for c, (x_ref, o_ref) in enumerate(((k_ref, kc_ref), (v_ref, vc_ref))):
        top = jnp.zeros((n_cmp, HEAD_DIM), jnp.float32)
        bot = jnp.zeros((n_cmp, HEAD_DIM), jnp.float32)
        for r in range(CMP_STRIDE):
            x = x_ref[pl.ds(r, n_cmp, stride=CMP_STRIDE), :]
            xt = (x + pe_ref[c, r:r + 1, :]).astype(jnp.bfloat16)
            xb = (x + pe_ref[c, CMP_STRIDE + r:CMP_STRIDE + r + 1, :]).astype(jnp.bfloat16)
            top = top + jnp.dot(xt, w_ref[c, r * HEAD_DIM:(r + 1) * HEAD_DIM, :], preferred_element_type=jnp.float32)
            bot = bot + jnp.dot(xb, w_ref[c, (CMP_STRIDE + r) * HEAD_DIM:(CMP_STRIDE + r + 1) * HEAD_DIM, :],
                                preferred_element_type=jnp.float32)
        o_ref[0, 0] = top + pltpu.roll(bot, n_cmp - 1, axis=0)


def _nsa_compress_prompt(proj, cmp_pe, cmp_w, n_batch, seq):
    n_cmp = seq // CMP_STRIDE
    k_col = Q_DIM // HEAD_DIM
    v_col = (Q_DIM + KV_DIM) // HEAD_DIM
    out = jax.ShapeDtypeStruct((n_batch, N_KV_HEADS, n_cmp, HEAD_DIM), jnp.float32)
    return pl.pallas_call(
        functools.partial(_nsa_compress_kernel, n_cmp=n_cmp),
        grid=(n_batch, N_KV_HEADS),
        in_specs=[
            pl.BlockSpec((seq, HEAD_DIM), lambda b, g: (b, k_col + g)),
            pl.BlockSpec((seq, HEAD_DIM), lambda b, g: (b, v_col + g)),
            pl.BlockSpec((2, CMP_BLOCK, HEAD_DIM), lambda b, g: (0, 0, 0)),
            pl.BlockSpec((2, CMP_BLOCK * HEAD_DIM, HEAD_DIM), lambda b, g: (0, 0, 0)),
        ],
        out_specs=[pl.BlockSpec((1, 1, n_cmp, HEAD_DIM), lambda b, g: (b, g, 0, 0))] * 2,
        out_shape=[out, out],
        compiler_params=pltpu.CompilerParams(
            dimension_semantics=("parallel", "parallel"), vmem_limit_bytes=VMEM_LIMIT),
    )(proj, proj, cmp_pe, cmp_w)


def _nsa_kernel(rb_ref, q_ref, gate_ref, kc_ref, vc_ref, ks_ref, vs_ref, kw_ref, vw_ref, ovt_ref, o_ref,
                bias_ref, band_ref, *, n_cmp, n_sel):
    g = pl.program_id(1)
    t = pl.program_id(2)
    tile_tokens = ATT_TILE // CMP_STRIDE

    @pl.when(t == 0)
    def _():
        _build_bias_tiles(rb_ref, g, bias_ref)
        i = lax.broadcasted_iota(jnp.int32, (ATT_TILE, LANE), 0)
        m = lax.broadcasted_iota(jnp.int32, (ATT_TILE, LANE), 1)
        band = _bias_from_dist(i - CMP_STRIDE * (m - CMP_BAND_START) - (CMP_BLOCK - 1), _head_vals(rb_ref, g))
        for r in range(Q_PER_KV):
            band_ref[r * ATT_TILE:(r + 1) * ATT_TILE, :] = band[r]

    qs = _stack_heads(q_ref)

    q_pos = t * ATT_TILE + lax.broadcasted_iota(jnp.int32, (ATT_TILE, n_cmp), 0)
    c_end = lax.broadcasted_iota(jnp.int32, (ATT_TILE, n_cmp), 1) * CMP_STRIDE + (CMP_BLOCK - 1)
    ok_c = _tile4(c_end <= q_pos)
    shift = (t * tile_tokens + n_cmp - CMP_BAND_START) % n_cmp
    bias_parts = []
    for r in range(Q_PER_KV):
        window = band_ref[r * ATT_TILE:(r + 1) * ATT_TILE, :]
        if n_cmp > LANE:
            far = jnp.zeros((ATT_TILE, n_cmp - LANE), jnp.float32) + rb_ref[REL_BUCKETS - 1, g * Q_PER_KV + r]
            window = jnp.concatenate([window, far], axis=1)
        bias_parts.append(pltpu.roll(window, shift, axis=1))
    bias_c = jnp.concatenate(bias_parts, axis=0)
    lc = lax.dot_general(qs, kc_ref[0, 0].astype(jnp.bfloat16), _NT, preferred_element_type=jnp.float32) * SCALE + bias_c
    lc = jnp.where(ok_c, lc, NEG_INF)
    e = jnp.where(ok_c, jnp.exp(lc - jnp.max(lc, axis=-1, keepdims=True)), 0.0)
    pc = e / jnp.maximum(jnp.sum(e, axis=-1, keepdims=True), 1e-30)
    o_cmp = jnp.dot(pc.astype(jnp.bfloat16), vc_ref[0, 0].astype(jnp.bfloat16), preferred_element_type=jnp.float32)

    p_sum = pc[0:ATT_TILE]
    for r in range(1, Q_PER_KV):
        p_sum = p_sum + pc[r * ATT_TILE:(r + 1) * ATT_TILE]
    imp = lax.dot_general(ovt_ref[...], p_sum.astype(jnp.bfloat16), _NT, preferred_element_type=jnp.float32)
    imp = imp[:n_sel]
    blk = lax.broadcasted_iota(jnp.int32, (n_sel, ATT_TILE), 0)
    own = t * (ATT_TILE // SEL_BLOCK) + (lax.broadcasted_iota(jnp.int32, (n_sel, ATT_TILE), 1) // SEL_BLOCK)
    diff = own - blk
    valid = diff >= 0
    forced = valid & ((blk == 0) | (diff < N_LOCAL_SEL))
    score = jnp.where(forced, FORCE_SCORE, jnp.where(valid, imp, NEG_INF))
    chosen = jnp.where(_rank_select(score, blk, SEL_TOPK) & valid, 1.0, 0.0)
    if n_sel < LANE:
        chosen = jnp.concatenate([chosen, jnp.zeros((LANE - n_sel, ATT_TILE), jnp.float32)], axis=0)
    sel = chosen.T.astype(jnp.bfloat16)

    def sel_mask(j):
        pick = jnp.where(
            lax.broadcasted_iota(jnp.int32, (LANE, ATT_TILE), 0)
            == j * (ATT_TILE // SEL_BLOCK) + lax.broadcasted_iota(jnp.int32, (LANE, ATT_TILE), 1) // SEL_BLOCK,
            1.0, 0.0).astype(jnp.bfloat16)
        return jnp.dot(sel, pick, preferred_element_type=jnp.float32) > 0.5

    row1 = lax.broadcasted_iota(jnp.int32, (ATT_TILE, ATT_TILE), 0)
    col1 = lax.broadcasted_iota(jnp.int32, (ATT_TILE, ATT_TILE), 1)
    causal = col1 <= row1
    m, l, acc = _flash_first(qs, _kv_tile(ks_ref, t), _kv_tile(vs_ref, t), bias_ref[0],
                             _tile4(sel_mask(t) & causal))

    def body(j, carry):
        bias = bias_ref[jnp.where(j == t - 1, 1, 2)]
        return _flash_next(qs, _kv_tile(ks_ref, j), _kv_tile(vs_ref, j), bias, _tile4(sel_mask(j)), *carry)

    m, l, acc = lax.fori_loop(0, t, body, (m, l, acc))
    o_sel = acc / jnp.maximum(l, 1e-30)

    m, l, acc = _flash_first(qs, _kv_tile(kw_ref, t), _kv_tile(vw_ref, t), bias_ref[0], _tile4(causal))
    t1 = jnp.maximum(t - 1, 0)
    m, l, acc = _flash_next(qs, _kv_tile(kw_ref, t1), _kv_tile(vw_ref, t1), bias_ref[1],
                            _tile4((row1 >= 0) & (t >= 1)), m, l, acc)
    t2 = jnp.maximum(t - 2, 0)
    m, l, acc = _flash_next(qs, _kv_tile(kw_ref, t2), _kv_tile(vw_ref, t2), bias_ref[2],
                            _tile4((col1 > row1) & (t >= 2)), m, l, acc)
    o_win = acc / jnp.maximum(l, 1e-30)

    gates = jax.nn.sigmoid(gate_ref[...])
    lane = lax.broadcasted_iota(jnp.int32, gates.shape, 1)
    for r in range(Q_PER_KV):
        rows = slice(r * ATT_TILE, (r + 1) * ATT_TILE)
        out = jnp.zeros((ATT_TILE, HEAD_DIM), jnp.float32)
        for c, branch in enumerate((o_cmp, o_sel, o_win)):
            gcol = jnp.sum(jnp.where(lane == g * (3 * Q_PER_KV) + r * 3 + c, gates, 0.0), axis=-1, keepdims=True)
            out = out + gcol * branch[rows]
        o_ref[:, r * HEAD_DIM:(r + 1) * HEAD_DIM] = out


def _nsa_overlap_t(n_cmp):
    ci = np.arange(n_cmp)[None, :]
    sj = np.arange(LANE)[:, None]
    ov = (ci * CMP_STRIDE < (sj + 1) * SEL_BLOCK) & (ci * CMP_STRIDE + CMP_BLOCK > sj * SEL_BLOCK) & (ci < n_cmp - 1)
    return jnp.asarray(ov, jnp.bfloat16)


def _nsa_attention(proj, kc, vc, rel_bias, n_batch, seq):
    n_tiles = seq // ATT_TILE
    n_cmp = seq // CMP_STRIDE
    q_cols = Q_PER_KV * HEAD_DIM
    kv0 = Q_DIM // HEAD_DIM
    per = KV_DIM // HEAD_DIM
    gate_col = (Q_DIM + 6 * KV_DIM) // LANE

    def kv_spec(comp):
        return pl.BlockSpec((seq, HEAD_DIM), lambda b, g, t: (b, kv0 + comp * per + g))

    cmp_spec = pl.BlockSpec((1, 1, n_cmp, HEAD_DIM), lambda b, g, t: (b, g, 0, 0))
    return pl.pallas_call(
        functools.partial(_nsa_kernel, n_cmp=n_cmp, n_sel=seq // SEL_BLOCK),
        grid=(n_batch, N_KV_HEADS, n_tiles),
        in_specs=[
            pl.BlockSpec(memory_space=pltpu.SMEM),
            pl.BlockSpec((ATT_TILE, q_cols), lambda b, g, t: (b * n_tiles + t, g)),
            pl.BlockSpec((ATT_TILE, LANE), lambda b, g, t: (b * n_tiles + t, gate_col)),
            cmp_spec, cmp_spec,
            kv_spec(2), kv_spec(3), kv_spec(4), kv_spec(5),
            pl.BlockSpec((LANE, n_cmp), lambda b, g, t: (0, 0)),
        ],
        out_specs=pl.BlockSpec((ATT_TILE, q_cols), lambda b, g, t: (b * n_tiles + t, g)),
        out_shape=jax.ShapeDtypeStruct((n_batch * seq, Q_DIM), jnp.float32),
        scratch_shapes=[pltpu.VMEM((3, HEAD_ROWS, ATT_TILE), jnp.float32),
                        pltpu.VMEM((HEAD_ROWS, LANE), jnp.float32)],
        compiler_params=pltpu.CompilerParams(
            dimension_semantics=("parallel", "parallel", "arbitrary"),
            vmem_limit_bytes=VMEM_LIMIT),
    )(rel_bias, proj, proj, kc, vc, proj, proj, proj, proj, _nsa_overlap_t(n_cmp))


NEW_ROWS = Q_PER_KV * DEC_SEQ


def _rank_select_rows(score, n_cand, top_k):
    lane = lax.broadcasted_iota(jnp.int32, score.shape, 1)
    cnt = jnp.zeros(score.shape, jnp.float32)
    for m in range(n_cand):
        col = score[:, m:m + 1]
        cnt = cnt + jnp.where(col > score, 1.0, jnp.where((col == score) & (lane > m), 1.0, 0.0))
    return cnt < top_k


def _stack_group_heads(q_ref, g):
    return jnp.concatenate(
        [q_ref[:, (g * Q_PER_KV + r) * HEAD_DIM:(g * Q_PER_KV + r + 1) * HEAD_DIM] for r in range(Q_PER_KV)],
        axis=0).astype(jnp.bfloat16)


def _group_bias(rb_ref, g, dist):
    return jnp.concatenate(_bias_from_dist(dist, _head_vals(rb_ref, g)), axis=0)


def _flash_update(refs, g, qs, k_tile, v_tile, bias, mask, first):
    m_ref, l_ref, acc_ref = refs
    if first:
        m, l, acc = _flash_first(qs, k_tile, v_tile, bias, mask)
    else:
        m, l, acc = _flash_next(qs, k_tile, v_tile, bias, mask, m_ref[g], l_ref[g], acc_ref[g])
    m_ref[g] = m
    l_ref[g] = l
    acc_ref[g] = acc


PAGES_PER_BLOCK = MOBA_BLOCK // PAGE_SIZE
KMEAN_PAGES = 8
MOBA_STEP_PAGES = 4
MOBA_STEP_KEYS = MOBA_STEP_PAGES * PAGE_SIZE


def _moba_kmean_kernel(pt_ref, *refs):
    o_ref = refs[KMEAN_PAGES]
    for i in range(KMEAN_PAGES // PAGES_PER_BLOCK):
        total = jnp.sum(refs[PAGES_PER_BLOCK * i][...], axis=0)
        for k in range(1, PAGES_PER_BLOCK):
            total = total + jnp.sum(refs[PAGES_PER_BLOCK * i + k][...], axis=0)
        o_ref[i] = total * (1.0 / MOBA_BLOCK)


def _page_spec(comp, which, per_step):
    return pl.BlockSpec((None, PAGE_SIZE, None, N_KV_HEADS, HEAD_DIM),
                        lambda b, n, pt: (pt[b, n * per_step + which], 0, comp, 0, 0))


def _moba_sample_kmean(cache, page_table):
    n_seq, n_pages = page_table.shape
    n_blk = n_pages // PAGES_PER_BLOCK
    blk_per_step = KMEAN_PAGES // PAGES_PER_BLOCK
    return pl.pallas_call(
        _moba_kmean_kernel,
        grid_spec=pltpu.PrefetchScalarGridSpec(
            num_scalar_prefetch=1,
            grid=(n_seq, n_pages // KMEAN_PAGES),
            in_specs=[_page_spec(0, k, KMEAN_PAGES) for k in range(KMEAN_PAGES)],
            out_specs=pl.BlockSpec((None, blk_per_step, N_KV_HEADS, HEAD_DIM), lambda b, n, pt: (b, n, 0, 0)),
        ),
        out_shape=jax.ShapeDtypeStruct((n_seq, n_blk, N_KV_HEADS, HEAD_DIM), jnp.float32),
        compiler_params=pltpu.CompilerParams(dimension_semantics=("parallel", "arbitrary")),
    )(page_table, *([cache] * KMEAN_PAGES))


def _moba_sample_kernel(pt_ref, rb_ref, q_ref, kn_ref, vn_ref, kmean_ref, *refs, n_blk, n_steps, past_len):
    k_refs = refs[:MOBA_STEP_PAGES]
    v_refs = refs[MOBA_STEP_PAGES:2 * MOBA_STEP_PAGES]
    o_ref, m_ref, l_ref, acc_ref, sel_ref = refs[2 * MOBA_STEP_PAGES:]
    n = pl.program_id(1)
    refs = (m_ref, l_ref, acc_ref)
    tok = lax.broadcasted_iota(jnp.int32, (DEC_SEQ, DEC_SEQ), 0)
    new = lax.broadcasted_iota(jnp.int32, (DEC_SEQ, DEC_SEQ), 1)

    @pl.when(n == 0)
    def _():
        for g in range(N_KV_HEADS):
            qs = _stack_group_heads(q_ref, g)
            gate = lax.dot_general(qs, kmean_ref[g].astype(jnp.bfloat16), _NT, preferred_element_type=jnp.float32)
            gate = jnp.concatenate([gate, jnp.full((NEW_ROWS, LANE - n_blk), NEG_INF, jnp.float32)], axis=1)
            lane = lax.broadcasted_iota(jnp.int32, gate.shape, 1)
            chosen = _rank_select_rows(gate, n_blk, MOBA_TOPK) & (lane < n_blk)
            sel_ref[g] = jnp.where(chosen, 1.0, 0.0)
            _flash_update(refs, g, qs, kn_ref[:, g * HEAD_DIM:(g + 1) * HEAD_DIM].astype(jnp.bfloat16),
                          vn_ref[:, g * HEAD_DIM:(g + 1) * HEAD_DIM].astype(jnp.bfloat16),
                          _group_bias(rb_ref, g, tok - new), _tile4(new <= tok), True)

    q_pos = past_len + lax.broadcasted_iota(jnp.int32, (DEC_SEQ, MOBA_STEP_KEYS), 0)
    k_pos = n * MOBA_STEP_KEYS + lax.broadcasted_iota(jnp.int32, (DEC_SEQ, MOBA_STEP_KEYS), 1)
    pick = jnp.where(
        lax.broadcasted_iota(jnp.int32, (LANE, MOBA_STEP_KEYS), 0)
        == n * (MOBA_STEP_KEYS // MOBA_BLOCK) + lax.broadcasted_iota(jnp.int32, (LANE, MOBA_STEP_KEYS), 1) // MOBA_BLOCK,
        1.0, 0.0).astype(jnp.bfloat16)
    for g in range(N_KV_HEADS):
        qs = _stack_group_heads(q_ref, g)
        k_tile = jnp.concatenate([ref[:, g, :] for ref in k_refs], axis=0).astype(jnp.bfloat16)
        v_tile = jnp.concatenate([ref[:, g, :] for ref in v_refs], axis=0).astype(jnp.bfloat16)
        mask = jnp.dot(sel_ref[g].astype(jnp.bfloat16), pick, preferred_element_type=jnp.float32) > 0.5
        _flash_update(refs, g, qs, k_tile, v_tile, _group_bias(rb_ref, g, q_pos - k_pos), mask, False)

    @pl.when(n == n_steps - 1)
    def _():
        for g in range(N_KV_HEADS):
            out = acc_ref[g] / jnp.maximum(l_ref[g], 1e-30)
            for r in range(Q_PER_KV):
                h = g * Q_PER_KV + r
                o_ref[:, h * HEAD_DIM:(h + 1) * HEAD_DIM] = out[r * DEC_SEQ:(r + 1) * DEC_SEQ]


def _moba_sample_attention(proj, kmean, cache, page_table, rel_bias, row_block0, past_len):
    n_seq, n_pages = page_table.shape
    n_blk = n_pages // PAGES_PER_BLOCK
    n_steps = n_pages // MOBA_STEP_PAGES
    kv_cols = KV_DIM
    return pl.pallas_call(
        functools.partial(_moba_sample_kernel, n_blk=n_blk, n_steps=n_steps, past_len=past_len),
        grid_spec=pltpu.PrefetchScalarGridSpec(
            num_scalar_prefetch=1,
            grid=(n_seq, n_steps),
            in_specs=[
                pl.BlockSpec(memory_space=pltpu.SMEM),
                pl.BlockSpec((DEC_SEQ, Q_DIM), lambda b, n, pt: (row_block0 + b, 0)),
                pl.BlockSpec((DEC_SEQ, kv_cols), lambda b, n, pt: (row_block0 + b, Q_DIM // kv_cols)),
                pl.BlockSpec((DEC_SEQ, kv_cols), lambda b, n, pt: (row_block0 + b, Q_DIM // kv_cols + 1)),
                pl.BlockSpec((None, N_KV_HEADS, n_blk, HEAD_DIM), lambda b, n, pt: (b, 0, 0, 0)),
            ] + [_page_spec(comp, k, MOBA_STEP_PAGES) for comp in range(2) for k in range(MOBA_STEP_PAGES)],
            out_specs=pl.BlockSpec((DEC_SEQ, Q_DIM), lambda b, n, pt: (b, 0)),
            scratch_shapes=[
                pltpu.VMEM((N_KV_HEADS, NEW_ROWS, 1), jnp.float32),
                pltpu.VMEM((N_KV_HEADS, NEW_ROWS, 1), jnp.float32),
                pltpu.VMEM((N_KV_HEADS, NEW_ROWS, HEAD_DIM), jnp.float32),
                pltpu.VMEM((N_KV_HEADS, NEW_ROWS, LANE), jnp.float32),
            ],
        ),
        out_shape=jax.ShapeDtypeStruct((n_seq * DEC_SEQ, Q_DIM), jnp.float32),
        compiler_params=pltpu.CompilerParams(dimension_semantics=("parallel", "arbitrary")),
    )(page_table, rel_bias, proj, proj, proj, kmean, *([cache] * (2 * MOBA_STEP_PAGES)))


CMP_PAGES = 8
SEL_STEP_PAGES = 4
SEL_STEP_KEYS = SEL_STEP_PAGES * PAGE_SIZE
CHUNKS_PER_PAGE = PAGE_SIZE // CMP_STRIDE


def _nsa_sample_compress_kernel(pt_ref, *refs):
    page_refs = refs[:CMP_PAGES]
    pe_ref, w_ref, top_ref, bot_ref = refs[CMP_PAGES:]
    rows = CMP_PAGES * CHUNKS_PER_PAGE
    for c in range(2):
        top = jnp.zeros((N_KV_HEADS * rows, HEAD_DIM), jnp.float32)
        bot = jnp.zeros((N_KV_HEADS * rows, HEAD_DIM), jnp.float32)
        for r in range(CMP_STRIDE):
            x = jnp.concatenate(
                [ref[pl.ds(r, CHUNKS_PER_PAGE, stride=CMP_STRIDE), c, g, :]
                 for g in range(N_KV_HEADS) for ref in page_refs], axis=0)
            xt = (x + pe_ref[c, r:r + 1, :]).astype(jnp.bfloat16)
            xb = (x + pe_ref[c, CMP_STRIDE + r:CMP_STRIDE + r + 1, :]).astype(jnp.bfloat16)
            top = top + jnp.dot(xt, w_ref[c, r * HEAD_DIM:(r + 1) * HEAD_DIM, :], preferred_element_type=jnp.float32)
            bot = bot + jnp.dot(xb, w_ref[c, (CMP_STRIDE + r) * HEAD_DIM:(CMP_STRIDE + r + 1) * HEAD_DIM, :],
                                preferred_element_type=jnp.float32)
        for g in range(N_KV_HEADS):
            top_ref[c, g] = top[g * rows:(g + 1) * rows]
            bot_ref[c, g] = bot[g * rows:(g + 1) * rows]


def _nsa_sample_compress(cache, page_table, cmp_pe, cmp_w):
    n_seq, n_pages = page_table.shape
    n_chunks = n_pages * CHUNKS_PER_PAGE
    rows = CMP_PAGES * CHUNKS_PER_PAGE

    def page_spec(k):
        return pl.BlockSpec((None, PAGE_SIZE, 2, N_KV_HEADS, HEAD_DIM),
                            lambda b, s, pt: (pt[b, s * CMP_PAGES + k], 0, 0, 0, 0))

    out = jax.ShapeDtypeStruct((n_seq, 2, N_KV_HEADS, n_chunks, HEAD_DIM), jnp.float32)
    out_spec = pl.BlockSpec((None, 2, N_KV_HEADS, rows, HEAD_DIM), lambda b, s, pt: (b, 0, 0, s, 0))
    return pl.pallas_call(
        _nsa_sample_compress_kernel,
        grid_spec=pltpu.PrefetchScalarGridSpec(
            num_scalar_prefetch=1,
            grid=(n_seq, n_pages // CMP_PAGES),
            in_specs=[page_spec(k) for k in range(CMP_PAGES)] + [
                pl.BlockSpec((2, CMP_BLOCK, HEAD_DIM), lambda b, s, pt: (0, 0, 0)),
                pl.BlockSpec((2, CMP_BLOCK * HEAD_DIM, HEAD_DIM), lambda b, s, pt: (0, 0, 0)),
            ],
            out_specs=[out_spec, out_spec],
        ),
        out_shape=[out, out],
        compiler_params=pltpu.CompilerParams(
            dimension_semantics=("parallel", "parallel"), vmem_limit_bytes=VMEM_LIMIT),
    )(page_table, *([cache] * CMP_PAGES), cmp_pe, cmp_w)


def _nsa_sample_kernel(pt_ref, rb_ref, q_ref, gate_ref, ksn_ref, vsn_ref, kwn_ref, vwn_ref, top_ref, bot_ref,
                       ov_ref, win_ref, *refs, n_pages, past_len):
    page_refs = refs[:SEL_STEP_PAGES]
    o_ref, m_ref, l_ref, acc_ref, sel_ref, ocmp_ref, owin_ref = refs[SEL_STEP_PAGES:]
    p = pl.program_id(1)
    refs = (m_ref, l_ref, acc_ref)
    n_chunks = n_pages * CHUNKS_PER_PAGE
    n_sel_pad = ov_ref.shape[1]
    n_sel = past_len // SEL_BLOCK + 1
    win_len = win_ref.shape[0]
    tok = lax.broadcasted_iota(jnp.int32, (DEC_SEQ, DEC_SEQ), 0)
    new = lax.broadcasted_iota(jnp.int32, (DEC_SEQ, DEC_SEQ), 1)

    def group_cols(ref, g):
        return ref[:, g * HEAD_DIM:(g + 1) * HEAD_DIM].astype(jnp.bfloat16)

    @pl.when(p == 0)
    def _():
        for g in range(N_KV_HEADS):
            qs = _stack_group_heads(q_ref, g)
            kc = top_ref[0, g] + pltpu.roll(bot_ref[0, g], n_chunks - 1, axis=0)
            vc = top_ref[1, g] + pltpu.roll(bot_ref[1, g], n_chunks - 1, axis=0)
            q_pos = past_len + lax.broadcasted_iota(jnp.int32, (DEC_SEQ, n_chunks), 0)
            tok_id = lax.broadcasted_iota(jnp.int32, (DEC_SEQ, n_chunks), 1)
            c_end = tok_id * CMP_STRIDE + (CMP_BLOCK - 1)
            ok_c = _tile4((tok_id < n_chunks - 1) & (c_end <= q_pos))
            lc = lax.dot_general(qs, kc.astype(jnp.bfloat16), _NT, preferred_element_type=jnp.float32) * SCALE
            lc = jnp.where(ok_c, lc + _group_bias(rb_ref, g, q_pos - c_end), NEG_INF)
            e = jnp.where(ok_c, jnp.exp(lc - jnp.max(lc, axis=-1, keepdims=True)), 0.0)
            pc = e / jnp.maximum(jnp.sum(e, axis=-1, keepdims=True), 1e-30)
            ocmp_ref[g] = jnp.dot(pc.astype(jnp.bfloat16), vc.astype(jnp.bfloat16), preferred_element_type=jnp.float32)
            p_sum = pc[0:DEC_SEQ]
            for r in range(1, Q_PER_KV):
                p_sum = p_sum + pc[r * DEC_SEQ:(r + 1) * DEC_SEQ]
            imp = jnp.dot(p_sum.astype(jnp.bfloat16), ov_ref[...], preferred_element_type=jnp.float32)
            blk = lax.broadcasted_iota(jnp.int32, (DEC_SEQ, n_sel_pad), 1)
            own = (past_len + lax.broadcasted_iota(jnp.int32, (DEC_SEQ, n_sel_pad), 0)) // SEL_BLOCK
            diff = own - blk
            valid = diff >= 0
            forced = valid & ((blk == 0) | (diff < N_LOCAL_SEL))
            score = jnp.where(forced, FORCE_SCORE, jnp.where(valid, imp, NEG_INF))
            chosen = _rank_select_rows(score, n_sel, SEL_TOPK) & valid
            sel_ref[g] = _tile4(jnp.where(chosen, 1.0, 0.0))
            causal = _tile4(new <= tok)
            near = _group_bias(rb_ref, g, tok - new)
            _flash_update(refs, g, qs, group_cols(ksn_ref, g), group_cols(vsn_ref, g), near, causal, True)
            mw, lw, aw = _flash_first(qs, group_cols(kwn_ref, g), group_cols(vwn_ref, g), near, causal)
            tok_w = lax.broadcasted_iota(jnp.int32, (DEC_SEQ, win_len), 0)
            dist_w = win_len + tok_w - lax.broadcasted_iota(jnp.int32, (DEC_SEQ, win_len), 1)
            mw, lw, aw = _flash_next(qs, win_ref[:, 0, g, :].astype(jnp.bfloat16), win_ref[:, 1, g, :].astype(jnp.bfloat16),
                                     _group_bias(rb_ref, g, dist_w), _tile4(dist_w < WINDOW), mw, lw, aw)
            owin_ref[g] = aw / jnp.maximum(lw, 1e-30)

    q_pos = past_len + lax.broadcasted_iota(jnp.int32, (DEC_SEQ, SEL_STEP_KEYS), 0)
    k_pos = p * SEL_STEP_KEYS + lax.broadcasted_iota(jnp.int32, (DEC_SEQ, SEL_STEP_KEYS), 1)
    pick = jnp.where(
        lax.broadcasted_iota(jnp.int32, (n_sel_pad, SEL_STEP_KEYS), 0)
        == p * (SEL_STEP_KEYS // SEL_BLOCK) + lax.broadcasted_iota(jnp.int32, (n_sel_pad, SEL_STEP_KEYS), 1) // SEL_BLOCK,
        1.0, 0.0).astype(jnp.bfloat16)
    for g in range(N_KV_HEADS):
        qs = _stack_group_heads(q_ref, g)
        k_tile = jnp.concatenate([ref[:, 0, g, :] for ref in page_refs], axis=0).astype(jnp.bfloat16)
        v_tile = jnp.concatenate([ref[:, 1, g, :] for ref in page_refs], axis=0).astype(jnp.bfloat16)
        mask = jnp.dot(sel_ref[g].astype(jnp.bfloat16), pick, preferred_element_type=jnp.float32) > 0.5
        _flash_update(refs, g, qs, k_tile, v_tile, _group_bias(rb_ref, g, q_pos - k_pos), mask, False)

    @pl.when(p == n_pages // SEL_STEP_PAGES - 1)
    def _():
        gates = jax.nn.sigmoid(gate_ref[...])
        for g in range(N_KV_HEADS):
            o_sel = acc_ref[g] / jnp.maximum(l_ref[g], 1e-30)
            for r in range(Q_PER_KV):
                rows = slice(r * DEC_SEQ, (r + 1) * DEC_SEQ)
                col = (g * Q_PER_KV + r) * 3
                out = (gates[:, col:col + 1] * ocmp_ref[g][rows] + gates[:, col + 1:col + 2] * o_sel[rows]
                       + gates[:, col + 2:col + 3] * owin_ref[g][rows])
                h = g * Q_PER_KV + r
                o_ref[:, h * HEAD_DIM:(h + 1) * HEAD_DIM] = out


def _nsa_sample_overlap(n_chunks, n_sel_pad):
    ci = np.arange(n_chunks)[:, None]
    sj = np.arange(n_sel_pad)[None, :]
    ov = (ci * CMP_STRIDE < (sj + 1) * SEL_BLOCK) & (ci * CMP_STRIDE + CMP_BLOCK > sj * SEL_BLOCK) & (ci < n_chunks - 1)
    return jnp.asarray(ov, jnp.bfloat16)


def _nsa_sample(proj, cache, win_buf, page_table, cmp_pe, cmp_w, rel_bias, row_block0, past_len):
    n_seq, n_pages = page_table.shape
    n_chunks = n_pages * CHUNKS_PER_PAGE
    n_sel_pad = -(-(past_len // SEL_BLOCK + 1) // LANE) * LANE
    top, bot = _nsa_sample_compress(cache, page_table, cmp_pe, cmp_w)
    kv0 = Q_DIM // KV_DIM
    gate_col = (Q_DIM + 6 * KV_DIM) // LANE

    def new_spec(comp):
        return pl.BlockSpec((DEC_SEQ, KV_DIM), lambda b, p, pt: (row_block0 + b, kv0 + comp))

    def sel_page_spec(k):
        return pl.BlockSpec((None, PAGE_SIZE, 2, N_KV_HEADS, HEAD_DIM),
                            lambda b, p, pt: (pt[b, p * SEL_STEP_PAGES + k], 0, 1, 0, 0))

    cmp_spec = pl.BlockSpec((None, 2, N_KV_HEADS, n_chunks, HEAD_DIM), lambda b, p, pt: (b, 0, 0, 0, 0))
    scratch = [
        pltpu.VMEM((N_KV_HEADS, NEW_ROWS, 1), jnp.float32),
        pltpu.VMEM((N_KV_HEADS, NEW_ROWS, 1), jnp.float32),
        pltpu.VMEM((N_KV_HEADS, NEW_ROWS, HEAD_DIM), jnp.float32),
        pltpu.VMEM((N_KV_HEADS, NEW_ROWS, n_sel_pad), jnp.float32),
        pltpu.VMEM((N_KV_HEADS, NEW_ROWS, HEAD_DIM), jnp.float32),
        pltpu.VMEM((N_KV_HEADS, NEW_ROWS, HEAD_DIM), jnp.float32),
    ]
    return pl.pallas_call(
        functools.partial(_nsa_sample_kernel, n_pages=n_pages, past_len=past_len),
        grid_spec=pltpu.PrefetchScalarGridSpec(
            num_scalar_prefetch=1,
            grid=(n_seq, n_pages // SEL_STEP_PAGES),
            in_specs=[
                pl.BlockSpec(memory_space=pltpu.SMEM),
                pl.BlockSpec((DEC_SEQ, Q_DIM), lambda b, p, pt: (row_block0 + b, 0)),
                pl.BlockSpec((DEC_SEQ, LANE), lambda b, p, pt: (row_block0 + b, gate_col)),
                new_spec(2), new_spec(3), new_spec(4), new_spec(5),
                cmp_spec, cmp_spec,
                pl.BlockSpec((n_chunks, n_sel_pad), lambda b, p, pt: (0, 0)),
                pl.BlockSpec((None,) + win_buf.shape[1:], lambda b, p, pt: (b, 0, 0, 0, 0)),
            ] + [sel_page_spec(k) for k in range(SEL_STEP_PAGES)],
            out_specs=pl.BlockSpec((DEC_SEQ, Q_DIM), lambda b, p, pt: (b, 0)),
            scratch_shapes=scratch,
        ),
        out_shape=jax.ShapeDtypeStruct((n_seq * DEC_SEQ, Q_DIM), jnp.float32),
        compiler_params=pltpu.CompilerParams(
            dimension_semantics=("parallel", "arbitrary"), vmem_limit_bytes=VMEM_LIMIT),
    )(page_table, rel_bias, proj, proj, proj, proj, proj, proj, top, bot,
      _nsa_sample_overlap(n_chunks, n_sel_pad), win_buf, *([cache] * SEL_STEP_PAGES))


SSD_PAIR = 2 * SSD_HEAD_DIM
GROUP_COLS = HEADS_PER_GROUP * SSD_HEAD_DIM
CARRY_ROWS = 8


def _pad_rows(x, rows):
    if x.shape[0] == rows:
        return x
    return jnp.concatenate([x, jnp.zeros((rows - x.shape[0],) + x.shape[1:], x.dtype)], axis=0)


def _ssd_kernel(z_ref, x_ref, b_ref, c_ref, dt_ref, prev_ref, s0_ref, cw_ref, cb_ref, dtb_ref, alog_ref, dsk_ref,
                nw_ref, y_ref, conv_ref, state_ref, carry_ref, s_ref, act_ref, *, rows_in, n_chunks):
    c = pl.program_id(1)
    L = SSD_CHUNK

    @pl.when(c == 0)
    def _():
        carry_ref[...] = prev_ref[...]
        s_ref[...] = s0_ref[...]

    row8 = lax.broadcasted_iota(jnp.int32, (CARRY_ROWS, 1), 0)
    off = 0
    for ref in (x_ref, b_ref, c_ref):
        width = ref.shape[1]
        cols = slice(off, off + width)
        raw_in = ref[...]
        raw = _pad_rows(raw_in, L)
        prev = carry_ref[:, cols]
        acc = cb_ref[:, cols] + cw_ref[CONV_W - 1:CONV_W, cols] * raw
        for k in range(1, CONV_W):
            rolled = pltpu.roll(raw, k, axis=0)
            head = jnp.where(row8 < k, pltpu.roll(prev, k, axis=0), rolled[:CARRY_ROWS])
            shifted = jnp.concatenate([head, rolled[CARRY_ROWS:]], axis=0)
            acc = acc + cw_ref[CONV_W - 1 - k:CONV_W - k, cols] * shifted
        act_ref[:, cols] = acc * jax.nn.sigmoid(acc)
        carry_ref[:, cols] = raw_in[rows_in - CARRY_ROWS:rows_in]

        @pl.when(c == n_chunks - 1)
        def _():
            conv_ref[:, cols] = raw_in[rows_in - (CONV_W - 1):rows_in]

        off += width

    row = lax.broadcasted_iota(jnp.int32, (L, LANE), 0)
    pre = _pad_rows(dt_ref[...], L) + dtb_ref[...]
    dt = jnp.maximum(pre, 0.0) + jnp.log1p(jnp.exp(-jnp.abs(pre)))
    dt = jnp.where(row < rows_in, dt, 0.0)
    acum = dt * (-jnp.exp(alog_ref[...]))
    shift = 1
    while shift < L:
        acum = acum + jnp.where(row >= shift, pltpu.roll(acum, shift, axis=0), 0.0)
        shift *= 2
    acum_t = acum.T
    exp_a = jnp.exp(acum)
    to_end = jnp.exp(acum[L - 1:L, :] - acum)
    li = lax.broadcasted_iota(jnp.int32, (L, L), 0)
    si = lax.broadcasted_iota(jnp.int32, (L, L), 1)
    causal = li >= si
    first_half = lax.broadcasted_iota(jnp.int32, (L, SSD_PAIR), 1) < SSD_HEAD_DIM

    def pair_cols(v, h):
        return jnp.where(first_half, v[:, h:h + 1], v[:, h + 1:h + 2])

    for g in range(SSD_GROUPS):
        bg = act_ref[:, D_INNER + g * D_STATE:D_INNER + (g + 1) * D_STATE].astype(jnp.bfloat16)
        cg = act_ref[:, D_INNER + GN + g * D_STATE:D_INNER + GN + (g + 1) * D_STATE].astype(jnp.bfloat16)
        cb = lax.dot_general(cg, bg, _NT, preferred_element_type=jnp.float32)
        s_prev = s_ref[g * GROUP_COLS:(g + 1) * GROUP_COLS, :]
        y_off = lax.dot_general(cg, s_prev.astype(jnp.bfloat16), _NT, preferred_element_type=jnp.float32)
        xw_parts = []
        for pr in range(HEADS_PER_GROUP // 2):
            h = g * HEADS_PER_GROUP + 2 * pr
            cols = slice(h * SSD_HEAD_DIM, (h + 2) * SSD_HEAD_DIM)
            xa = act_ref[:, cols]
            xdt = xa * pair_cols(dt, h)
            y = y_off[:, 2 * pr * SSD_HEAD_DIM:(2 * pr + 2) * SSD_HEAD_DIM] * pair_cols(exp_a, h)
            for half in range(2):
                hh = h + half
                decay = jnp.where(causal, jnp.exp(acum[:, hh:hh + 1] - acum_t[hh:hh + 1, :]), 0.0)
                mine = first_half if half == 0 else jnp.logical_not(first_half)
                y = y + jnp.dot((cb * decay).astype(jnp.bfloat16), jnp.where(mine, xdt, 0.0).astype(jnp.bfloat16),
                                preferred_element_type=jnp.float32)
            zz = z_ref[:, cols]
            y = y[:rows_in] + xa[:rows_in] * dsk_ref[:, cols]
            y_ref[:, cols] = y * (zz * jax.nn.sigmoid(zz))
            xw_parts.append(xdt * pair_cols(to_end, h))
        xw = jnp.concatenate(xw_parts, axis=1)
        chunk_state = jnp.dot(xw.T.astype(jnp.bfloat16), bg, preferred_element_type=jnp.float32)
        for r in range(HEADS_PER_GROUP):
            h = g * HEADS_PER_GROUP + r
            rows = slice(h * SSD_HEAD_DIM, (h + 1) * SSD_HEAD_DIM)
            s_ref[rows, :] = (s_ref[rows, :] * jnp.exp(acum_t[h:h + 1, L - 1:L])
                              + chunk_state[r * SSD_HEAD_DIM:(r + 1) * SSD_HEAD_DIM])
        gcols = slice(g * GROUP_COLS, (g + 1) * GROUP_COLS)
        yg = y_ref[:, gcols]
        y_ref[:, gcols] = yg * lax.rsqrt(jnp.mean(yg * yg, axis=-1, keepdims=True) + RMS_EPS) * nw_ref[:, gcols]

    @pl.when(c == n_chunks - 1)
    def _():
        state_ref[...] = s_ref[...]


def _ssd_mix(proj, conv_prev, ssm_prev, conv_w, conv_b, dt_bias, a_log, d_skip, norm_w, row_block0, rows_in, n_chunks):
    n_seq = conv_prev.shape[0]

    def rows_spec(width, col_block):
        return pl.BlockSpec((rows_in, width), lambda b, c: (row_block0 + b * n_chunks + c, col_block))

    def const_spec(shape):
        return pl.BlockSpec(shape, lambda b, c: (0,) * len(shape))

    pad_lanes = lambda v: jnp.pad(v, (0, LANE - v.shape[0])).reshape(1, LANE)
    return pl.pallas_call(
        functools.partial(_ssd_kernel, rows_in=rows_in, n_chunks=n_chunks),
        grid=(n_seq, n_chunks),
        in_specs=[
            rows_spec(D_INNER, 0),
            rows_spec(D_INNER, 1),
            rows_spec(GN, 2 * D_INNER // GN),
            rows_spec(GN, 2 * D_INNER // GN + 1),
            rows_spec(LANE, (D_INNER + CONV_DIM) // LANE),
            pl.BlockSpec((None, CARRY_ROWS, CONV_DIM), lambda b, c: (b, 0, 0)),
            pl.BlockSpec((None, D_INNER, D_STATE), lambda b, c: (b, 0, 0)),
            const_spec((CONV_W, CONV_DIM)),
            const_spec((1, CONV_DIM)),
            const_spec((1, LANE)),
            const_spec((1, LANE)),
            const_spec((1, D_INNER)),
            const_spec((1, D_INNER)),
        ],
        out_specs=[
            pl.BlockSpec((rows_in, D_INNER), lambda b, c: (b * n_chunks + c, 0)),
            pl.BlockSpec((None, CONV_W - 1, CONV_DIM), lambda b, c: (b, 0, 0)),
            pl.BlockSpec((None, D_INNER, D_STATE), lambda b, c: (b, 0, 0)),
        ],
        out_shape=[
            jax.ShapeDtypeStruct((n_seq * n_chunks * rows_in, D_INNER), jnp.float32),
            jax.ShapeDtypeStruct((n_seq, CONV_W - 1, CONV_DIM), jnp.float32),
            jax.ShapeDtypeStruct((n_seq, D_INNER, D_STATE), jnp.float32),
        ],
        scratch_shapes=[
            pltpu.VMEM((CARRY_ROWS, CONV_DIM), jnp.float32),
            pltpu.VMEM((D_INNER, D_STATE), jnp.float32),
            pltpu.VMEM((SSD_CHUNK, CONV_DIM), jnp.float32),
        ],
        compiler_params=pltpu.CompilerParams(
            dimension_semantics=("parallel", "arbitrary"), vmem_limit_bytes=VMEM_LIMIT),
    )(proj, proj, proj, proj, proj, conv_prev, ssm_prev, conv_w, conv_b.reshape(1, CONV_DIM),
      pad_lanes(dt_bias), pad_lanes(a_log), jnp.repeat(d_skip, SSD_HEAD_DIM).reshape(1, D_INNER),
      norm_w.reshape(1, D_INNER))


def _masked_softmax(logits, mask):
    logits = jnp.where(mask, logits.astype(jnp.float32), NEG_INF)
    m = jnp.max(logits, axis=-1, keepdims=True)
    e = jnp.where(mask, jnp.exp(logits - m), 0.0)
    return e / jnp.maximum(jnp.sum(e, axis=-1, keepdims=True), 1e-30)


def _rel_bucket(dist):
    n = jnp.maximum(dist, 0)
    log_ratio = jnp.log(jnp.maximum(n, 1).astype(jnp.float32) / REL_MAX_EXACT) / math.log(REL_MAX_DIST / REL_MAX_EXACT)
    large = jnp.minimum(REL_MAX_EXACT + (log_ratio * (REL_BUCKETS - REL_MAX_EXACT)).astype(jnp.int32), REL_BUCKETS - 1)
    return jnp.where(n < REL_MAX_EXACT, n, large)


def _shared_bias(rel_bias, q_pos, k_pos):
    bias = rel_bias[_rel_bucket(q_pos[:, None] - k_pos[None, :])]
    nq, nk = bias.shape[:2]
    return bias.reshape(nq, nk, N_KV_HEADS, Q_PER_KV).transpose(0, 2, 3, 1).astype(jnp.float32)


def _gathered_bias(rel_bias, dist):
    table = rel_bias.T.reshape(N_KV_HEADS, Q_PER_KV, REL_BUCKETS)
    g_idx = jnp.arange(N_KV_HEADS).reshape(1, 1, -1, 1, 1, 1)
    r_idx = jnp.arange(Q_PER_KV).reshape(1, 1, 1, -1, 1, 1)
    return table[g_idx, r_idx, _rel_bucket(dist)].astype(jnp.float32)


def _gather_pages(pool, page_table, comp):
    rows = pool[page_table, :, comp]
    return rows.reshape(page_table.shape[0], page_table.shape[1] * pool.shape[1], N_KV_HEADS, HEAD_DIM)


def _ssd_scan(x, dt, a, bmat, cmat, init_state):
    b, t, nh, p = x.shape
    l = min(SSD_CHUNK, t)
    nc = -(-t // l)
    pad = nc * l - t

    def padt(arr):
        return jnp.pad(arr.astype(jnp.float32), [(0, 0), (0, pad)] + [(0, 0)] * (arr.ndim - 2))

    xr = padt(x).reshape(b, nc, l, SSD_GROUPS, HEADS_PER_GROUP, p)
    dtr = padt(dt).reshape(b, nc, l, SSD_GROUPS, HEADS_PER_GROUP)
    br = padt(bmat).reshape(b, nc, l, SSD_GROUPS, D_STATE)
    cr = padt(cmat).reshape(b, nc, l, SSD_GROUPS, D_STATE)
    acum = jnp.cumsum(dtr * a.reshape(SSD_GROUPS, HEADS_PER_GROUP), axis=2)
    xdt = xr * dtr[..., None]
    causal = jnp.tril(jnp.ones((l, l), dtype=bool))[None, None, :, :, None, None]
    decay = jnp.exp(jnp.where(causal, acum[:, :, :, None] - acum[:, :, None, :], -jnp.inf))
    cb = jnp.einsum('bclgn,bcsgn->bclsg', cr, br)
    y_diag = jnp.einsum('bclsg,bclsgr,bcsgrp->bclgrp', cb, decay, xdt)
    chunk_states = jnp.einsum('bclgn,bclgr,bclgrp->bcgrpn', br, jnp.exp(acum[:, :, -1:] - acum), xdt)
    chunk_decay = jnp.exp(acum[:, :, -1])

    def step(s, inp):
        st, dc = inp
        return s * dc[..., None, None] + st, s

    s0 = init_state.astype(jnp.float32).reshape(b, SSD_GROUPS, HEADS_PER_GROUP, p, D_STATE)
    final, prev = lax.scan(step, s0, (jnp.moveaxis(chunk_states, 1, 0), jnp.moveaxis(chunk_decay, 1, 0)))
    prev = jnp.moveaxis(prev, 0, 1)
    y_off = jnp.einsum('bclgn,bcgrpn,bclgr->bclgrp', cr, prev, jnp.exp(acum))
    y = (y_diag + y_off).reshape(b, nc * l, nh, p)[:, :t]
    return y.astype(x.dtype), final.reshape(b, nh, p, D_STATE).astype(init_state.dtype)


def _ssd_core(proj, conv_state, ssm_state, conv_w, conv_b, dt_bias, a_log, d_skip, norm_w):
    b, t, _ = proj.shape
    z = proj[..., :D_INNER]
    xbc = proj[..., D_INNER:D_INNER + CONV_DIM]
    dt_raw = proj[..., D_INNER + CONV_DIM:D_INNER + CONV_DIM + SSD_HEADS]
    xbc_ext = jnp.concatenate([conv_state.astype(xbc.dtype), xbc], axis=1)
    conv = conv_b + sum(xbc_ext[:, k:k + t] * conv_w[k] for k in range(CONV_W))
    xbc = jax.nn.silu(conv)
    x = xbc[..., :D_INNER].reshape(b, t, SSD_HEADS, SSD_HEAD_DIM)
    bmat = xbc[..., D_INNER:D_INNER + GN].reshape(b, t, SSD_GROUPS, D_STATE)
    cmat = xbc[..., D_INNER + GN:].reshape(b, t, SSD_GROUPS, D_STATE)
    dt = jax.nn.softplus((dt_raw + dt_bias).astype(jnp.float32))
    a = -jnp.exp(a_log.astype(jnp.float32))
    y, new_ssm = _ssd_scan(x, dt, a, bmat, cmat, ssm_state)
    y = (y + x * d_skip[:, None]).reshape(b, t, D_INNER) * jax.nn.silu(z)
    yg = y.reshape(b, t, SSD_GROUPS, D_INNER // SSD_GROUPS).astype(jnp.float32)
    yg = yg * lax.rsqrt(jnp.mean(yg * yg, axis=-1, keepdims=True) + RMS_EPS)
    y = yg.reshape(b, t, D_INNER) * norm_w
    return y, xbc_ext[:, xbc_ext.shape[1] - (CONV_W - 1):], new_ssm


def _nsa_split(proj):
    b, t, _ = proj.shape
    q = proj[..., :Q_DIM].reshape(b, t, N_KV_HEADS, Q_PER_KV, HEAD_DIM)
    kv = proj[..., Q_DIM:Q_DIM + 6 * KV_DIM].reshape(b, t, 6, N_KV_HEADS, HEAD_DIM)
    gates = jax.nn.sigmoid(proj[..., Q_DIM + 6 * KV_DIM:NSA_IN_DIM]).reshape(b, t, N_KV_HEADS, Q_PER_KV, 3)
    return q, kv, gates


def _nsa_compress(k, pe, w):
    b, t = k.shape[:2]
    n_str = t // CMP_STRIDE
    ch = k[:, :n_str * CMP_STRIDE].reshape(b, n_str, CMP_STRIDE, N_KV_HEADS, HEAD_DIM)
    blk = jnp.concatenate([ch[:, :-1], ch[:, 1:]], axis=2) + pe[:, None, :]
    flat = blk.transpose(0, 1, 3, 2, 4).reshape(b, n_str - 1, N_KV_HEADS, CMP_BLOCK * HEAD_DIM)
    return flat @ w


def _nsa_branch_keys(k_cmp, v_cmp, k_sel, v_sel, cmp_pe, cmp_w):
    b, t = k_cmp.shape[:2]
    kc = _nsa_compress(k_cmp, cmp_pe[0], cmp_w[0])
    vc = _nsa_compress(v_cmp, cmp_pe[1], cmp_w[1])
    c_end = jnp.arange(kc.shape[1], dtype=jnp.int32) * CMP_STRIDE + (CMP_BLOCK - 1)
    n_sel = -(-t // SEL_BLOCK)
    pad = ((0, 0), (0, n_sel * SEL_BLOCK - t), (0, 0), (0, 0))
    ks = jnp.pad(k_sel, pad).reshape(b, n_sel, SEL_BLOCK, N_KV_HEADS, HEAD_DIM)
    vs = jnp.pad(v_sel, pad).reshape(b, n_sel, SEL_BLOCK, N_KV_HEADS, HEAD_DIM)
    return kc, vc, c_end, ks, vs


def _nsa_core(q, q_pos, kc, vc, c_end, ks_blk, vs_blk, kw, vw, w_pos, gates, rel_bias):
    bsz, nq = q.shape[:2]
    scale = HEAD_DIM ** -0.5
    tq = q_pos[:, None]
    lc = jnp.einsum('bqgrd,bkgd->bqgrk', q, kc) * scale + _shared_bias(rel_bias, q_pos, c_end)
    pc = _masked_softmax(lc, (c_end[None, :] <= tq)[None, :, None, None, :])
    o_cmp = jnp.einsum('bqgrk,bkgd->bqgrd', pc.astype(vc.dtype), vc)
    nc, ns = kc.shape[1], ks_blk.shape[1]
    ci = jnp.arange(nc)[:, None]
    sj = jnp.arange(ns)[None, :]
    overlap = ((ci * CMP_STRIDE < (sj + 1) * SEL_BLOCK) & (ci * CMP_STRIDE + CMP_BLOCK > sj * SEL_BLOCK)).astype(jnp.float32)
    imp = jnp.einsum('bqgk,kj->bqgj', jnp.sum(pc, axis=3), overlap)
    own = q_pos // SEL_BLOCK
    j = jnp.arange(ns)
    diff = own[:, None] - j[None, :]
    valid = diff >= 0
    forced = (j[None, :] == 0) | (valid & (diff < N_LOCAL_SEL))
    score = jnp.where((valid & forced)[None, :, None, :], FORCE_SCORE, jnp.where(valid[None, :, None, :], imp, NEG_INF))
    k_top = min(SEL_TOPK, ns)
    _, idx = lax.top_k(score, k_top)
    sel_ok = idx <= own[None, :, None, None]
    bi = jnp.arange(bsz).reshape(-1, 1, 1, 1)
    gi = jnp.arange(N_KV_HEADS).reshape(1, 1, -1, 1)
    kg = ks_blk[bi, idx, :, gi]
    vg = vs_blk[bi, idx, :, gi]
    pos = idx[..., None] * SEL_BLOCK + jnp.arange(SEL_BLOCK)
    tq5 = q_pos[None, :, None, None, None]
    ms = sel_ok[..., None] & (pos <= tq5)
    ls = jnp.einsum('bqgrd,bqgkcd->bqgrkc', q, kg) * scale + _gathered_bias(rel_bias, (tq5 - pos)[:, :, :, None])
    ps = _masked_softmax(ls.reshape(bsz, nq, N_KV_HEADS, Q_PER_KV, k_top * SEL_BLOCK),
                         ms[:, :, :, None].reshape(bsz, nq, N_KV_HEADS, 1, k_top * SEL_BLOCK))
    ps = ps.reshape(bsz, nq, N_KV_HEADS, Q_PER_KV, k_top, SEL_BLOCK).astype(vg.dtype)
    o_sel = jnp.einsum('bqgrkc,bqgkcd->bqgrd', ps, vg)
    lw = jnp.einsum('bqgrd,bkgd->bqgrk', q, kw) * scale + _shared_bias(rel_bias, q_pos, w_pos)
    dw = tq - w_pos[None, :]
    mw = ((dw >= 0) & (dw < WINDOW) & (w_pos[None, :] >= 0))[None, :, None, None, :]
    o_win = jnp.einsum('bqgrk,bkgd->bqgrd', _masked_softmax(lw, mw).astype(vw.dtype), vw)
    return gates[..., 0:1] * o_cmp + gates[..., 1:2] * o_sel + gates[..., 2:3] * o_win


def _nsa_prompt_core(proj, cmp_pe, cmp_w, rel_bias):
    b, t, _ = proj.shape
    q, kv, gates = _nsa_split(proj)
    kc, vc, c_end, ks, vs = _nsa_branch_keys(kv[:, :, 0], kv[:, :, 1], kv[:, :, 2], kv[:, :, 3], cmp_pe, cmp_w)
    wpad = ((0, 0), (WINDOW, 0), (0, 0), (0, 0))
    kw_pad, vw_pad = jnp.pad(kv[:, :, 4], wpad), jnp.pad(kv[:, :, 5], wpad)

    def block(i):
        s0 = i * Q_BLOCK
        q_pos = s0 + jnp.arange(Q_BLOCK, dtype=jnp.int32)
        w_pos = s0 - WINDOW + jnp.arange(WINDOW + Q_BLOCK, dtype=jnp.int32)
        qb = lax.dynamic_slice_in_dim(q, s0, Q_BLOCK, axis=1)
        gb = lax.dynamic_slice_in_dim(gates, s0, Q_BLOCK, axis=1)
        kw = lax.dynamic_slice_in_dim(kw_pad, s0, WINDOW + Q_BLOCK, axis=1)
        vw = lax.dynamic_slice_in_dim(vw_pad, s0, WINDOW + Q_BLOCK, axis=1)
        return _nsa_core(qb, q_pos, kc, vc, c_end, ks, vs, kw, vw, w_pos, gb, rel_bias)

    o = lax.map(block, jnp.arange(t // Q_BLOCK, dtype=jnp.int32))
    o = jnp.moveaxis(o, 0, 1).reshape(b, t, Q_DIM)
    wb = min(WINDOW, t)
    return o, kv[:, :, :4], kv[:, t - wb:, 4:]


def _nsa_sample_core(proj, cache_kv, win_buf, page_table, cmp_pe, cmp_w, rel_bias):
    b, t, _ = proj.shape
    q, kv, gates = _nsa_split(proj)
    full = [jnp.concatenate([_gather_pages(cache_kv, page_table, c).astype(kv.dtype), kv[:, :, c]], axis=1) for c in range(4)]
    kc, vc, c_end, ks, vs = _nsa_branch_keys(full[0], full[1], full[2], full[3], cmp_pe, cmp_w)
    wb = win_buf.shape[1]
    win = jnp.concatenate([win_buf.astype(kv.dtype), kv[:, :, 4:]], axis=1)
    w_pos = PAST_LEN - wb + jnp.arange(wb + t, dtype=jnp.int32)
    q_pos = PAST_LEN + jnp.arange(t, dtype=jnp.int32)
    o = _nsa_core(q, q_pos, kc, vc, c_end, ks, vs, win[:, :, 0], win[:, :, 1], w_pos, gates, rel_bias)
    return o.reshape(b, t, Q_DIM), kv[:, :, :4], win[:, t:]


def _moba_split(proj):
    b, t, _ = proj.shape
    q = proj[..., :Q_DIM].reshape(b, t, N_KV_HEADS, Q_PER_KV, HEAD_DIM)
    kv = proj[..., Q_DIM:].reshape(b, t, 2, N_KV_HEADS, HEAD_DIM)
    return q, kv


def _moba_blocks(k, v):
    b, t = k.shape[:2]
    nf = t // MOBA_BLOCK
    kb = k[:, :nf * MOBA_BLOCK].reshape(b, nf, MOBA_BLOCK, N_KV_HEADS, HEAD_DIM)
    vb = v[:, :nf * MOBA_BLOCK].reshape(b, nf, MOBA_BLOCK, N_KV_HEADS, HEAD_DIM)
    return kb, vb, jnp.mean(kb.astype(jnp.float32), axis=2)


def _moba_core(q, q_pos, kb, vb, kmean, own_k, own_v, own_pos, rel_bias):
    bsz, nq = q.shape[:2]
    scale = HEAD_DIM ** -0.5
    q_blk = q_pos // MOBA_BLOCK
    lo = jnp.einsum('bqgrd,bkgd->bqgrk', q, own_k) * scale + _shared_bias(rel_bias, q_pos, own_pos)
    mo = (own_pos[None, :] <= q_pos[:, None]) & ((own_pos // MOBA_BLOCK)[None, :] == q_blk[:, None])
    mo = jnp.broadcast_to(mo[None, :, None, None, :], lo.shape)
    nf = kmean.shape[1]
    k_top = min(MOBA_TOPK, nf)
    gate = jnp.einsum('bqgrd,bngd->bqgrn', q.astype(jnp.float32), kmean)
    past_ok = (jnp.arange(nf)[None, :] < q_blk[:, None])[None, :, None, None, :]
    _, idx = lax.top_k(jnp.where(past_ok, gate, NEG_INF), k_top)
    sel_ok = idx < q_blk[None, :, None, None, None]
    bi = jnp.arange(bsz).reshape(-1, 1, 1, 1, 1)
    gi = jnp.arange(N_KV_HEADS).reshape(1, 1, -1, 1, 1)
    kg = kb[bi, idx, :, gi]
    vg = vb[bi, idx, :, gi]
    pos = idx[..., None] * MOBA_BLOCK + jnp.arange(MOBA_BLOCK)
    lp = jnp.einsum('bqgrd,bqgrkcd->bqgrkc', q, kg) * scale + _gathered_bias(rel_bias, q_pos[None, :, None, None, None, None] - pos)
    mp = jnp.broadcast_to(sel_ok[..., None], lp.shape)
    n_past = k_top * MOBA_BLOCK
    logits = jnp.concatenate([lp.reshape(bsz, nq, N_KV_HEADS, Q_PER_KV, n_past), lo], axis=-1)
    mask = jnp.concatenate([mp.reshape(bsz, nq, N_KV_HEADS, Q_PER_KV, n_past), mo], axis=-1)
    p = _masked_softmax(logits, mask).astype(vg.dtype)
    pp = p[..., :n_past].reshape(bsz, nq, N_KV_HEADS, Q_PER_KV, k_top, MOBA_BLOCK)
    return jnp.einsum('bqgrkc,bqgrkcd->bqgrd', pp, vg) + jnp.einsum('bqgrk,bkgd->bqgrd', p[..., n_past:], own_v)


def _moba_prompt_core(proj, rel_bias):
    b, t, _ = proj.shape
    q, kv = _moba_split(proj)
    k, v = kv[:, :, 0], kv[:, :, 1]
    kb, vb, kmean = _moba_blocks(k, v)

    def block(i):
        s0 = i * Q_BLOCK
        q_pos = s0 + jnp.arange(Q_BLOCK, dtype=jnp.int32)
        own0 = (s0 // MOBA_BLOCK) * MOBA_BLOCK
        own_pos = own0 + jnp.arange(MOBA_BLOCK, dtype=jnp.int32)
        qb = lax.dynamic_slice_in_dim(q, s0, Q_BLOCK, axis=1)
        ok = lax.dynamic_slice_in_dim(k, own0, MOBA_BLOCK, axis=1)
        ov = lax.dynamic_slice_in_dim(v, own0, MOBA_BLOCK, axis=1)
        return _moba_core(qb, q_pos, kb, vb, kmean, ok, ov, own_pos, rel_bias)

    o = lax.map(block, jnp.arange(t // Q_BLOCK, dtype=jnp.int32))
    o = jnp.moveaxis(o, 0, 1).reshape(b, t, Q_DIM)
    return o, kv


def _moba_sample_core(proj, cache_kv, page_table, rel_bias):
    b, t, _ = proj.shape
    q, kv = _moba_split(proj)
    k = jnp.concatenate([_gather_pages(cache_kv, page_table, 0).astype(kv.dtype), kv[:, :, 0]], axis=1)
    v = jnp.concatenate([_gather_pages(cache_kv, page_table, 1).astype(kv.dtype), kv[:, :, 1]], axis=1)
    kb, vb, kmean = _moba_blocks(k, v)
    own0 = (PAST_LEN // MOBA_BLOCK) * MOBA_BLOCK
    own_pos = own0 + jnp.arange(k.shape[1] - own0, dtype=jnp.int32)
    q_pos = PAST_LEN + jnp.arange(t, dtype=jnp.int32)
    o = _moba_core(q, q_pos, kb, vb, kmean, k[:, own0:], v[:, own0:], own_pos, rel_bias)
    return o.reshape(b, t, Q_DIM), kv


def _pad_cols(w, n):
    return jnp.pad(w, ((0, 0), (0, n - w.shape[1])))


def _split_rows(a):
    return (a[:M_PROMPT].reshape(BATCH, SEQ, a.shape[-1]),
            a[M_PROMPT:].reshape(DEC_BATCH, DEC_SEQ, a.shape[-1]))


def _join_rows(p, s):
    return jnp.concatenate([p.reshape(M_PROMPT, p.shape[-1]), s.reshape(M_SAMPLE, s.shape[-1])], axis=0)


def kernel(x_prompt, x_sample, state_ssm, state_conv, cache_nsa_kv, cache_nsa_win, cache_moba_kv, page_table, rel_bias, norm_w, final_norm_w, ffn_w_gate, ffn_w_up, ffn_w_down, ssd_w_in, ssd_conv_w, ssd_conv_b, ssd_dt_bias, ssd_a_log, ssd_d, ssd_norm_w, ssd_w_out, nsa_w_in, nsa_cmp_pe, nsa_cmp_w, nsa_w_out, moba_w_in, moba_w_out):
    bf = jnp.bfloat16
    h = _join_rows(x_prompt, x_sample)
    ssm_p, ssm_s, conv_p, conv_s = [], [], [], []
    nsa_kv_p, nsa_kv_s, nsa_win_p, nsa_win_s = [], [], [], []
    moba_kv_p, moba_kv_s = [], []
    for i in range(DEPTH):
        kind, j = i % N_MIXERS, i // N_MIXERS
        h = _ffn_half(h, norm_w[i, 0], ffn_w_gate[i, 0].astype(bf), ffn_w_up[i, 0].astype(bf), ffn_w_down[i, 0].astype(bf))
        if kind == 0:
            n_pad = -(-SSD_IN_DIM // 512) * 512
            proj = _norm_proj(h, norm_w[i, 1], _pad_cols(ssd_w_in[j], n_pad).astype(bf))
            w = (ssd_conv_w[j], ssd_conv_b[j], ssd_dt_bias[j], ssd_a_log[j], ssd_d[j], ssd_norm_w[j])
            conv0 = jnp.zeros((BATCH, CARRY_ROWS, CONV_DIM), jnp.float32)
            ssm0 = jnp.zeros((BATCH, D_INNER, D_STATE), jnp.float32)
            y_p, c_p, s_p = _ssd_mix(proj, conv0, ssm0, *w, 0, SSD_CHUNK, SEQ // SSD_CHUNK)
            conv1 = jnp.pad(state_conv[j], ((0, 0), (CARRY_ROWS - (CONV_W - 1), 0), (0, 0)))
            ssm1 = state_ssm[j].reshape(DEC_BATCH, D_INNER, D_STATE)
            y_s, c_s, s_s = _ssd_mix(proj, conv1, ssm1, *w, M_PROMPT // DEC_SEQ, DEC_SEQ, 1)
            conv_p.append(c_p)
            conv_s.append(c_s)
            ssm_p.append(s_p.reshape(BATCH, SSD_HEADS, SSD_HEAD_DIM, D_STATE))
            ssm_s.append(s_s.reshape(DEC_BATCH, SSD_HEADS, SSD_HEAD_DIM, D_STATE))
            w_out = ssd_w_out[j]
        elif kind == 1:
            n_pad = -(-NSA_IN_DIM // 512) * 512
            proj = _norm_proj(h, norm_w[i, 1], _pad_cols(nsa_w_in[j], n_pad).astype(bf))
            proj_p, proj_s = _split_rows(proj)
            kc, vc = _nsa_compress_prompt(proj, nsa_cmp_pe[j], nsa_cmp_w[j].astype(bf), BATCH, SEQ)
            y_p = _nsa_attention(proj, kc, vc, rel_bias, BATCH, SEQ)
            kv_all = proj_p[..., Q_DIM:Q_DIM + 6 * KV_DIM].reshape(BATCH, SEQ, 6, N_KV_HEADS, HEAD_DIM)
            kv_p, win_p = kv_all[:, :, :4], kv_all[:, SEQ - WINDOW:, 4:]
            cmp_w_bf = nsa_cmp_w[j].astype(bf)
            y_s = _nsa_sample(proj, cache_nsa_kv[j], cache_nsa_win[j], page_table, nsa_cmp_pe[j], cmp_w_bf,
                              rel_bias, M_PROMPT // DEC_SEQ, PAST_LEN)
            kv_new = proj_s[..., Q_DIM:Q_DIM + 6 * KV_DIM].reshape(DEC_BATCH, DEC_SEQ, 6, N_KV_HEADS, HEAD_DIM)
            kv_s = kv_new[:, :, :4]
            win_s = jnp.concatenate([cache_nsa_win[j][:, DEC_SEQ:], kv_new[:, :, 4:]], axis=1)
            nsa_kv_p.append(kv_p)
            nsa_kv_s.append(kv_s)
            nsa_win_p.append(win_p)
            nsa_win_s.append(win_s)
            w_out = nsa_w_out[j]
        else:
            proj = _norm_proj(h, norm_w[i, 1], moba_w_in[j].astype(bf))
            proj_p, proj_s = _split_rows(proj)
            y_p = _moba_attention(proj, rel_bias, BATCH, SEQ)
            kv_p = proj_p[..., Q_DIM:].reshape(BATCH, SEQ, 2, N_KV_HEADS, HEAD_DIM)
            kmean = _moba_sample_kmean(cache_moba_kv[j], page_table)
            y_s = _moba_sample_attention(proj, jnp.swapaxes(kmean, 1, 2), cache_moba_kv[j], page_table, rel_bias,
                                         M_PROMPT // DEC_SEQ, PAST_LEN)
            kv_s = proj_s[..., Q_DIM:].reshape(DEC_BATCH, DEC_SEQ, 2, N_KV_HEADS, HEAD_DIM)
            moba_kv_p.append(kv_p)
            moba_kv_s.append(kv_s)
            w_out = moba_w_out[j]
        h = _proj_residual(_join_rows(y_p, y_s), w_out.astype(bf), h)
        h = _ffn_half(h, norm_w[i, 2], ffn_w_gate[i, 1].astype(bf), ffn_w_up[i, 1].astype(bf), ffn_w_down[i, 1].astype(bf))
    y = _final_norm(h, final_norm_w)
    y_prompt, y_sample = _split_rows(y)
    return (y_prompt, y_sample,
            jnp.stack(ssm_p), jnp.stack(ssm_s), jnp.stack(conv_p), jnp.stack(conv_s),
            jnp.stack(nsa_kv_p), jnp.stack(nsa_kv_s), jnp.stack(nsa_win_p), jnp.stack(nsa_win_s),
            jnp.stack(moba_kv_p), jnp.stack(moba_kv_s))
```

```python
import functools
import math

import jax
import jax.numpy as jnp
import numpy as np
from jax import lax
from jax.experimental import pallas as pl
from jax.experimental.pallas import tpu as pltpu

D_MODEL = 2048
BATCH = 2
SEQ = 4096
DEPTH = 4
DEC_BATCH = 8
DEC_SEQ = 8
PAST_LEN = 16384
PAGE_SIZE = 128
N_MIXERS = 3
D_FF = 5632
RMS_EPS = 1e-6
D_INNER = 2 * D_MODEL
SSD_HEAD_DIM = 64
SSD_HEADS = D_INNER // SSD_HEAD_DIM
D_STATE = 128
SSD_GROUPS = 8
HEADS_PER_GROUP = SSD_HEADS // SSD_GROUPS
GN = SSD_GROUPS * D_STATE
CONV_W = 4
CONV_DIM = D_INNER + 2 * GN
SSD_IN_DIM = D_INNER + CONV_DIM + SSD_HEADS
SSD_CHUNK = 128
HEAD_DIM = 128
N_HEADS = D_MODEL // HEAD_DIM
N_KV_HEADS = 4
Q_PER_KV = N_HEADS // N_KV_HEADS
Q_DIM = N_HEADS * HEAD_DIM
KV_DIM = N_KV_HEADS * HEAD_DIM
Q_BLOCK = 64
CMP_BLOCK = 32
CMP_STRIDE = 16
SEL_BLOCK = 64
SEL_TOPK = 16
N_LOCAL_SEL = 2
WINDOW = 512
NSA_IN_DIM = Q_DIM + 6 * KV_DIM + 3 * N_HEADS
MOBA_BLOCK = 256
MOBA_TOPK = 3
MOBA_IN_DIM = Q_DIM + 2 * KV_DIM
REL_BUCKETS = 32
REL_MAX_EXACT = REL_BUCKETS // 2
REL_MAX_DIST = 128
NEG_INF = -1e30
FORCE_SCORE = 1e9

M_PROMPT = BATCH * SEQ
M_SAMPLE = DEC_BATCH * DEC_SEQ
M_TOTAL = M_PROMPT + M_SAMPLE
ROW_TILE = 688
LANE = 128
VMEM_LIMIT = 56 * 1024 * 1024


def _rms_rows(x, g):
    ms = jnp.mean(x * x, axis=-1, keepdims=True)
    return x * lax.rsqrt(ms + RMS_EPS) * g


def _ffn_kernel(h_ref, g_ref, wg_ref, wu_ref, wd_ref, o_ref, xn_ref, acc_ref):
    f = pl.program_id(1)

    @pl.when(f == 0)
    def _():
        xn_ref[...] = _rms_rows(h_ref[...], g_ref[...]).astype(jnp.bfloat16)
        acc_ref[...] = jnp.zeros_like(acc_ref)

    xn = xn_ref[...]
    gate = jnp.dot(xn, wg_ref[...], preferred_element_type=jnp.float32)
    up = jnp.dot(xn, wu_ref[...], preferred_element_type=jnp.float32)
    act = (gate * jax.nn.sigmoid(gate) * up).astype(jnp.bfloat16)
    acc_ref[...] += jnp.dot(act, wd_ref[...], preferred_element_type=jnp.float32)

    @pl.when(f == pl.num_programs(1) - 1)
    def _():
        o_ref[...] = h_ref[...] + 0.5 * acc_ref[...]


def _ffn_half(h, g, wg, wu, wd, *, ff_tile=512):
    m, d = h.shape
    dff = wg.shape[1]
    grid = (m // ROW_TILE, dff // ff_tile)
    return pl.pallas_call(
        _ffn_kernel,
        grid=grid,
        in_specs=[
            pl.BlockSpec((ROW_TILE, d), lambda i, f: (i, 0)),
            pl.BlockSpec((1, d), lambda i, f: (0, 0)),
            pl.BlockSpec((d, ff_tile), lambda i, f: (0, f)),
            pl.BlockSpec((d, ff_tile), lambda i, f: (0, f)),
            pl.BlockSpec((ff_tile, d), lambda i, f: (f, 0)),
        ],
        out_specs=pl.BlockSpec((ROW_TILE, d), lambda i, f: (i, 0)),
        out_shape=jax.ShapeDtypeStruct((m, d), jnp.float32),
        scratch_shapes=[
            pltpu.VMEM((ROW_TILE, d), jnp.bfloat16),
            pltpu.VMEM((ROW_TILE, d), jnp.float32),
        ],
        compiler_params=pltpu.CompilerParams(
            dimension_semantics=("parallel", "arbitrary"),
            vmem_limit_bytes=VMEM_LIMIT),
    )(h, g.reshape(1, d), wg, wu, wd)


def _norm_proj_kernel(h_ref, g_ref, w_ref, o_ref, xn_ref):
    @pl.when(pl.program_id(1) == 0)
    def _():
        xn_ref[...] = _rms_rows(h_ref[...], g_ref[...]).astype(jnp.bfloat16)

    o_ref[...] = jnp.dot(xn_ref[...], w_ref[...], preferred_element_type=jnp.float32)


PROJ_TILE_MAX = 1536


def _norm_proj(h, g, w):
    m, d = h.shape
    n = w.shape[1]
    n_tile = max(t for t in range(LANE, PROJ_TILE_MAX + 1, LANE) if n % t == 0)
    grid = (m // ROW_TILE, n // n_tile)
    return pl.pallas_call(
        _norm_proj_kernel,
        grid=grid,
        in_specs=[
            pl.BlockSpec((ROW_TILE, d), lambda i, j: (i, 0)),
            pl.BlockSpec((1, d), lambda i, j: (0, 0)),
            pl.BlockSpec((d, n_tile), lambda i, j: (0, j)),
        ],
        out_specs=pl.BlockSpec((ROW_TILE, n_tile), lambda i, j: (i, j)),
        out_shape=jax.ShapeDtypeStruct((m, n), jnp.float32),
        scratch_shapes=[pltpu.VMEM((ROW_TILE, d), jnp.bfloat16)],
        compiler_params=pltpu.CompilerParams(
            dimension_semantics=("parallel", "arbitrary"),
            vmem_limit_bytes=VMEM_LIMIT),
    )(h, g.reshape(1, d), w)


def _proj_residual_kernel(y_ref, w_ref, h_ref, o_ref):
    o_ref[...] = h_ref[...] + jnp.dot(
        y_ref[...].astype(jnp.bfloat16), w_ref[...], preferred_element_type=jnp.float32)


def _proj_residual(y, w, h, *, n_tile=512):
    m, k = y.shape
    d = w.shape[1]
    grid = (m // ROW_TILE, d // n_tile)
    return pl.pallas_call(
        _proj_residual_kernel,
        grid=grid,
        in_specs=[
            pl.BlockSpec((ROW_TILE, k), lambda i, j: (i, 0)),
            pl.BlockSpec((k, n_tile), lambda i, j: (0, j)),
            pl.BlockSpec((ROW_TILE, n_tile), lambda i, j: (i, j)),
        ],
        out_specs=pl.BlockSpec((ROW_TILE, n_tile), lambda i, j: (i, j)),
        out_shape=jax.ShapeDtypeStruct((m, d), jnp.float32),
        compiler_params=pltpu.CompilerParams(
            dimension_semantics=("parallel", "arbitrary"),
            vmem_limit_bytes=VMEM_LIMIT),
    )(y, w, h)


def _final_norm_kernel(h_ref, g_ref, o_ref):
    o_ref[...] = _rms_rows(h_ref[...], g_ref[...])


def _final_norm(h, g):
    m, d = h.shape
    return pl.pallas_call(
        _final_norm_kernel,
        grid=(m // ROW_TILE,),
        in_specs=[pl.BlockSpec((ROW_TILE, d), lambda i: (i, 0)),
                  pl.BlockSpec((1, d), lambda i: (0, 0))],
        out_specs=pl.BlockSpec((ROW_TILE, d), lambda i: (i, 0)),
        out_shape=jax.ShapeDtypeStruct((m, d), jnp.float32),
        compiler_params=pltpu.CompilerParams(dimension_semantics=("parallel",)),
    )(h, g.reshape(1, d))


ATT_TILE = 256
HEAD_ROWS = Q_PER_KV * ATT_TILE
SCALE = HEAD_DIM ** -0.5
CMP_BAND_START = LANE - ATT_TILE // CMP_STRIDE
_NT = (((1,), (1,)), ((), ()))


def _bucket_upper_bounds():
    n = np.arange(0, 2 * REL_MAX_DIST)
    ratio = np.log(np.maximum(n, 1).astype(np.float32) / np.float32(REL_MAX_EXACT)) / np.float32(
        math.log(REL_MAX_DIST / REL_MAX_EXACT))
    large = np.minimum(REL_MAX_EXACT + (ratio * np.float32(REL_BUCKETS - REL_MAX_EXACT)).astype(np.int32),
                       REL_BUCKETS - 1)
    bucket = np.where(n < REL_MAX_EXACT, n, large)
    return [int(n[bucket > b].min()) for b in range(REL_BUCKETS - 1)]


_BUCKET_HI = _bucket_upper_bounds()


def _bias_from_dist(dist, head_vals):
    accs = [jnp.zeros(dist.shape, jnp.float32) + vals[REL_BUCKETS - 1] for vals in head_vals]
    for b in range(REL_BUCKETS - 2, -1, -1):
        near = dist < _BUCKET_HI[b]
        accs = [jnp.where(near, vals[b], acc) for vals, acc in zip(head_vals, accs)]
    return accs


def _head_vals(rb_ref, g):
    return [[rb_ref[b, g * Q_PER_KV + r] for b in range(REL_BUCKETS)] for r in range(Q_PER_KV)]


def _build_bias_tiles_t(rb_ref, g, bias_ref):
    key = lax.broadcasted_iota(jnp.int32, (ATT_TILE, ATT_TILE), 0)
    qry = lax.broadcasted_iota(jnp.int32, (ATT_TILE, ATT_TILE), 1)
    vals = _head_vals(rb_ref, g)
    for kind in range(2):
        tiles = _bias_from_dist(qry - key + kind * ATT_TILE, vals)
        for r in range(Q_PER_KV):
            bias_ref[kind, :, r * ATT_TILE:(r + 1) * ATT_TILE] = tiles[r]
    for r in range(Q_PER_KV):
        bias_ref[2, :, r * ATT_TILE:(r + 1) * ATT_TILE] = (
            jnp.zeros((ATT_TILE, ATT_TILE), jnp.float32) + vals[r][REL_BUCKETS - 1])


def _flash_t(qs, k_tile, v_tile_t, bias, mask, state):
    s = lax.dot_general(k_tile, qs, _NT, preferred_element_type=jnp.float32) * SCALE + bias
    if mask is not None:
        s = jnp.where(mask, s, NEG_INF)
    tile_max = jnp.max(s, axis=0, keepdims=True)
    if state is None:
        m_new = tile_max
        p = jnp.exp(s - m_new)
        l = jnp.sum(p, axis=0, keepdims=True)
        acc = jnp.dot(v_tile_t, p.astype(jnp.bfloat16), preferred_element_type=jnp.float32)
    else:
        m, l, acc = state
        m_new = jnp.maximum(m, tile_max)
        alpha = jnp.exp(m - m_new)
        p = jnp.exp(s - m_new)
        l = alpha * l + jnp.sum(p, axis=0, keepdims=True)
        acc = alpha * acc + jnp.dot(v_tile_t, p.astype(jnp.bfloat16), preferred_element_type=jnp.float32)
    return m_new, l, acc


def _kv_tile_t(ref, j):
    return ref[pl.ds(pl.multiple_of(j * ATT_TILE, ATT_TILE), ATT_TILE), :].T.astype(jnp.bfloat16)


def _store_heads_t(o_ref, out_t):
    for r in range(Q_PER_KV):
        o_ref[:, r * HEAD_DIM:(r + 1) * HEAD_DIM] = out_t[:, r * ATT_TILE:(r + 1) * ATT_TILE].T


def _stack_heads(q_ref):
    return jnp.concatenate(
        [q_ref[:, r * HEAD_DIM:(r + 1) * HEAD_DIM] for r in range(Q_PER_KV)], axis=0).astype(jnp.bfloat16)


def _tile4(x):
    return jnp.concatenate([x] * Q_PER_KV, axis=0)


def _flash_first(qs, k_tile, v_tile, bias, mask):
    s = lax.dot_general(qs, k_tile, _NT, preferred_element_type=jnp.float32) * SCALE + bias
    s = jnp.where(mask, s, NEG_INF)
    m = jnp.max(s, axis=-1, keepdims=True)
    p = jnp.exp(s - m)
    l = jnp.sum(p, axis=-1, keepdims=True)
    acc = jnp.dot(p.astype(jnp.bfloat16), v_tile, preferred_element_type=jnp.float32)
    return m, l, acc


def _flash_next(qs, k_tile, v_tile, bias, mask, m, l, acc):
    s = lax.dot_general(qs, k_tile, _NT, preferred_element_type=jnp.float32) * SCALE + bias
    s = jnp.where(mask, s, NEG_INF)
    m_new = jnp.maximum(m, jnp.max(s, axis=-1, keepdims=True))
    alpha = jnp.exp(m - m_new)
    p = jnp.exp(s - m_new)
    l = alpha * l + jnp.sum(p, axis=-1, keepdims=True)
    acc = alpha * acc + jnp.dot(p.astype(jnp.bfloat16), v_tile, preferred_element_type=jnp.float32)
    return m_new, l, acc


def _kv_tile(ref, j):
    return ref[pl.ds(pl.multiple_of(j * ATT_TILE, ATT_TILE), ATT_TILE), :].astype(jnp.bfloat16)


def _rank_select(score, blk, top_k):
    cnt = jnp.zeros(score.shape, jnp.float32)
    for m in range(score.shape[0]):
        row = score[m:m + 1, :]
        beats = jnp.where(row > score, 1.0, jnp.where((row == score) & (blk > m), 1.0, 0.0))
        cnt = cnt + beats
    return cnt < top_k


def _moba_kernel(rb_ref, q_ref, k_ref, v_ref, o_ref, bias_ref, kmean_ref, sel_ref, *, n_tiles):
    g = pl.program_id(1)
    t = pl.program_id(2)

    @pl.when(t == 0)
    def _():
        _build_bias_tiles_t(rb_ref, g, bias_ref)
        kmean_ref[...] = jnp.zeros_like(kmean_ref)
        for n in range(n_tiles):
            kmean_ref[n:n + 1, :] = jnp.mean(k_ref[n * ATT_TILE:(n + 1) * ATT_TILE, :], axis=0, keepdims=True)

    qs = _stack_heads(q_ref)
    gate = lax.dot_general(kmean_ref[...].astype(jnp.bfloat16), qs, _NT, preferred_element_type=jnp.float32)
    blk = lax.broadcasted_iota(jnp.int32, gate.shape, 0)
    past = blk < t
    gate = jnp.where(past, gate, NEG_INF)
    chosen = _rank_select(gate, blk, MOBA_TOPK) & past
    sel_ref[...] = jnp.where(chosen, 1.0, 0.0)

    key = lax.broadcasted_iota(jnp.int32, (ATT_TILE, HEAD_ROWS), 0)
    qry = lax.broadcasted_iota(jnp.int32, (ATT_TILE, HEAD_ROWS), 1) & (ATT_TILE - 1)
    state = _flash_t(qs, _kv_tile(k_ref, t), _kv_tile_t(v_ref, t), bias_ref[0], key <= qry, None)

    def picked(j):
        return sel_ref[pl.ds(j, 1), :] > 0.5

    def far_body(j, state):
        bias = jnp.where(picked(j), bias_ref[2, 0:1, :], NEG_INF)
        return _flash_t(qs, _kv_tile(k_ref, j), _kv_tile_t(v_ref, j), bias, None, state)

    def prev_body(j, state):
        return _flash_t(qs, _kv_tile(k_ref, j), _kv_tile_t(v_ref, j), bias_ref[1], picked(j), state)

    prev = jnp.maximum(t - 1, 0)
    state = lax.fori_loop(0, prev, far_body, state)
    m, l, acc = lax.fori_loop(prev, t, prev_body, state)
    _store_heads_t(o_ref, acc / jnp.maximum(l, 1e-30))


def _moba_attention(proj, rel_bias, n_batch, seq):
    n_tiles = seq // ATT_TILE
    k_col = Q_DIM // HEAD_DIM
    v_col = (Q_DIM + KV_DIM) // HEAD_DIM
    q_cols = Q_PER_KV * HEAD_DIM
    return pl.pallas_call(
        functools.partial(_moba_kernel, n_tiles=n_tiles),
        grid=(n_batch, N_KV_HEADS, n_tiles),
        in_specs=[
            pl.BlockSpec(memory_space=pltpu.SMEM),
            pl.BlockSpec((ATT_TILE, q_cols), lambda b, g, t: (b * n_tiles + t, g)),
            pl.BlockSpec((seq, HEAD_DIM), lambda b, g, t: (b, k_col + g)),
            pl.BlockSpec((seq, HEAD_DIM), lambda b, g, t: (b, v_col + g)),
        ],
        out_specs=pl.BlockSpec((ATT_TILE, q_cols), lambda b, g, t: (b * n_tiles + t, g)),
        out_shape=jax.ShapeDtypeStruct((n_batch * seq, Q_DIM), jnp.float32),
        scratch_shapes=[
            pltpu.VMEM((3, ATT_TILE, HEAD_ROWS), jnp.float32),
            pltpu.VMEM((max(n_tiles, 16), HEAD_DIM), jnp.float32),
            pltpu.VMEM((max(n_tiles, 16), HEAD_ROWS), jnp.float32),
        ],
        compiler_params=pltpu.CompilerParams(
            dimension_semantics=("parallel", "parallel", "arbitrary"),
            vmem_limit_bytes=VMEM_LIMIT),
    )(rel_bias, proj, proj, proj)


def _nsa_compress_kernel(k_ref, v_ref, pe_ref, w_ref, kc_ref, vc_ref, *, n_cmp):
    for c, (x_ref, o_ref) in enumerate(((k_ref, kc_ref), (v_ref, vc_ref))):
        top = jnp.zeros((n_cmp, HEAD_DIM), jnp.float32)
        bot = jnp.zeros((n_cmp, HEAD_DIM), jnp.float32)
        for r in range(CMP_STRIDE):
            x = x_ref[pl.ds(r, n_cmp, stride=CMP_STRIDE), :]
            xt = (x + pe_ref[c, r:r + 1, :]).astype(jnp.bfloat16)
            xb = (x + pe_ref[c, CMP_STRIDE + r:CMP_STRIDE + r + 1, :]).astype(jnp.bfloat16)
            top = top + jnp.dot(xt, w_ref[c, r * HEAD_DIM:(r + 1) * HEAD_DIM, :], preferred_element_type=jnp.float32)
            bot = bot + jnp.dot(xb, w_ref[c, (CMP_STRIDE + r) * HEAD_DIM:(CMP_STRIDE + r + 1) * HEAD_DIM, :],
                                preferred_element_type=jnp.float32)
        o_ref[0, 0] = top + pltpu.roll(bot, n_cmp - 1, axis=0)


def _nsa_compress_prompt(proj, cmp_pe, cmp_w, n_batch, seq):
    n_cmp = seq // CMP_STRIDE
    k_col = Q_DIM // HEAD_DIM
    v_col = (Q_DIM + KV_DIM) // HEAD_DIM
    out = jax.ShapeDtypeStruct((n_batch, N_KV_HEADS, n_cmp, HEAD_DIM), jnp.float32)
    return pl.pallas_call(
        functools.partial(_nsa_compress_kernel, n_cmp=n_cmp),
        grid=(n_batch, N_KV_HEADS),
        in_specs=[
            pl.BlockSpec((seq, HEAD_DIM), lambda b, g: (b, k_col + g)),
            pl.BlockSpec((seq, HEAD_DIM), lambda b, g: (b, v_col + g)),
            pl.BlockSpec((2, CMP_BLOCK, HEAD_DIM), lambda b, g: (0, 0, 0)),
            pl.BlockSpec((2, CMP_BLOCK * HEAD_DIM, HEAD_DIM), lambda b, g: (0, 0, 0)),
        ],
        out_specs=[pl.BlockSpec((1, 1, n_cmp, HEAD_DIM), lambda b, g: (b, g, 0, 0))] * 2,
        out_shape=[out, out],
        compiler_params=pltpu.CompilerParams(
            dimension_semantics=("parallel", "parallel"), vmem_limit_bytes=VMEM_LIMIT),
    )(proj, proj, cmp_pe, cmp_w)


def _nsa_kernel_t(rb_ref, q_ref, gate_ref, kc_ref, vc_ref, ks_ref, vs_ref, kw_ref, vw_ref, ovt_ref, o_ref,
                  bias_ref, cbias_ref, *, n_cmp, n_sel):
    g = pl.program_id(1)
    t = pl.program_id(2)
    tile_tokens = ATT_TILE // CMP_STRIDE
    blocks_per_tile = ATT_TILE // SEL_BLOCK

    @pl.when(t == 0)
    def _():
        _build_bias_tiles_t(rb_ref, g, bias_ref)
        m = lax.broadcasted_iota(jnp.int32, (LANE, ATT_TILE), 0)
        i = lax.broadcasted_iota(jnp.int32, (LANE, ATT_TILE), 1)
        vals = _head_vals(rb_ref, g)
        band = _bias_from_dist(i - CMP_STRIDE * (m - CMP_BAND_START) - (CMP_BLOCK - 1), vals)
        for r in range(Q_PER_KV):
            cols = slice(r * ATT_TILE, (r + 1) * ATT_TILE)
            far = jnp.zeros((n_cmp, ATT_TILE), jnp.float32) + vals[r][REL_BUCKETS - 1]
            cbias_ref[0:n_cmp, cols] = far
            cbias_ref[n_cmp:n_cmp + LANE, cols] = band[r]
            cbias_ref[n_cmp + LANE:2 * n_cmp + LANE, cols] = far

    qs = _stack_heads(q_ref)
    key = lax.broadcasted_iota(jnp.int32, (ATT_TILE, HEAD_ROWS), 0)
    qry = lax.broadcasted_iota(jnp.int32, (ATT_TILE, HEAD_ROWS), 1) & (ATT_TILE - 1)

    start = pl.multiple_of(n_cmp + CMP_BAND_START - t * tile_tokens, tile_tokens)
    c_end = lax.broadcasted_iota(jnp.int32, (n_cmp, HEAD_ROWS), 0) * CMP_STRIDE + (CMP_BLOCK - 1)
    q_pos = t * ATT_TILE + (lax.broadcasted_iota(jnp.int32, (n_cmp, HEAD_ROWS), 1) & (ATT_TILE - 1))
    ok_c = c_end <= q_pos
    lc = lax.dot_general(kc_ref[0, 0].astype(jnp.bfloat16), qs, _NT, preferred_element_type=jnp.float32) * SCALE
    lc = jnp.where(ok_c, lc + cbias_ref[pl.ds(start, n_cmp), :], NEG_INF)
    e = jnp.where(ok_c, jnp.exp(lc - jnp.max(lc, axis=0, keepdims=True)), 0.0)
    pc = e / jnp.maximum(jnp.sum(e, axis=0, keepdims=True), 1e-30)
    o_cmp = jnp.dot(vc_ref[0, 0].T.astype(jnp.bfloat16), pc.astype(jnp.bfloat16), preferred_element_type=jnp.float32)

    p_sum = pc[:, 0:ATT_TILE]
    for r in range(1, Q_PER_KV):
        p_sum = p_sum + pc[:, r * ATT_TILE:(r + 1) * ATT_TILE]
    imp = jnp.dot(ovt_ref[...], p_sum.astype(jnp.bfloat16), preferred_element_type=jnp.float32)[:n_sel]
    blk = lax.broadcasted_iota(jnp.int32, (n_sel, ATT_TILE), 0)
    own = t * blocks_per_tile + (lax.broadcasted_iota(jnp.int32, (n_sel, ATT_TILE), 1) // SEL_BLOCK)
    diff = own - blk
    valid = diff >= 0
    forced = valid & ((blk == 0) | (diff < N_LOCAL_SEL))
    score = jnp.where(forced, FORCE_SCORE, jnp.where(valid, imp, NEG_INF))
    chosen = jnp.where(_rank_select(score, blk, SEL_TOPK) & valid, 1.0, 0.0)
    if n_sel < LANE:
        chosen = jnp.concatenate([chosen, jnp.zeros((LANE - n_sel, ATT_TILE), jnp.float32)], axis=0)
    chosen = chosen.astype(jnp.bfloat16)

    def sel_mask(j):
        pick = jnp.where(
            lax.broadcasted_iota(jnp.int32, (ATT_TILE, LANE), 1)
            == j * blocks_per_tile + lax.broadcasted_iota(jnp.int32, (ATT_TILE, LANE), 0) // SEL_BLOCK,
            1.0, 0.0).astype(jnp.bfloat16)
        return _tile4_lanes(jnp.dot(pick, chosen, preferred_element_type=jnp.float32)) > 0.5

    causal = key <= qry
    state = _flash_t(qs, _kv_tile(ks_ref, t), _kv_tile_t(vs_ref, t), bias_ref[0], sel_mask(t) & causal, None)

    def far_body(j, state):
        return _flash_t(qs, _kv_tile(ks_ref, j), _kv_tile_t(vs_ref, j), bias_ref[2, 0:1, :], sel_mask(j), state)

    def prev_body(j, state):
        return _flash_t(qs, _kv_tile(ks_ref, j), _kv_tile_t(vs_ref, j), bias_ref[1], sel_mask(j), state)

    prev = jnp.maximum(t - 1, 0)
    state = lax.fori_loop(0, prev, far_body, state)
    m, l, acc = lax.fori_loop(prev, t, prev_body, state)
    o_sel = acc / jnp.maximum(l, 1e-30)

    state = _flash_t(qs, _kv_tile(kw_ref, t), _kv_tile_t(vw_ref, t), bias_ref[0], causal, None)
    state = _flash_t(qs, _kv_tile(kw_ref, prev), _kv_tile_t(vw_ref, prev), bias_ref[1], (key >= 0) & (t >= 1), state)
    t2 = jnp.maximum(t - 2, 0)
    m, l, acc = _flash_t(qs, _kv_tile(kw_ref, t2), _kv_tile_t(vw_ref, t2), bias_ref[2, 0:1, :],
                         (key > qry) & (t >= 2), state)
    o_win = acc / jnp.maximum(l, 1e-30)

    gates = jax.nn.sigmoid(gate_ref[...]).T
    gate_row = lax.broadcasted_iota(jnp.int32, gates.shape, 0)
    for r in range(Q_PER_KV):
        cols = slice(r * ATT_TILE, (r + 1) * ATT_TILE)
        out = jnp.zeros((HEAD_DIM, ATT_TILE), jnp.float32)
        for c, branch in enumerate((o_cmp, o_sel, o_win)):
            gvec = jnp.sum(jnp.where(gate_row == g * (3 * Q_PER_KV) + r * 3 + c, gates, 0.0), axis=0, keepdims=True)
            out = out + gvec * branch[:, cols]
        o_ref[:, r * HEAD_DIM:(r + 1) * HEAD_DIM] = out.T


def _tile4_lanes(x):
    return jnp.concatenate([x] * Q_PER_KV, axis=1)


def _nsa_overlap_t(n_cmp):
    ci = np.arange(n_cmp)[None, :]
    sj = np.arange(LANE)[:, None]
    ov = (ci * CMP_STRIDE < (sj + 1) * SEL_BLOCK) & (ci * CMP_STRIDE + CMP_BLOCK > sj * SEL_BLOCK) & (ci < n_cmp - 1)
    return jnp.asarray(ov, jnp.bfloat16)


def _nsa_attention(proj, kc, vc, rel_bias, n_batch, seq):
    n_tiles = seq // ATT_TILE
    n_cmp = seq // CMP_STRIDE
    q_cols = Q_PER_KV * HEAD_DIM
    kv0 = Q_DIM // HEAD_DIM
    per = KV_DIM // HEAD_DIM
    gate_col = (Q_DIM + 6 * KV_DIM) // LANE

    def kv_spec(comp):
        return pl.BlockSpec((seq, HEAD_DIM), lambda b, g, t: (b, kv0 + comp * per + g))

    cmp_spec = pl.BlockSpec((1, 1, n_cmp, HEAD_DIM), lambda b, g, t: (b, g, 0, 0))
    return pl.pallas_call(
        functools.partial(_nsa_kernel_t, n_cmp=n_cmp, n_sel=seq // SEL_BLOCK),
        grid=(n_batch, N_KV_HEADS, n_tiles),
        in_specs=[
            pl.BlockSpec(memory_space=pltpu.SMEM),
            pl.BlockSpec((ATT_TILE, q_cols), lambda b, g, t: (b * n_tiles + t, g)),
            pl.BlockSpec((ATT_TILE, LANE), lambda b, g, t: (b * n_tiles + t, gate_col)),
            cmp_spec, cmp_spec,
            kv_spec(2), kv_spec(3), kv_spec(4), kv_spec(5),
            pl.BlockSpec((LANE, n_cmp), lambda b, g, t: (0, 0)),
        ],
        out_specs=pl.BlockSpec((ATT_TILE, q_cols), lambda b, g, t: (b * n_tiles + t, g)),
        out_shape=jax.ShapeDtypeStruct((n_batch * seq, Q_DIM), jnp.float32),
        scratch_shapes=[pltpu.VMEM((3, ATT_TILE, HEAD_ROWS), jnp.float32),
                        pltpu.VMEM((2 * n_cmp + LANE, HEAD_ROWS), jnp.float32)],
        compiler_params=pltpu.CompilerParams(
            dimension_semantics=("parallel", "parallel", "arbitrary"),
            vmem_limit_bytes=VMEM_LIMIT),
    )(rel_bias, proj, proj, kc, vc, proj, proj, proj, proj, _nsa_overlap_t(n_cmp))


NEW_ROWS = Q_PER_KV * DEC_SEQ


def _rank_select_rows(score, n_cand, top_k):
    lane = lax.broadcasted_iota(jnp.int32, score.shape, 1)
    cnt = jnp.zeros(score.shape, jnp.float32)
    for m in range(n_cand):
        col = score[:, m:m + 1]
        cnt = cnt + jnp.where(col > score, 1.0, jnp.where((col == score) & (lane > m), 1.0, 0.0))
    return cnt < top_k


def _stack_group_heads(q_ref, g):
    return jnp.concatenate(
        [q_ref[:, (g * Q_PER_KV + r) * HEAD_DIM:(g * Q_PER_KV + r + 1) * HEAD_DIM] for r in range(Q_PER_KV)],
        axis=0).astype(jnp.bfloat16)


def _group_bias(rb_ref, g, dist):
    return jnp.concatenate(_bias_from_dist(dist, _head_vals(rb_ref, g)), axis=0)


def _cache_step_bias(rb_ref, g, past_len, first_key, n_keys):
    q_pos = past_len + lax.broadcasted_iota(jnp.int32, (DEC_SEQ, n_keys), 0)
    k_pos = first_key + lax.broadcasted_iota(jnp.int32, (DEC_SEQ, n_keys), 1)

    def far():
        return jnp.concatenate(
            [jnp.zeros((DEC_SEQ, n_keys), jnp.float32) + rb_ref[REL_BUCKETS - 1, g * Q_PER_KV + r]
             for r in range(Q_PER_KV)], axis=0)

    return lax.cond(past_len - (first_key + n_keys - 1) < REL_MAX_DIST,
                    lambda: _group_bias(rb_ref, g, q_pos - k_pos), far)


def _flash_update(refs, g, qs, k_tile, v_tile, bias, mask, first):
    m_ref, l_ref, acc_ref = refs
    if first:
        m, l, acc = _flash_first(qs, k_tile, v_tile, bias, mask)
    else:
        m, l, acc = _flash_next(qs, k_tile, v_tile, bias, mask, m_ref[g], l_ref[g], acc_ref[g])
    m_ref[g] = m
    l_ref[g] = l
    acc_ref[g] = acc


PAGES_PER_BLOCK = MOBA_BLOCK // PAGE_SIZE
KMEAN_PAGES = 8
MOBA_STEP_PAGES = 4
MOBA_STEP_KEYS = MOBA_STEP_PAGES * PAGE_SIZE


def _moba_kmean_kernel(pt_ref, *refs):
    o_ref = refs[KMEAN_PAGES]
    for i in range(KMEAN_PAGES // PAGES_PER_BLOCK):
        total = jnp.sum(refs[PAGES_PER_BLOCK * i][...], axis=0)
        for k in range(1, PAGES_PER_BLOCK):
            total = total + jnp.sum(refs[PAGES_PER_BLOCK * i + k][...], axis=0)
        o_ref[i] = total * (1.0 / MOBA_BLOCK)


def _page_spec(comp, which, per_step):
    return pl.BlockSpec((None, PAGE_SIZE, None, N_KV_HEADS, HEAD_DIM),
                        lambda b, n, pt: (pt[b, n * per_step + which], 0, comp, 0, 0))


def _moba_sample_kmean(cache, page_table):
    n_seq, n_pages = page_table.shape
    n_blk = n_pages // PAGES_PER_BLOCK
    blk_per_step = KMEAN_PAGES // PAGES_PER_BLOCK
    return pl.pallas_call(
        _moba_kmean_kernel,
        grid_spec=pltpu.PrefetchScalarGridSpec(
            num_scalar_prefetch=1,
            grid=(n_seq, n_pages // KMEAN_PAGES),
            in_specs=[_page_spec(0, k, KMEAN_PAGES) for k in range(KMEAN_PAGES)],
            out_specs=pl.BlockSpec((None, blk_per_step, N_KV_HEADS, HEAD_DIM), lambda b, n, pt: (b, n, 0, 0)),
        ),
        out_shape=jax.ShapeDtypeStruct((n_seq, n_blk, N_KV_HEADS, HEAD_DIM), jnp.float32),
        compiler_params=pltpu.CompilerParams(dimension_semantics=("parallel", "arbitrary")),
    )(page_table, *([cache] * KMEAN_PAGES))


def _moba_sample_kernel(pt_ref, rb_ref, q_ref, kn_ref, vn_ref, kmean_ref, *refs, n_blk, n_steps, past_len):
    k_refs = refs[:MOBA_STEP_PAGES]
    v_refs = refs[MOBA_STEP_PAGES:2 * MOBA_STEP_PAGES]
    o_ref, m_ref, l_ref, acc_ref, sel_ref = refs[2 * MOBA_STEP_PAGES:]
    n = pl.program_id(1)
    refs = (m_ref, l_ref, acc_ref)
    tok = lax.broadcasted_iota(jnp.int32, (DEC_SEQ, DEC_SEQ), 0)
    new = lax.broadcasted_iota(jnp.int32, (DEC_SEQ, DEC_SEQ), 1)

    @pl.when(n == 0)
    def _():
        for g in range(N_KV_HEADS):
            qs = _stack_group_heads(q_ref, g)
            gate = lax.dot_general(qs, kmean_ref[g].astype(jnp.bfloat16), _NT, preferred_element_type=jnp.float32)
            gate = jnp.concatenate([gate, jnp.full((NEW_ROWS, LANE - n_blk), NEG_INF, jnp.float32)], axis=1)
            lane = lax.broadcasted_iota(jnp.int32, gate.shape, 1)
            chosen = _rank_select_rows(gate, n_blk, MOBA_TOPK) & (lane < n_blk)
            sel_ref[g] = jnp.where(chosen, 1.0, 0.0)
            _flash_update(refs, g, qs, kn_ref[:, g * HEAD_DIM:(g + 1) * HEAD_DIM].astype(jnp.bfloat16),
                          vn_ref[:, g * HEAD_DIM:(g + 1) * HEAD_DIM].astype(jnp.bfloat16),
                          _group_bias(rb_ref, g, tok - new), _tile4(new <= tok), True)

    pick = jnp.where(
        lax.broadcasted_iota(jnp.int32, (LANE, MOBA_STEP_KEYS), 0)
        == n * (MOBA_STEP_KEYS // MOBA_BLOCK) + lax.broadcasted_iota(jnp.int32, (LANE, MOBA_STEP_KEYS), 1) // MOBA_BLOCK,
        1.0, 0.0).astype(jnp.bfloat16)
    for g in range(N_KV_HEADS):
        qs = _stack_group_heads(q_ref, g)
        k_tile = jnp.concatenate([ref[:, g, :] for ref in k_refs], axis=0).astype(jnp.bfloat16)
        v_tile = jnp.concatenate([ref[:, g, :] for ref in v_refs], axis=0).astype(jnp.bfloat16)
        mask = jnp.dot(sel_ref[g].astype(jnp.bfloat16), pick, preferred_element_type=jnp.float32) > 0.5
        bias = _cache_step_bias(rb_ref, g, past_len, n * MOBA_STEP_KEYS, MOBA_STEP_KEYS)
        _flash_update(refs, g, qs, k_tile, v_tile, bias, mask, False)

    @pl.when(n == n_steps - 1)
    def _():
        for g in range(N_KV_HEADS):
            out = acc_ref[g] / jnp.maximum(l_ref[g], 1e-30)
            for r in range(Q_PER_KV):
                h = g * Q_PER_KV + r
                o_ref[:, h * HEAD_DIM:(h + 1) * HEAD_DIM] = out[r * DEC_SEQ:(r + 1) * DEC_SEQ]


def _moba_sample_attention(proj, kmean, cache, page_table, rel_bias, row_block0, past_len):
    n_seq, n_pages = page_table.shape
    n_blk = n_pages // PAGES_PER_BLOCK
    n_steps = n_pages // MOBA_STEP_PAGES
    kv_cols = KV_DIM
    return pl.pallas_call(
        functools.partial(_moba_sample_kernel, n_blk=n_blk, n_steps=n_steps, past_len=past_len),
        grid_spec=pltpu.PrefetchScalarGridSpec(
            num_scalar_prefetch=1,
            grid=(n_seq, n_steps),
            in_specs=[
                pl.BlockSpec(memory_space=pltpu.SMEM),
                pl.BlockSpec((DEC_SEQ, Q_DIM), lambda b, n, pt: (row_block0 + b, 0)),
                pl.BlockSpec((DEC_SEQ, kv_cols), lambda b, n, pt: (row_block0 + b, Q_DIM // kv_cols)),
                pl.BlockSpec((DEC_SEQ, kv_cols), lambda b, n, pt: (row_block0 + b, Q_DIM // kv_cols + 1)),
                pl.BlockSpec((None, N_KV_HEADS, n_blk, HEAD_DIM), lambda b, n, pt: (b, 0, 0, 0)),
            ] + [_page_spec(comp, k, MOBA_STEP_PAGES) for comp in range(2) for k in range(MOBA_STEP_PAGES)],
            out_specs=pl.BlockSpec((DEC_SEQ, Q_DIM), lambda b, n, pt: (b, 0)),
            scratch_shapes=[
                pltpu.VMEM((N_KV_HEADS, NEW_ROWS, 1), jnp.float32),
                pltpu.VMEM((N_KV_HEADS, NEW_ROWS, 1), jnp.float32),
                pltpu.VMEM((N_KV_HEADS, NEW_ROWS, HEAD_DIM), jnp.float32),
                pltpu.VMEM((N_KV_HEADS, NEW_ROWS, LANE), jnp.float32),
            ],
        ),
        out_shape=jax.ShapeDtypeStruct((n_seq * DEC_SEQ, Q_DIM), jnp.float32),
        compiler_params=pltpu.CompilerParams(dimension_semantics=("parallel", "arbitrary")),
    )(page_table, rel_bias, proj, proj, proj, kmean, *([cache] * (2 * MOBA_STEP_PAGES)))


CMP_PAGES = 8
SEL_STEP_PAGES = 4
SEL_STEP_KEYS = SEL_STEP_PAGES * PAGE_SIZE
CHUNKS_PER_PAGE = PAGE_SIZE // CMP_STRIDE


def _nsa_sample_compress_kernel(pt_ref, *refs):
    page_refs = refs[:CMP_PAGES]
    pe_ref, w_ref, top_ref, bot_ref = refs[CMP_PAGES:]
    rows = CMP_PAGES * CHUNKS_PER_PAGE
    for c in range(2):
        top = jnp.zeros((N_KV_HEADS * rows, HEAD_DIM), jnp.float32)
        bot = jnp.zeros((N_KV_HEADS * rows, HEAD_DIM), jnp.float32)
        for r in range(CMP_STRIDE):
            x = jnp.concatenate(
                [ref[pl.ds(r, CHUNKS_PER_PAGE, stride=CMP_STRIDE), c, g, :]
                 for g in range(N_KV_HEADS) for ref in page_refs], axis=0)
            xt = (x + pe_ref[c, r:r + 1, :]).astype(jnp.bfloat16)
            xb = (x + pe_ref[c, CMP_STRIDE + r:CMP_STRIDE + r + 1, :]).astype(jnp.bfloat16)
            top = top + jnp.dot(xt, w_ref[c, r * HEAD_DIM:(r + 1) * HEAD_DIM, :], preferred_element_type=jnp.float32)
            bot = bot + jnp.dot(xb, w_ref[c, (CMP_STRIDE + r) * HEAD_DIM:(CMP_STRIDE + r + 1) * HEAD_DIM, :],
                                preferred_element_type=jnp.float32)
        for g in range(N_KV_HEADS):
            top_ref[c, g] = top[g * rows:(g + 1) * rows]
            bot_ref[c, g] = bot[g * rows:(g + 1) * rows]


def _nsa_sample_compress(cache, page_table, cmp_pe, cmp_w):
    n_seq, n_pages = page_table.shape
    n_chunks = n_pages * CHUNKS_PER_PAGE
    rows = CMP_PAGES * CHUNKS_PER_PAGE

    def page_spec(k):
        return pl.BlockSpec((None, PAGE_SIZE, 2, N_KV_HEADS, HEAD_DIM),
                            lambda b, s, pt: (pt[b, s * CMP_PAGES + k], 0, 0, 0, 0))

    out = jax.ShapeDtypeStruct((n_seq, 2, N_KV_HEADS, n_chunks, HEAD_DIM), jnp.float32)
    out_spec = pl.BlockSpec((None, 2, N_KV_HEADS, rows, HEAD_DIM), lambda b, s, pt: (b, 0, 0, s, 0))
    return pl.pallas_call(
        _nsa_sample_compress_kernel,
        grid_spec=pltpu.PrefetchScalarGridSpec(
            num_scalar_prefetch=1,
            grid=(n_seq, n_pages // CMP_PAGES),
            in_specs=[page_spec(k) for k in range(CMP_PAGES)] + [
                pl.BlockSpec((2, CMP_BLOCK, HEAD_DIM), lambda b, s, pt: (0, 0, 0)),
                pl.BlockSpec((2, CMP_BLOCK * HEAD_DIM, HEAD_DIM), lambda b, s, pt: (0, 0, 0)),
            ],
            out_specs=[out_spec, out_spec],
        ),
        out_shape=[out, out],
        compiler_params=pltpu.CompilerParams(
            dimension_semantics=("parallel", "parallel"), vmem_limit_bytes=VMEM_LIMIT),
    )(page_table, *([cache] * CMP_PAGES), cmp_pe, cmp_w)


def _nsa_sample_kernel(pt_ref, rb_ref, q_ref, gate_ref, ksn_ref, vsn_ref, kwn_ref, vwn_ref, top_ref, bot_ref,
                       ov_ref, win_ref, *refs, n_pages, past_len):
    page_refs = refs[:SEL_STEP_PAGES]
    o_ref, m_ref, l_ref, acc_ref, sel_ref, ocmp_ref, owin_ref = refs[SEL_STEP_PAGES:]
    p = pl.program_id(1)
    refs = (m_ref, l_ref, acc_ref)
    n_chunks = n_pages * CHUNKS_PER_PAGE
    n_sel_pad = ov_ref.shape[1]
    n_sel = past_len // SEL_BLOCK + 1
    win_len = win_ref.shape[0]
    tok = lax.broadcasted_iota(jnp.int32, (DEC_SEQ, DEC_SEQ), 0)
    new = lax.broadcasted_iota(jnp.int32, (DEC_SEQ, DEC_SEQ), 1)

    def group_cols(ref, g):
        return ref[:, g * HEAD_DIM:(g + 1) * HEAD_DIM].astype(jnp.bfloat16)

    @pl.when(p == 0)
    def _():
        for g in range(N_KV_HEADS):
            qs = _stack_group_heads(q_ref, g)
            kc = top_ref[0, g] + pltpu.roll(bot_ref[0, g], n_chunks - 1, axis=0)
            vc = top_ref[1, g] + pltpu.roll(bot_ref[1, g], n_chunks - 1, axis=0)
            q_pos = past_len + lax.broadcasted_iota(jnp.int32, (DEC_SEQ, n_chunks), 0)
            tok_id = lax.broadcasted_iota(jnp.int32, (DEC_SEQ, n_chunks), 1)
            c_end = tok_id * CMP_STRIDE + (CMP_BLOCK - 1)
            ok_c = _tile4((tok_id < n_chunks - 1) & (c_end <= q_pos))
            lc = lax.dot_general(qs, kc.astype(jnp.bfloat16), _NT, preferred_element_type=jnp.float32) * SCALE
            lc = jnp.where(ok_c, lc + _group_bias(rb_ref, g, q_pos - c_end), NEG_INF)
            e = jnp.where(ok_c, jnp.exp(lc - jnp.max(lc, axis=-1, keepdims=True)), 0.0)
            pc = e / jnp.maximum(jnp.sum(e, axis=-1, keepdims=True), 1e-30)
            ocmp_ref[g] = jnp.dot(pc.astype(jnp.bfloat16), vc.astype(jnp.bfloat16), preferred_element_type=jnp.float32)
            p_sum = pc[0:DEC_SEQ]
            for r in range(1, Q_PER_KV):
                p_sum = p_sum + pc[r * DEC_SEQ:(r + 1) * DEC_SEQ]
            imp = jnp.dot(p_sum.astype(jnp.bfloat16), ov_ref[...], preferred_element_type=jnp.float32)
            blk = lax.broadcasted_iota(jnp.int32, (DEC_SEQ, n_sel_pad), 1)
            own = (past_len + lax.broadcasted_iota(jnp.int32, (DEC_SEQ, n_sel_pad), 0)) // SEL_BLOCK
            diff = own - blk
            valid = diff >= 0
            forced = valid & ((blk == 0) | (diff < N_LOCAL_SEL))
            score = jnp.where(forced, FORCE_SCORE, jnp.where(valid, imp, NEG_INF))
            chosen = _rank_select_rows(score, n_sel, SEL_TOPK) & valid
            sel_ref[g] = _tile4(jnp.where(chosen, 1.0, 0.0))
            causal = _tile4(new <= tok)
            near = _group_bias(rb_ref, g, tok - new)
            _flash_update(refs, g, qs, group_cols(ksn_ref, g), group_cols(vsn_ref, g), near, causal, True)
            mw, lw, aw = _flash_first(qs, group_cols(kwn_ref, g), group_cols(vwn_ref, g), near, causal)
            tok_w = lax.broadcasted_iota(jnp.int32, (DEC_SEQ, win_len), 0)
            dist_w = win_len + tok_w - lax.broadcasted_iota(jnp.int32, (DEC_SEQ, win_len), 1)
            mw, lw, aw = _flash_next(qs, win_ref[:, 0, g, :].astype(jnp.bfloat16), win_ref[:, 1, g, :].astype(jnp.bfloat16),
                                     _group_bias(rb_ref, g, dist_w), _tile4(dist_w < WINDOW), mw, lw, aw)
            owin_ref[g] = aw / jnp.maximum(lw, 1e-30)

    pick = jnp.where(
        lax.broadcasted_iota(jnp.int32, (n_sel_pad, SEL_STEP_KEYS), 0)
        == p * (SEL_STEP_KEYS // SEL_BLOCK) + lax.broadcasted_iota(jnp.int32, (n_sel_pad, SEL_STEP_KEYS), 1) // SEL_BLOCK,
        1.0, 0.0).astype(jnp.bfloat16)
    for g in range(N_KV_HEADS):
        qs = _stack_group_heads(q_ref, g)
        k_tile = jnp.concatenate([ref[:, 0, g, :] for ref in page_refs], axis=0).astype(jnp.bfloat16)
        v_tile = jnp.concatenate([ref[:, 1, g, :] for ref in page_refs], axis=0).astype(jnp.bfloat16)
        mask = jnp.dot(sel_ref[g].astype(jnp.bfloat16), pick, preferred_element_type=jnp.float32) > 0.5
        bias = _cache_step_bias(rb_ref, g, past_len, p * SEL_STEP_KEYS, SEL_STEP_KEYS)
        _flash_update(refs, g, qs, k_tile, v_tile, bias, mask, False)

    @pl.when(p == n_pages // SEL_STEP_PAGES - 1)
    def _():
        gates = jax.nn.sigmoid(gate_ref[...])
        for g in range(N_KV_HEADS):
            o_sel = acc_ref[g] / jnp.maximum(l_ref[g], 1e-30)
            for r in range(Q_PER_KV):
                rows = slice(r * DEC_SEQ, (r + 1) * DEC_SEQ)
                col = (g * Q_PER_KV + r) * 3
                out = (gates[:, col:col + 1] * ocmp_ref[g][rows] + gates[:, col + 1:col + 2] * o_sel[rows]
                       + gates[:, col + 2:col + 3] * owin_ref[g][rows])
                h = g * Q_PER_KV + r
                o_ref[:, h * HEAD_DIM:(h + 1) * HEAD_DIM] = out


def _nsa_sample_overlap(n_chunks, n_sel_pad):
    ci = np.arange(n_chunks)[:, None]
    sj = np.arange(n_sel_pad)[None, :]
    ov = (ci * CMP_STRIDE < (sj + 1) * SEL_BLOCK) & (ci * CMP_STRIDE + CMP_BLOCK > sj * SEL_BLOCK) & (ci < n_chunks - 1)
    return jnp.asarray(ov, jnp.bfloat16)


def _nsa_sample(proj, cache, win_buf, page_table, cmp_pe, cmp_w, rel_bias, row_block0, past_len):
    n_seq, n_pages = page_table.shape
    n_chunks = n_pages * CHUNKS_PER_PAGE
    n_sel_pad = -(-(past_len // SEL_BLOCK + 1) // LANE) * LANE
    top, bot = _nsa_sample_compress(cache, page_table, cmp_pe, cmp_w)
    kv0 = Q_DIM // KV_DIM
    gate_col = (Q_DIM + 6 * KV_DIM) // LANE

    def new_spec(comp):
        return pl.BlockSpec((DEC_SEQ, KV_DIM), lambda b, p, pt: (row_block0 + b, kv0 + comp))

    def sel_page_spec(k):
        return pl.BlockSpec((None, PAGE_SIZE, 2, N_KV_HEADS, HEAD_DIM),
                            lambda b, p, pt: (pt[b, p * SEL_STEP_PAGES + k], 0, 1, 0, 0))

    cmp_spec = pl.BlockSpec((None, 2, N_KV_HEADS, n_chunks, HEAD_DIM), lambda b, p, pt: (b, 0, 0, 0, 0))
    scratch = [
        pltpu.VMEM((N_KV_HEADS, NEW_ROWS, 1), jnp.float32),
        pltpu.VMEM((N_KV_HEADS, NEW_ROWS, 1), jnp.float32),
        pltpu.VMEM((N_KV_HEADS, NEW_ROWS, HEAD_DIM), jnp.float32),
        pltpu.VMEM((N_KV_HEADS, NEW_ROWS, n_sel_pad), jnp.float32),
        pltpu.VMEM((N_KV_HEADS, NEW_ROWS, HEAD_DIM), jnp.float32),
        pltpu.VMEM((N_KV_HEADS, NEW_ROWS, HEAD_DIM), jnp.float32),
    ]
    return pl.pallas_call(
        functools.partial(_nsa_sample_kernel, n_pages=n_pages, past_len=past_len),
        grid_spec=pltpu.PrefetchScalarGridSpec(
            num_scalar_prefetch=1,
            grid=(n_seq, n_pages // SEL_STEP_PAGES),
            in_specs=[
                pl.BlockSpec(memory_space=pltpu.SMEM),
                pl.BlockSpec((DEC_SEQ, Q_DIM), lambda b, p, pt: (row_block0 + b, 0)),
                pl.BlockSpec((DEC_SEQ, LANE), lambda b, p, pt: (row_block0 + b, gate_col)),
                new_spec(2), new_spec(3), new_spec(4), new_spec(5),
                cmp_spec, cmp_spec,
                pl.BlockSpec((n_chunks, n_sel_pad), lambda b, p, pt: (0, 0)),
                pl.BlockSpec((None,) + win_buf.shape[1:], lambda b, p, pt: (b, 0, 0, 0, 0)),
            ] + [sel_page_spec(k) for k in range(SEL_STEP_PAGES)],
            out_specs=pl.BlockSpec((DEC_SEQ, Q_DIM), lambda b, p, pt: (b, 0)),
            scratch_shapes=scratch,
        ),
        out_shape=jax.ShapeDtypeStruct((n_seq * DEC_SEQ, Q_DIM), jnp.float32),
        compiler_params=pltpu.CompilerParams(
            dimension_semantics=("parallel", "arbitrary"), vmem_limit_bytes=VMEM_LIMIT),
    )(page_table, rel_bias, proj, proj, proj, proj, proj, proj, top, bot,
      _nsa_sample_overlap(n_chunks, n_sel_pad), win_buf, *([cache] * SEL_STEP_PAGES))


SSD_PAIR = 2 * SSD_HEAD_DIM
GROUP_COLS = HEADS_PER_GROUP * SSD_HEAD_DIM
CARRY_ROWS = 8


def _pad_rows(x, rows):
    if x.shape[0] == rows:
        return x
    return jnp.concatenate([x, jnp.zeros((rows - x.shape[0],) + x.shape[1:], x.dtype)], axis=0)


def _ssd_kernel(z_ref, x_ref, b_ref, c_ref, dt_ref, prev_ref, s0_ref, cw_ref, cb_ref, dtb_ref, alog_ref, dsk_ref,
                nw_ref, y_ref, conv_ref, state_ref, carry_ref, s_ref, act_ref, *, rows_in, n_chunks):
    c = pl.program_id(1)
    L = SSD_CHUNK

    @pl.when(c == 0)
    def _():
        carry_ref[...] = prev_ref[...]
        s_ref[...] = s0_ref[...]

    row8 = lax.broadcasted_iota(jnp.int32, (CARRY_ROWS, 1), 0)
    off = 0
    for ref in (x_ref, b_ref, c_ref):
        width = ref.shape[1]
        cols = slice(off, off + width)
        raw_in = ref[...]
        raw = _pad_rows(raw_in, L)
        prev = carry_ref[:, cols]
        acc = cb_ref[:, cols] + cw_ref[CONV_W - 1:CONV_W, cols] * raw
        for k in range(1, CONV_W):
            rolled = pltpu.roll(raw, k, axis=0)
            head = jnp.where(row8 < k, pltpu.roll(prev, k, axis=0), rolled[:CARRY_ROWS])
            shifted = jnp.concatenate([head, rolled[CARRY_ROWS:]], axis=0)
            acc = acc + cw_ref[CONV_W - 1 - k:CONV_W - k, cols] * shifted
        act_ref[:, cols] = acc * jax.nn.sigmoid(acc)
        carry_ref[:, cols] = raw_in[rows_in - CARRY_ROWS:rows_in]

        @pl.when(c == n_chunks - 1)
        def _():
            conv_ref[:, cols] = raw_in[rows_in - (CONV_W - 1):rows_in]

        off += width

    row = lax.broadcasted_iota(jnp.int32, (L, LANE), 0)
    pre = _pad_rows(dt_ref[...], L) + dtb_ref[...]
    dt = jnp.maximum(pre, 0.0) + jnp.log1p(jnp.exp(-jnp.abs(pre)))
    dt = jnp.where(row < rows_in, dt, 0.0)
    acum = dt * (-jnp.exp(alog_ref[...]))
    shift = 1
    while shift < L:
        acum = acum + jnp.where(row >= shift, pltpu.roll(acum, shift, axis=0), 0.0)
        shift *= 2
    acum_t = acum.T
    exp_a = jnp.exp(acum)
    to_end = jnp.exp(acum[L - 1:L, :] - acum)
    li = lax.broadcasted_iota(jnp.int32, (L, L), 0)
    si = lax.broadcasted_iota(jnp.int32, (L, L), 1)
    causal = li >= si
    first_half = lax.broadcasted_iota(jnp.int32, (L, SSD_PAIR), 1) < SSD_HEAD_DIM

    def pair_cols(v, h):
        return jnp.where(first_half, v[:, h:h + 1], v[:, h + 1:h + 2])

    for g in range(SSD_GROUPS):
        bg = act_ref[:, D_INNER + g * D_STATE:D_INNER + (g + 1) * D_STATE].astype(jnp.bfloat16)
        cg = act_ref[:, D_INNER + GN + g * D_STATE:D_INNER + GN + (g + 1) * D_STATE].astype(jnp.bfloat16)
        cb = lax.dot_general(cg, bg, _NT, preferred_element_type=jnp.float32)
        s_prev = s_ref[g * GROUP_COLS:(g + 1) * GROUP_COLS, :]
        y_off = lax.dot_general(cg, s_prev.astype(jnp.bfloat16), _NT, preferred_element_type=jnp.float32)
        xw_parts = []
        for pr in range(HEADS_PER_GROUP // 2):
            h = g * HEADS_PER_GROUP + 2 * pr
            cols = slice(h * SSD_HEAD_DIM, (h + 2) * SSD_HEAD_DIM)
            xa = act_ref[:, cols]
            xdt = xa * pair_cols(dt, h)
            y = y_off[:, 2 * pr * SSD_HEAD_DIM:(2 * pr + 2) * SSD_HEAD_DIM] * pair_cols(exp_a, h)
            for half in range(2):
                hh = h + half
                decay = jnp.where(causal, jnp.exp(acum[:, hh:hh + 1] - acum_t[hh:hh + 1, :]), 0.0)
                mine = first_half if half == 0 else jnp.logical_not(first_half)
                y = y + jnp.dot((cb * decay).astype(jnp.bfloat16), jnp.where(mine, xdt, 0.0).astype(jnp.bfloat16),
                                preferred_element_type=jnp.float32)
            zz = z_ref[:, cols]
            y = y[:rows_in] + xa[:rows_in] * dsk_ref[:, cols]
            y_ref[:, cols] = y * (zz * jax.nn.sigmoid(zz))
            xw_parts.append(xdt * pair_cols(to_end, h))
        xw = jnp.concatenate(xw_parts, axis=1)
        chunk_state = jnp.dot(xw.T.astype(jnp.bfloat16), bg, preferred_element_type=jnp.float32)
        for r in range(HEADS_PER_GROUP):
            h = g * HEADS_PER_GROUP + r
            rows = slice(h * SSD_HEAD_DIM, (h + 1) * SSD_HEAD_DIM)
            s_ref[rows, :] = (s_ref[rows, :] * jnp.exp(acum_t[h:h + 1, L - 1:L])
                              + chunk_state[r * SSD_HEAD_DIM:(r + 1) * SSD_HEAD_DIM])
        gcols = slice(g * GROUP_COLS, (g + 1) * GROUP_COLS)
        yg = y_ref[:, gcols]
        y_ref[:, gcols] = yg * lax.rsqrt(jnp.mean(yg * yg, axis=-1, keepdims=True) + RMS_EPS) * nw_ref[:, gcols]

    @pl.when(c == n_chunks - 1)
    def _():
        state_ref[...] = s_ref[...]


def _ssd_mix(proj, conv_prev, ssm_prev, conv_w, conv_b, dt_bias, a_log, d_skip, norm_w, row_block0, rows_in, n_chunks):
    n_seq = conv_prev.shape[0]

    def rows_spec(width, col_block):
        return pl.BlockSpec((rows_in, width), lambda b, c: (row_block0 + b * n_chunks + c, col_block))

    def const_spec(shape):
        return pl.BlockSpec(shape, lambda b, c: (0,) * len(shape))

    pad_lanes = lambda v: jnp.pad(v, (0, LANE - v.shape[0])).reshape(1, LANE)
    return pl.pallas_call(
        functools.partial(_ssd_kernel, rows_in=rows_in, n_chunks=n_chunks),
        grid=(n_seq, n_chunks),
        in_specs=[
            rows_spec(D_INNER, 0),
            rows_spec(D_INNER, 1),
            rows_spec(GN, 2 * D_INNER // GN),
            rows_spec(GN, 2 * D_INNER // GN + 1),
            rows_spec(LANE, (D_INNER + CONV_DIM) // LANE),
            pl.BlockSpec((None, CARRY_ROWS, CONV_DIM), lambda b, c: (b, 0, 0)),
            pl.BlockSpec((None, D_INNER, D_STATE), lambda b, c: (b, 0, 0)),
            const_spec((CONV_W, CONV_DIM)),
            const_spec((1, CONV_DIM)),
            const_spec((1, LANE)),
            const_spec((1, LANE)),
            const_spec((1, D_INNER)),
            const_spec((1, D_INNER)),
        ],
        out_specs=[
            pl.BlockSpec((rows_in, D_INNER), lambda b, c: (b * n_chunks + c, 0)),
            pl.BlockSpec((None, CONV_W - 1, CONV_DIM), lambda b, c: (b, 0, 0)),
            pl.BlockSpec((None, D_INNER, D_STATE), lambda b, c: (b, 0, 0)),
        ],
        out_shape=[
            jax.ShapeDtypeStruct((n_seq * n_chunks * rows_in, D_INNER), jnp.float32),
            jax.ShapeDtypeStruct((n_seq, CONV_W - 1, CONV_DIM), jnp.float32),
            jax.ShapeDtypeStruct((n_seq, D_INNER, D_STATE), jnp.float32),
        ],
        scratch_shapes=[
            pltpu.VMEM((CARRY_ROWS, CONV_DIM), jnp.float32),
            pltpu.VMEM((D_INNER, D_STATE), jnp.float32),
            pltpu.VMEM((SSD_CHUNK, CONV_DIM), jnp.float32),
        ],
        compiler_params=pltpu.CompilerParams(
            dimension_semantics=("parallel", "arbitrary"), vmem_limit_bytes=VMEM_LIMIT),
    )(proj, proj, proj, proj, proj, conv_prev, ssm_prev, conv_w, conv_b.reshape(1, CONV_DIM),
      pad_lanes(dt_bias), pad_lanes(a_log), jnp.repeat(d_skip, SSD_HEAD_DIM).reshape(1, D_INNER),
      norm_w.reshape(1, D_INNER))


def _pad_cols(w, n):
    return jnp.pad(w, ((0, 0), (0, n - w.shape[1])))


def _split_rows(a):
    return (a[:M_PROMPT].reshape(BATCH, SEQ, a.shape[-1]),
            a[M_PROMPT:].reshape(DEC_BATCH, DEC_SEQ, a.shape[-1]))


def _join_rows(p, s):
    return jnp.concatenate([p.reshape(M_PROMPT, p.shape[-1]), s.reshape(M_SAMPLE, s.shape[-1])], axis=0)


def kernel(x_prompt, x_sample, state_ssm, state_conv, cache_nsa_kv, cache_nsa_win, cache_moba_kv, page_table, rel_bias, norm_w, final_norm_w, ffn_w_gate, ffn_w_up, ffn_w_down, ssd_w_in, ssd_conv_w, ssd_conv_b, ssd_dt_bias, ssd_a_log, ssd_d, ssd_norm_w, ssd_w_out, nsa_w_in, nsa_cmp_pe, nsa_cmp_w, nsa_w_out, moba_w_in, moba_w_out):
    bf = jnp.bfloat16
    h = _join_rows(x_prompt, x_sample)
    ssm_p, ssm_s, conv_p, conv_s = [], [], [], []
    nsa_kv_p, nsa_kv_s, nsa_win_p, nsa_win_s = [], [], [], []
    moba_kv_p, moba_kv_s = [], []
    for i in range(DEPTH):
        kind, j = i % N_MIXERS, i // N_MIXERS
        h = _ffn_half(h, norm_w[i, 0], ffn_w_gate[i, 0].astype(bf), ffn_w_up[i, 0].astype(bf), ffn_w_down[i, 0].astype(bf))
        if kind == 0:
            n_pad = -(-SSD_IN_DIM // 512) * 512
            proj = _norm_proj(h, norm_w[i, 1], _pad_cols(ssd_w_in[j], n_pad).astype(bf))
            w = (ssd_conv_w[j], ssd_conv_b[j], ssd_dt_bias[j], ssd_a_log[j], ssd_d[j], ssd_norm_w[j])
            conv0 = jnp.zeros((BATCH, CARRY_ROWS, CONV_DIM), jnp.float32)
            ssm0 = jnp.zeros((BATCH, D_INNER, D_STATE), jnp.float32)
            y_p, c_p, s_p = _ssd_mix(proj, conv0, ssm0, *w, 0, SSD_CHUNK, SEQ // SSD_CHUNK)
            conv1 = jnp.pad(state_conv[j], ((0, 0), (CARRY_ROWS - (CONV_W - 1), 0), (0, 0)))
            ssm1 = state_ssm[j].reshape(DEC_BATCH, D_INNER, D_STATE)
            y_s, c_s, s_s = _ssd_mix(proj, conv1, ssm1, *w, M_PROMPT // DEC_SEQ, DEC_SEQ, 1)
            conv_p.append(c_p)
            conv_s.append(c_s)
            ssm_p.append(s_p.reshape(BATCH, SSD_HEADS, SSD_HEAD_DIM, D_STATE))
            ssm_s.append(s_s.reshape(DEC_BATCH, SSD_HEADS, SSD_HEAD_DIM, D_STATE))
            w_out = ssd_w_out[j]
        elif kind == 1:
            n_pad = -(-NSA_IN_DIM // 512) * 512
            proj = _norm_proj(h, norm_w[i, 1], _pad_cols(nsa_w_in[j], n_pad).astype(bf))
            proj_p, proj_s = _split_rows(proj)
            kc, vc = _nsa_compress_prompt(proj, nsa_cmp_pe[j], nsa_cmp_w[j].astype(bf), BATCH, SEQ)
            y_p = _nsa_attention(proj, kc, vc, rel_bias, BATCH, SEQ)
            kv_all = proj_p[..., Q_DIM:Q_DIM + 6 * KV_DIM].reshape(BATCH, SEQ, 6, N_KV_HEADS, HEAD_DIM)
            kv_p, win_p = kv_all[:, :, :4], kv_all[:, SEQ - WINDOW:, 4:]
            cmp_w_bf = nsa_cmp_w[j].astype(bf)
            y_s = _nsa_sample(proj, cache_nsa_kv[j], cache_nsa_win[j], page_table, nsa_cmp_pe[j], cmp_w_bf,
                              rel_bias, M_PROMPT // DEC_SEQ, PAST_LEN)
            kv_new = proj_s[..., Q_DIM:Q_DIM + 6 * KV_DIM].reshape(DEC_BATCH, DEC_SEQ, 6, N_KV_HEADS, HEAD_DIM)
            kv_s = kv_new[:, :, :4]
            win_s = jnp.concatenate([cache_nsa_win[j][:, DEC_SEQ:], kv_new[:, :, 4:]], axis=1)
            nsa_kv_p.append(kv_p)
            nsa_kv_s.append(kv_s)
            nsa_win_p.append(win_p)
            nsa_win_s.append(win_s)
            w_out = nsa_w_out[j]
        else:
            proj = _norm_proj(h, norm_w[i, 1], moba_w_in[j].astype(bf))
            proj_p, proj_s = _split_rows(proj)
            y_p = _moba_attention(proj, rel_bias, BATCH, SEQ)
            kv_p = proj_p[..., Q_DIM:].reshape(BATCH, SEQ, 2, N_KV_HEADS, HEAD_DIM)
            kmean = _moba_sample_kmean(cache_moba_kv[j], page_table)
            y_s = _moba_sample_attention(proj, jnp.swapaxes(kmean, 1, 2), cache_moba_kv[j], page_table, rel_bias,
                                         M_PROMPT // DEC_SEQ, PAST_LEN)
            kv_s = proj_s[..., Q_DIM:].reshape(DEC_BATCH, DEC_SEQ, 2, N_KV_HEADS, HEAD_DIM)
            moba_kv_p.append(kv_p)
            moba_kv_s.append(kv_s)
            w_out = moba_w_out[j]
        h = _proj_residual(_join_rows(y_p, y_s), w_out.astype(bf), h)
        h = _ffn_half(h, norm_w[i, 2], ffn_w_gate[i, 1].astype(bf), ffn_w_up[i, 1].astype(bf), ffn_w_down[i, 1].astype(bf))
    y = _final_norm(h, final_norm_w)
    y_prompt, y_sample = _split_rows(y)
    return (y_prompt, y_sample,
            jnp.stack(ssm_p), jnp.stack(ssm_s), jnp.stack(conv_p), jnp.stack(conv_s),
            jnp.stack(nsa_kv_p), jnp.stack(nsa_kv_s), jnp.stack(nsa_win_p), jnp.stack(nsa_win_s),
            jnp.stack(moba_kv_p), jnp.stack(moba_kv_s))
```

```python
import functools
import math

import jax
import jax.numpy as jnp
import numpy as np
from jax import lax
from jax.experimental import pallas as pl
from jax.experimental.pallas import tpu as pltpu

D_MODEL = 2048
BATCH = 2
SEQ = 4096
DEPTH = 4
DEC_BATCH = 8
DEC_SEQ = 8
PAST_LEN = 16384
PAGE_SIZE = 128
N_MIXERS = 3
D_FF = 5632
RMS_EPS = 1e-6
D_INNER = 2 * D_MODEL
SSD_HEAD_DIM = 64
SSD_HEADS = D_INNER // SSD_HEAD_DIM
D_STATE = 128
SSD_GROUPS = 8
HEADS_PER_GROUP = SSD_HEADS // SSD_GROUPS
GN = SSD_GROUPS * D_STATE
CONV_W = 4
CONV_DIM = D_INNER + 2 * GN
SSD_IN_DIM = D_INNER + CONV_DIM + SSD_HEADS
SSD_CHUNK = 128
HEAD_DIM = 128
N_HEADS = D_MODEL // HEAD_DIM
N_KV_HEADS = 4
Q_PER_KV = N_HEADS // N_KV_HEADS
Q_DIM = N_HEADS * HEAD_DIM
KV_DIM = N_KV_HEADS * HEAD_DIM
Q_BLOCK = 64
CMP_BLOCK = 32
CMP_STRIDE = 16
SEL_BLOCK = 64
SEL_TOPK = 16
N_LOCAL_SEL = 2
WINDOW = 512
NSA_IN_DIM = Q_DIM + 6 * KV_DIM + 3 * N_HEADS
MOBA_BLOCK = 256
MOBA_TOPK = 3
MOBA_IN_DIM = Q_DIM + 2 * KV_DIM
REL_BUCKETS = 32
REL_MAX_EXACT = REL_BUCKETS // 2
REL_MAX_DIST = 128
NEG_INF = -1e30
FORCE_SCORE = 1e9

M_PROMPT = BATCH * SEQ
M_SAMPLE = DEC_BATCH * DEC_SEQ
M_TOTAL = M_PROMPT + M_SAMPLE
ROW_TILE = 688
LANE = 128
VMEM_LIMIT = 56 * 1024 * 1024


def _rms_rows(x, g):
    ms = jnp.mean(x * x, axis=-1, keepdims=True)
    return x * lax.rsqrt(ms + RMS_EPS) * g


def _ffn_kernel(h_ref, g_ref, wg_ref, wu_ref, wd_ref, o_ref, xn_ref, acc_ref):
    f = pl.program_id(1)

    @pl.when(f == 0)
    def _():
        xn_ref[...] = _rms_rows(h_ref[...], g_ref[...]).astype(jnp.bfloat16)
        acc_ref[...] = jnp.zeros_like(acc_ref)

    xn = xn_ref[...]
    gate = jnp.dot(xn, wg_ref[...], preferred_element_type=jnp.float32)
    up = jnp.dot(xn, wu_ref[...], preferred_element_type=jnp.float32)
    act = (gate * jax.nn.sigmoid(gate) * up).astype(jnp.bfloat16)
    acc_ref[...] += jnp.dot(act, wd_ref[...], preferred_element_type=jnp.float32)

    @pl.when(f == pl.num_programs(1) - 1)
    def _():
        o_ref[...] = h_ref[...] + 0.5 * acc_ref[...]


def _ffn_half(h, g, wg, wu, wd, *, ff_tile=512):
    m, d = h.shape
    dff = wg.shape[1]
    grid = (m // ROW_TILE, dff // ff_tile)
    return pl.pallas_call(
        _ffn_kernel,
        grid=grid,
        in_specs=[
            pl.BlockSpec((ROW_TILE, d), lambda i, f: (i, 0)),
            pl.BlockSpec((1, d), lambda i, f: (0, 0)),
            pl.BlockSpec((d, ff_tile), lambda i, f: (0, f)),
            pl.BlockSpec((d, ff_tile), lambda i, f: (0, f)),
            pl.BlockSpec((ff_tile, d), lambda i, f: (f, 0)),
        ],
        out_specs=pl.BlockSpec((ROW_TILE, d), lambda i, f: (i, 0)),
        out_shape=jax.ShapeDtypeStruct((m, d), jnp.float32),
        scratch_shapes=[
            pltpu.VMEM((ROW_TILE, d), jnp.bfloat16),
            pltpu.VMEM((ROW_TILE, d), jnp.float32),
        ],
        compiler_params=pltpu.CompilerParams(
            dimension_semantics=("parallel", "arbitrary"),
            vmem_limit_bytes=VMEM_LIMIT),
    )(h, g.reshape(1, d), wg, wu, wd)


def _norm_proj_kernel(h_ref, g_ref, w_ref, o_ref, xn_ref):
    @pl.when(pl.program_id(1) == 0)
    def _():
        xn_ref[...] = _rms_rows(h_ref[...], g_ref[...]).astype(jnp.bfloat16)

    o_ref[...] = jnp.dot(xn_ref[...], w_ref[...], preferred_element_type=jnp.float32)


PROJ_TILE_MAX = 1536


def _norm_proj(h, g, w):
    m, d = h.shape
    n = w.shape[1]
    n_tile = max(t for t in range(LANE, PROJ_TILE_MAX + 1, LANE) if n % t == 0)
    grid = (m // ROW_TILE, n // n_tile)
    return pl.pallas_call(
        _norm_proj_kernel,
        grid=grid,
        in_specs=[
            pl.BlockSpec((ROW_TILE, d), lambda i, j: (i, 0)),
            pl.BlockSpec((1, d), lambda i, j: (0, 0)),
            pl.BlockSpec((d, n_tile), lambda i, j: (0, j)),
        ],
        out_specs=pl.BlockSpec((ROW_TILE, n_tile), lambda i, j: (i, j)),
        out_shape=jax.ShapeDtypeStruct((m, n), jnp.float32),
        scratch_shapes=[pltpu.VMEM((ROW_TILE, d), jnp.bfloat16)],
        compiler_params=pltpu.CompilerParams(
            dimension_semantics=("parallel", "arbitrary"),
            vmem_limit_bytes=VMEM_LIMIT),
    )(h, g.reshape(1, d), w)


def _proj_residual_kernel(y_ref, w_ref, h_ref, o_ref):
    o_ref[...] = h_ref[...] + jnp.dot(
        y_ref[...].astype(jnp.bfloat16), w_ref[...], preferred_element_type=jnp.float32)


def _proj_residual(y, w, h, *, n_tile=512):
    m, k = y.shape
    d = w.shape[1]
    grid = (m // ROW_TILE, d // n_tile)
    return pl.pallas_call(
        _proj_residual_kernel,
        grid=grid,
        in_specs=[
            pl.BlockSpec((ROW_TILE, k), lambda i, j: (i, 0)),
            pl.BlockSpec((k, n_tile), lambda i, j: (0, j)),
            pl.BlockSpec((ROW_TILE, n_tile), lambda i, j: (i, j)),
        ],
        out_specs=pl.BlockSpec((ROW_TILE, n_tile), lambda i, j: (i, j)),
        out_shape=jax.ShapeDtypeStruct((m, d), jnp.float32),
        compiler_params=pltpu.CompilerParams(
            dimension_semantics=("parallel", "arbitrary"),
            vmem_limit_bytes=VMEM_LIMIT),
    )(y, w, h)


def _final_norm_kernel(h_ref, g_ref, o_ref):
    o_ref[...] = _rms_rows(h_ref[...], g_ref[...])


def _final_norm(h, g):
    m, d = h.shape
    return pl.pallas_call(
        _final_norm_kernel,
        grid=(m // ROW_TILE,),
        in_specs=[pl.BlockSpec((ROW_TILE, d), lambda i: (i, 0)),
                  pl.BlockSpec((1, d), lambda i: (0, 0))],
        out_specs=pl.BlockSpec((ROW_TILE, d), lambda i: (i, 0)),
        out_shape=jax.ShapeDtypeStruct((m, d), jnp.float32),
        compiler_params=pltpu.CompilerParams(dimension_semantics=("parallel",)),
    )(h, g.reshape(1, d))


ATT_TILE = 256
HEAD_ROWS = Q_PER_KV * ATT_TILE
SCALE = HEAD_DIM ** -0.5
CMP_BAND_START = LANE - ATT_TILE // CMP_STRIDE
_NT = (((1,), (1,)), ((), ()))


def _bucket_upper_bounds():
    n = np.arange(0, 2 * REL_MAX_DIST)
    ratio = np.log(np.maximum(n, 1).astype(np.float32) / np.float32(REL_MAX_EXACT)) / np.float32(
        math.log(REL_MAX_DIST / REL_MAX_EXACT))
    large = np.minimum(REL_MAX_EXACT + (ratio * np.float32(REL_BUCKETS - REL_MAX_EXACT)).astype(np.int32),
                       REL_BUCKETS - 1)
    bucket = np.where(n < REL_MAX_EXACT, n, large)
    return [int(n[bucket > b].min()) for b in range(REL_BUCKETS - 1)]


_BUCKET_HI = _bucket_upper_bounds()


def _bias_from_dist(dist, head_vals):
    accs = [jnp.zeros(dist.shape, jnp.float32) + vals[REL_BUCKETS - 1] for vals in head_vals]
    for b in range(REL_BUCKETS - 2, -1, -1):
        near = dist < _BUCKET_HI[b]
        accs = [jnp.where(near, vals[b], acc) for vals, acc in zip(head_vals, accs)]
    return accs


def _head_vals(rb_ref, g):
    return [[rb_ref[b, g * Q_PER_KV + r] for b in range(REL_BUCKETS)] for r in range(Q_PER_KV)]


def _build_bias_tiles_t(rb_ref, g, bias_ref):
    key = lax.broadcasted_iota(jnp.int32, (ATT_TILE, ATT_TILE), 0)
    qry = lax.broadcasted_iota(jnp.int32, (ATT_TILE, ATT_TILE), 1)
    vals = _head_vals(rb_ref, g)
    for kind in range(2):
        tiles = _bias_from_dist(qry - key + kind * ATT_TILE, vals)
        for r in range(Q_PER_KV):
            bias_ref[kind, :, r * ATT_TILE:(r + 1) * ATT_TILE] = tiles[r]
    for r in range(Q_PER_KV):
        bias_ref[2, :, r * ATT_TILE:(r + 1) * ATT_TILE] = (
            jnp.zeros((ATT_TILE, ATT_TILE), jnp.float32) + vals[r][REL_BUCKETS - 1])


def _flash_t(qs, k_tile, v_tile_t, bias, mask, state):
    s = lax.dot_general(k_tile, qs, _NT, preferred_element_type=jnp.float32) * SCALE + bias
    if mask is not None:
        s = jnp.where(mask, s, NEG_INF)
    tile_max = jnp.max(s, axis=0, keepdims=True)
    if state is None:
        m_new = tile_max
        p = jnp.exp(s - m_new)
        l = jnp.sum(p, axis=0, keepdims=True)
        acc = jnp.dot(v_tile_t, p.astype(jnp.bfloat16), preferred_element_type=jnp.float32)
    else:
        m, l, acc = state
        m_new = jnp.maximum(m, tile_max)
        alpha = jnp.exp(m - m_new)
        p = jnp.exp(s - m_new)
        l = alpha * l + jnp.sum(p, axis=0, keepdims=True)
        acc = alpha * acc + jnp.dot(v_tile_t, p.astype(jnp.bfloat16), preferred_element_type=jnp.float32)
    return m_new, l, acc


def _kv_tile_t(ref, j):
    return ref[pl.ds(pl.multiple_of(j * ATT_TILE, ATT_TILE), ATT_TILE), :].T.astype(jnp.bfloat16)


def _store_heads_t(o_ref, out_t):
    for r in range(Q_PER_KV):
        o_ref[:, r * HEAD_DIM:(r + 1) * HEAD_DIM] = out_t[:, r * ATT_TILE:(r + 1) * ATT_TILE].T


def _stack_heads(q_ref):
    return jnp.concatenate(
        [q_ref[:, r * HEAD_DIM:(r + 1) * HEAD_DIM] for r in range(Q_PER_KV)], axis=0).astype(jnp.bfloat16)


def _tile4(x):
    return jnp.concatenate([x] * Q_PER_KV, axis=0)


def _flash_first(qs, k_tile, v_tile, bias, mask):
    s = lax.dot_general(qs, k_tile, _NT, preferred_element_type=jnp.float32) * SCALE + bias
    s = jnp.where(mask, s, NEG_INF)
    m = jnp.max(s, axis=-1, keepdims=True)
    p = jnp.exp(s - m)
    l = jnp.sum(p, axis=-1, keepdims=True)
    acc = jnp.dot(p.astype(jnp.bfloat16), v_tile, preferred_element_type=jnp.float32)
    return m, l, acc


def _flash_next(qs, k_tile, v_tile, bias, mask, m, l, acc):
    s = lax.dot_general(qs, k_tile, _NT, preferred_element_type=jnp.float32) * SCALE + bias
    s = jnp.where(mask, s, NEG_INF)
    m_new = jnp.maximum(m, jnp.max(s, axis=-1, keepdims=True))
    alpha = jnp.exp(m - m_new)
    p = jnp.exp(s - m_new)
    l = alpha * l + jnp.sum(p, axis=-1, keepdims=True)
    acc = alpha * acc + jnp.dot(p.astype(jnp.bfloat16), v_tile, preferred_element_type=jnp.float32)
    return m_new, l, acc


def _kv_tile(ref, j):
    return ref[pl.ds(pl.multiple_of(j * ATT_TILE, ATT_TILE), ATT_TILE), :].astype(jnp.bfloat16)


def _rank_select(score, blk, top_k):
    cnt = jnp.zeros(score.shape, jnp.float32)
    for m in range(score.shape[0]):
        row = score[m:m + 1, :]
        beats = jnp.where(row > score, 1.0, jnp.where((row == score) & (blk > m), 1.0, 0.0))
        cnt = cnt + beats
    return cnt < top_k


def _moba_kernel(rb_ref, q_ref, k_ref, v_ref, o_ref, bias_ref, kmean_ref, sel_ref, *, n_tiles):
    g = pl.program_id(1)
    t = pl.program_id(2)

    @pl.when(t == 0)
    def _():
        _build_bias_tiles_t(rb_ref, g, bias_ref)
        kmean_ref[...] = jnp.zeros_like(kmean_ref)
        for n in range(n_tiles):
            kmean_ref[n:n + 1, :] = jnp.mean(k_ref[n * ATT_TILE:(n + 1) * ATT_TILE, :], axis=0, keepdims=True)

    qs = _stack_heads(q_ref)
    gate = lax.dot_general(kmean_ref[...].astype(jnp.bfloat16), qs, _NT, preferred_element_type=jnp.float32)
    blk = lax.broadcasted_iota(jnp.int32, gate.shape, 0)
    past = blk < t
    gate = jnp.where(past, gate, NEG_INF)
    chosen = _rank_select(gate, blk, MOBA_TOPK) & past
    sel_ref[...] = jnp.where(chosen, 1.0, 0.0)

    key = lax.broadcasted_iota(jnp.int32, (ATT_TILE, HEAD_ROWS), 0)
    qry = lax.broadcasted_iota(jnp.int32, (ATT_TILE, HEAD_ROWS), 1) & (ATT_TILE - 1)
    state = _flash_t(qs, _kv_tile(k_ref, t), _kv_tile_t(v_ref, t), bias_ref[0], key <= qry, None)

    def picked(j):
        return sel_ref[pl.ds(j, 1), :] > 0.5

    def far_body(j, state):
        bias = jnp.where(picked(j), bias_ref[2, 0:1, :], NEG_INF)
        return _flash_t(qs, _kv_tile(k_ref, j), _kv_tile_t(v_ref, j), bias, None, state)

    def prev_body(j, state):
        return _flash_t(qs, _kv_tile(k_ref, j), _kv_tile_t(v_ref, j), bias_ref[1], picked(j), state)

    prev = jnp.maximum(t - 1, 0)
    state = lax.fori_loop(0, prev, far_body, state)
    m, l, acc = lax.fori_loop(prev, t, prev_body, state)
    _store_heads_t(o_ref, acc / jnp.maximum(l, 1e-30))


def _moba_attention(proj, rel_bias, n_batch, seq):
    n_tiles = seq // ATT_TILE
    k_col = Q_DIM // HEAD_DIM
    v_col = (Q_DIM + KV_DIM) // HEAD_DIM
    q_cols = Q_PER_KV * HEAD_DIM
    return pl.pallas_call(
        functools.partial(_moba_kernel, n_tiles=n_tiles),
        grid=(n_batch, N_KV_HEADS, n_tiles),
        in_specs=[
            pl.BlockSpec(memory_space=pltpu.SMEM),
            pl.BlockSpec((ATT_TILE, q_cols), lambda b, g, t: (b * n_tiles + t, g)),
            pl.BlockSpec((seq, HEAD_DIM), lambda b, g, t: (b, k_col + g)),
            pl.BlockSpec((seq, HEAD_DIM), lambda b, g, t: (b, v_col + g)),
        ],
        out_specs=pl.BlockSpec((ATT_TILE, q_cols), lambda b, g, t: (b * n_tiles + t, g)),
        out_shape=jax.ShapeDtypeStruct((n_batch * seq, Q_DIM), jnp.float32),
        scratch_shapes=[
            pltpu.VMEM((3, ATT_TILE, HEAD_ROWS), jnp.float32),
            pltpu.VMEM((max(n_tiles, 16), HEAD_DIM), jnp.float32),
            pltpu.VMEM((max(n_tiles, 16), HEAD_ROWS), jnp.float32),
        ],
        compiler_params=pltpu.CompilerParams(
            dimension_semantics=("parallel", "parallel", "arbitrary"),
            vmem_limit_bytes=VMEM_LIMIT),
    )(rel_bias, proj, proj, proj)


def _nsa_compress_kernel(k_ref, v_ref, pe_ref, w_ref, kc_ref, vc_ref, *, n_cmp):
    for c, (x_ref, o_ref) in enumerate(((k_ref, kc_ref), (v_ref, vc_ref))):
        top = jnp.zeros((n_cmp, HEAD_DIM), jnp.float32)
        bot = jnp.zeros((n_cmp, HEAD_DIM), jnp.float32)
        for r in range(CMP_STRIDE):
            x = x_ref[pl.ds(r, n_cmp, stride=CMP_STRIDE), :]
            xt = (x + pe_ref[c, r:r + 1, :]).astype(jnp.bfloat16)
            xb = (x + pe_ref[c, CMP_STRIDE + r:CMP_STRIDE + r + 1, :]).astype(jnp.bfloat16)
            top = top + jnp.dot(xt, w_ref[c, r * HEAD_DIM:(r + 1) * HEAD_DIM, :], preferred_element_type=jnp.float32)
            bot = bot + jnp.dot(xb, w_ref[c, (CMP_STRIDE + r) * HEAD_DIM:(CMP_STRIDE + r + 1) * HEAD_DIM, :],
                                preferred_element_type=jnp.float32)
        o_ref[0, 0] = top + pltpu.roll(bot, n_cmp - 1, axis=0)


def _nsa_compress_prompt(proj, cmp_pe, cmp_w, n_batch, seq):
    n_cmp = seq // CMP_STRIDE
    k_col = Q_DIM // HEAD_DIM
    v_col = (Q_DIM + KV_DIM) // HEAD_DIM
    out = jax.ShapeDtypeStruct((n_batch, N_KV_HEADS, n_cmp, HEAD_DIM), jnp.float32)
    return pl.pallas_call(
        functools.partial(_nsa_compress_kernel, n_cmp=n_cmp),
        grid=(n_batch, N_KV_HEADS),
        in_specs=[
            pl.BlockSpec((seq, HEAD_DIM), lambda b, g: (b, k_col + g)),
            pl.BlockSpec((seq, HEAD_DIM), lambda b, g: (b, v_col + g)),
            pl.BlockSpec((2, CMP_BLOCK, HEAD_DIM), lambda b, g: (0, 0, 0)),
            pl.BlockSpec((2, CMP_BLOCK * HEAD_DIM, HEAD_DIM), lambda b, g: (0, 0, 0)),
        ],
        out_specs=[pl.BlockSpec((1, 1, n_cmp, HEAD_DIM), lambda b, g: (b, g, 0, 0))] * 2,
        out_shape=[out, out],
        compiler_params=pltpu.CompilerParams(
            dimension_semantics=("parallel", "parallel"), vmem_limit_bytes=VMEM_LIMIT),
    )(proj, proj, cmp_pe, cmp_w)


def _nsa_kernel_t(rb_ref, q_ref, gate_ref, kc_ref, vc_ref, ks_ref, vs_ref, kw_ref, vw_ref, ovt_ref, o_ref,
                  bias_ref, cbias_ref, *, n_cmp, n_sel):
    g = pl.program_id(1)
    t = pl.program_id(2)
    tile_tokens = ATT_TILE // CMP_STRIDE
    blocks_per_tile = ATT_TILE // SEL_BLOCK

    @pl.when(t == 0)
    def _():
        _build_bias_tiles_t(rb_ref, g, bias_ref)
        m = lax.broadcasted_iota(jnp.int32, (LANE, ATT_TILE), 0)
        i = lax.broadcasted_iota(jnp.int32, (LANE, ATT_TILE), 1)
        vals = _head_vals(rb_ref, g)
        band = _bias_from_dist(i - CMP_STRIDE * (m - CMP_BAND_START) - (CMP_BLOCK - 1), vals)
        for r in range(Q_PER_KV):
            cols = slice(r * ATT_TILE, (r + 1) * ATT_TILE)
            far = jnp.zeros((n_cmp, ATT_TILE), jnp.float32) + vals[r][REL_BUCKETS - 1]
            cbias_ref[0:n_cmp, cols] = far
            cbias_ref[n_cmp:n_cmp + LANE, cols] = band[r]
            cbias_ref[n_cmp + LANE:2 * n_cmp + LANE, cols] = far

    qs = _stack_heads(q_ref)
    key = lax.broadcasted_iota(jnp.int32, (ATT_TILE, HEAD_ROWS), 0)
    qry = lax.broadcasted_iota(jnp.int32, (ATT_TILE, HEAD_ROWS), 1) & (ATT_TILE - 1)

    start = pl.multiple_of(n_cmp + CMP_BAND_START - t * tile_tokens, tile_tokens)
    c_end = lax.broadcasted_iota(jnp.int32, (n_cmp, HEAD_ROWS), 0) * CMP_STRIDE + (CMP_BLOCK - 1)
    q_pos = t * ATT_TILE + (lax.broadcasted_iota(jnp.int32, (n_cmp, HEAD_ROWS), 1) & (ATT_TILE - 1))
    ok_c = c_end <= q_pos
    lc = lax.dot_general(kc_ref[0, 0].astype(jnp.bfloat16), qs, _NT, preferred_element_type=jnp.float32) * SCALE
    lc = jnp.where(ok_c, lc + cbias_ref[pl.ds(start, n_cmp), :], NEG_INF)
    e = jnp.where(ok_c, jnp.exp(lc - jnp.max(lc, axis=0, keepdims=True)), 0.0)
    pc = e / jnp.maximum(jnp.sum(e, axis=0, keepdims=True), 1e-30)
    o_cmp = jnp.dot(vc_ref[0, 0].T.astype(jnp.bfloat16), pc.astype(jnp.bfloat16), preferred_element_type=jnp.float32)

    p_sum = pc[:, 0:ATT_TILE]
    for r in range(1, Q_PER_KV):
        p_sum = p_sum + pc[:, r * ATT_TILE:(r + 1) * ATT_TILE]
    imp = jnp.dot(ovt_ref[...], p_sum.astype(jnp.bfloat16), preferred_element_type=jnp.float32)[:n_sel]
    blk = lax.broadcasted_iota(jnp.int32, (n_sel, ATT_TILE), 0)
    own = t * blocks_per_tile + (lax.broadcasted_iota(jnp.int32, (n_sel, ATT_TILE), 1) // SEL_BLOCK)
    diff = own - blk
    valid = diff >= 0
    forced = valid & ((blk == 0) | (diff < N_LOCAL_SEL))
    score = jnp.where(forced, FORCE_SCORE, jnp.where(valid, imp, NEG_INF))
    chosen = jnp.where(_rank_select(score, blk, SEL_TOPK) & valid, 1.0, 0.0)
    if n_sel < LANE:
        chosen = jnp.concatenate([chosen, jnp.zeros((LANE - n_sel, ATT_TILE), jnp.float32)], axis=0)
    chosen = chosen.astype(jnp.bfloat16)

    def sel_mask(j):
        pick = jnp.where(
            lax.broadcasted_iota(jnp.int32, (ATT_TILE, LANE), 1)
            == j * blocks_per_tile + lax.broadcasted_iota(jnp.int32, (ATT_TILE, LANE), 0) // SEL_BLOCK,
            1.0, 0.0).astype(jnp.bfloat16)
        return _tile4_lanes(jnp.dot(pick, chosen, preferred_element_type=jnp.float32)) > 0.5

    causal = key <= qry
    state = _flash_t(qs, _kv_tile(ks_ref, t), _kv_tile_t(vs_ref, t), bias_ref[0], sel_mask(t) & causal, None)

    def far_body(j, state):
        return _flash_t(qs, _kv_tile(ks_ref, j), _kv_tile_t(vs_ref, j), bias_ref[2, 0:1, :], sel_mask(j), state)

    def prev_body(j, state):
        return _flash_t(qs, _kv_tile(ks_ref, j), _kv_tile_t(vs_ref, j), bias_ref[1], sel_mask(j), state)

    prev = jnp.maximum(t - 1, 0)
    state = lax.fori_loop(0, prev, far_body, state)
    m, l, acc = lax.fori_loop(prev, t, prev_body, state)
    o_sel = acc / jnp.maximum(l, 1e-30)

    state = _flash_t(qs, _kv_tile(kw_ref, t), _kv_tile_t(vw_ref, t), bias_ref[0], causal, None)
    state = _flash_t(qs, _kv_tile(kw_ref, prev), _kv_tile_t(vw_ref, prev), bias_ref[1], (key >= 0) & (t >= 1), state)
    t2 = jnp.maximum(t - 2, 0)
    m, l, acc = _flash_t(qs, _kv_tile(kw_ref, t2), _kv_tile_t(vw_ref, t2), bias_ref[2, 0:1, :],
                         (key > qry) & (t >= 2), state)
    o_win = acc / jnp.maximum(l, 1e-30)

    gates = jax.nn.sigmoid(gate_ref[...]).T
    gate_row = lax.broadcasted_iota(jnp.int32, gates.shape, 0)
    for r in range(Q_PER_KV):
        cols = slice(r * ATT_TILE, (r + 1) * ATT_TILE)
        out = jnp.zeros((HEAD_DIM, ATT_TILE), jnp.float32)
        for c, branch in enumerate((o_cmp, o_sel, o_win)):
            gvec = jnp.sum(jnp.where(gate_row == g * (3 * Q_PER_KV) + r * 3 + c, gates, 0.0), axis=0, keepdims=True)
            out = out + gvec * branch[:, cols]
        o_ref[:, r * HEAD_DIM:(r + 1) * HEAD_DIM] = out.T


def _tile4_lanes(x):
    return jnp.concatenate([x] * Q_PER_KV, axis=1)


def _nsa_overlap_t(n_cmp):
    ci = np.arange(n_cmp)[None, :]
    sj = np.arange(LANE)[:, None]
    ov = (ci * CMP_STRIDE < (sj + 1) * SEL_BLOCK) & (ci * CMP_STRIDE + CMP_BLOCK > sj * SEL_BLOCK) & (ci < n_cmp - 1)
    return jnp.asarray(ov, jnp.bfloat16)


def _nsa_attention(proj, kc, vc, rel_bias, n_batch, seq):
    n_tiles = seq // ATT_TILE
    n_cmp = seq // CMP_STRIDE
    q_cols = Q_PER_KV * HEAD_DIM
    kv0 = Q_DIM // HEAD_DIM
    per = KV_DIM // HEAD_DIM
    gate_col = (Q_DIM + 6 * KV_DIM) // LANE

    def kv_spec(comp):
        return pl.BlockSpec((seq, HEAD_DIM), lambda b, g, t: (b, kv0 + comp * per + g))

    cmp_spec = pl.BlockSpec((1, 1, n_cmp, HEAD_DIM), lambda b, g, t: (b, g, 0, 0))
    return pl.pallas_call(
        functools.partial(_nsa_kernel_t, n_cmp=n_cmp, n_sel=seq // SEL_BLOCK),
        grid=(n_batch, N_KV_HEADS, n_tiles),
        in_specs=[
            pl.BlockSpec(memory_space=pltpu.SMEM),
            pl.BlockSpec((ATT_TILE, q_cols), lambda b, g, t: (b * n_tiles + t, g)),
            pl.BlockSpec((ATT_TILE, LANE), lambda b, g, t: (b * n_tiles + t, gate_col)),
            cmp_spec, cmp_spec,
            kv_spec(2), kv_spec(3), kv_spec(4), kv_spec(5),
            pl.BlockSpec((LANE, n_cmp), lambda b, g, t: (0, 0)),
        ],
        out_specs=pl.BlockSpec((ATT_TILE, q_cols), lambda b, g, t: (b * n_tiles + t, g)),
        out_shape=jax.ShapeDtypeStruct((n_batch * seq, Q_DIM), jnp.float32),
        scratch_shapes=[pltpu.VMEM((3, ATT_TILE, HEAD_ROWS), jnp.float32),
                        pltpu.VMEM((2 * n_cmp + LANE, HEAD_ROWS), jnp.float32)],
        compiler_params=pltpu.CompilerParams(
            dimension_semantics=("parallel", "parallel", "arbitrary"),
            vmem_limit_bytes=VMEM_LIMIT),
    )(rel_bias, proj, proj, kc, vc, proj, proj, proj, proj, _nsa_overlap_t(n_cmp))


NEW_ROWS = Q_PER_KV * DEC_SEQ


def _rank_select_rows(score, n_cand, top_k):
    lane = lax.broadcasted_iota(jnp.int32, score.shape, 1)
    cnt = jnp.zeros(score.shape, jnp.float32)
    for m in range(n_cand):
        col = score[:, m:m + 1]
        cnt = cnt + jnp.where(col > score, 1.0, jnp.where((col == score) & (lane > m), 1.0, 0.0))
    return cnt < top_k


def _stack_group_heads(q_ref, g):
    return jnp.concatenate(
        [q_ref[:, (g * Q_PER_KV + r) * HEAD_DIM:(g * Q_PER_KV + r + 1) * HEAD_DIM] for r in range(Q_PER_KV)],
        axis=0).astype(jnp.bfloat16)


def _group_bias(rb_ref, g, dist):
    return jnp.concatenate(_bias_from_dist(dist, _head_vals(rb_ref, g)), axis=0)


def _flash_update(refs, g, qs, k_tile, v_tile, bias, mask, first):
    m_ref, l_ref, acc_ref = refs
    if first:
        m, l, acc = _flash_first(qs, k_tile, v_tile, bias, mask)
    else:
        m, l, acc = _flash_next(qs, k_tile, v_tile, bias, mask, m_ref[g], l_ref[g], acc_ref[g])
    m_ref[g] = m
    l_ref[g] = l
    acc_ref[g] = acc


PAGES_PER_BLOCK = MOBA_BLOCK // PAGE_SIZE
KMEAN_PAGES = 8
MOBA_STEP_PAGES = 4
MOBA_STEP_KEYS = MOBA_STEP_PAGES * PAGE_SIZE


def _moba_kmean_kernel(pt_ref, *refs):
    o_ref = refs[KMEAN_PAGES]
    for i in range(KMEAN_PAGES // PAGES_PER_BLOCK):
        total = jnp.sum(refs[PAGES_PER_BLOCK * i][...], axis=0)
        for k in range(1, PAGES_PER_BLOCK):
            total = total + jnp.sum(refs[PAGES_PER_BLOCK * i + k][...], axis=0)
        o_ref[i] = total * (1.0 / MOBA_BLOCK)


def _page_spec(comp, which, per_step):
    return pl.BlockSpec((None, PAGE_SIZE, None, N_KV_HEADS, HEAD_DIM),
                        lambda b, n, pt: (pt[b, n * per_step + which], 0, comp, 0, 0))


def _moba_sample_kmean(cache, page_table):
    n_seq, n_pages = page_table.shape
    n_blk = n_pages // PAGES_PER_BLOCK
    blk_per_step = KMEAN_PAGES // PAGES_PER_BLOCK
    return pl.pallas_call(
        _moba_kmean_kernel,
        grid_spec=pltpu.PrefetchScalarGridSpec(
            num_scalar_prefetch=1,
            grid=(n_seq, n_pages // KMEAN_PAGES),
            in_specs=[_page_spec(0, k, KMEAN_PAGES) for k in range(KMEAN_PAGES)],
            out_specs=pl.BlockSpec((None, blk_per_step, N_KV_HEADS, HEAD_DIM), lambda b, n, pt: (b, n, 0, 0)),
        ),
        out_shape=jax.ShapeDtypeStruct((n_seq, n_blk, N_KV_HEADS, HEAD_DIM), jnp.float32),
        compiler_params=pltpu.CompilerParams(dimension_semantics=("parallel", "arbitrary")),
    )(page_table, *([cache] * KMEAN_PAGES))


def _moba_sample_kernel(pt_ref, rb_ref, q_ref, kn_ref, vn_ref, kmean_ref, *refs, n_blk, n_steps, past_len):
    k_refs = refs[:MOBA_STEP_PAGES]
    v_refs = refs[MOBA_STEP_PAGES:2 * MOBA_STEP_PAGES]
    o_ref, m_ref, l_ref, acc_ref, sel_ref = refs[2 * MOBA_STEP_PAGES:]
    n = pl.program_id(1)
    refs = (m_ref, l_ref, acc_ref)
    tok = lax.broadcasted_iota(jnp.int32, (DEC_SEQ, DEC_SEQ), 0)
    new = lax.broadcasted_iota(jnp.int32, (DEC_SEQ, DEC_SEQ), 1)

    @pl.when(n == 0)
    def _():
        for g in range(N_KV_HEADS):
            qs = _stack_group_heads(q_ref, g)
            gate = lax.dot_general(qs, kmean_ref[g].astype(jnp.bfloat16), _NT, preferred_element_type=jnp.float32)
            gate = jnp.concatenate([gate, jnp.full((NEW_ROWS, LANE - n_blk), NEG_INF, jnp.float32)], axis=1)
            lane = lax.broadcasted_iota(jnp.int32, gate.shape, 1)
            chosen = _rank_select_rows(gate, n_blk, MOBA_TOPK) & (lane < n_blk)
            sel_ref[g] = jnp.where(chosen, 1.0, 0.0)
            _flash_update(refs, g, qs, kn_ref[:, g * HEAD_DIM:(g + 1) * HEAD_DIM].astype(jnp.bfloat16),
                          vn_ref[:, g * HEAD_DIM:(g + 1) * HEAD_DIM].astype(jnp.bfloat16),
                          _group_bias(rb_ref, g, tok - new), _tile4(new <= tok), True)

    q_pos = past_len + lax.broadcasted_iota(jnp.int32, (DEC_SEQ, MOBA_STEP_KEYS), 0)
    k_pos = n * MOBA_STEP_KEYS + lax.broadcasted_iota(jnp.int32, (DEC_SEQ, MOBA_STEP_KEYS), 1)
    pick = jnp.where(
        lax.broadcasted_iota(jnp.int32, (LANE, MOBA_STEP_KEYS), 0)
        == n * (MOBA_STEP_KEYS // MOBA_BLOCK) + lax.broadcasted_iota(jnp.int32, (LANE, MOBA_STEP_KEYS), 1) // MOBA_BLOCK,
        1.0, 0.0).astype(jnp.bfloat16)
    for g in range(N_KV_HEADS):
        qs = _stack_group_heads(q_ref, g)
        k_tile = jnp.concatenate([ref[:, g, :] for ref in k_refs], axis=0).astype(jnp.bfloat16)
        v_tile = jnp.concatenate([ref[:, g, :] for ref in v_refs], axis=0).astype(jnp.bfloat16)
        mask = jnp.dot(sel_ref[g].astype(jnp.bfloat16), pick, preferred_element_type=jnp.float32) > 0.5
        _flash_update(refs, g, qs, k_tile, v_tile, _group_bias(rb_ref, g, q_pos - k_pos), mask, False)

    @pl.when(n == n_steps - 1)
    def _():
        for g in range(N_KV_HEADS):
            out = acc_ref[g] / jnp.maximum(l_ref[g], 1e-30)
            for r in range(Q_PER_KV):
                h = g * Q_PER_KV + r
                o_ref[:, h * HEAD_DIM:(h + 1) * HEAD_DIM] = out[r * DEC_SEQ:(r + 1) * DEC_SEQ]


def _moba_sample_attention(proj, kmean, cache, page_table, rel_bias, row_block0, past_len):
    n_seq, n_pages = page_table.shape
    n_blk = n_pages // PAGES_PER_BLOCK
    n_steps = n_pages // MOBA_STEP_PAGES
    kv_cols = KV_DIM
    return pl.pallas_call(
        functools.partial(_moba_sample_kernel, n_blk=n_blk, n_steps=n_steps, past_len=past_len),
        grid_spec=pltpu.PrefetchScalarGridSpec(
            num_scalar_prefetch=1,
            grid=(n_seq, n_steps),
            in_specs=[
                pl.BlockSpec(memory_space=pltpu.SMEM),
                pl.BlockSpec((DEC_SEQ, Q_DIM), lambda b, n, pt: (row_block0 + b, 0)),
                pl.BlockSpec((DEC_SEQ, kv_cols), lambda b, n, pt: (row_block0 + b, Q_DIM // kv_cols)),
                pl.BlockSpec((DEC_SEQ, kv_cols), lambda b, n, pt: (row_block0 + b, Q_DIM // kv_cols + 1)),
                pl.BlockSpec((None, N_KV_HEADS, n_blk, HEAD_DIM), lambda b, n, pt: (b, 0, 0, 0)),
            ] + [_page_spec(comp, k, MOBA_STEP_PAGES) for comp in range(2) for k in range(MOBA_STEP_PAGES)],
            out_specs=pl.BlockSpec((DEC_SEQ, Q_DIM), lambda b, n, pt: (b, 0)),
            scratch_shapes=[
                pltpu.VMEM((N_KV_HEADS, NEW_ROWS, 1), jnp.float32),
                pltpu.VMEM((N_KV_HEADS, NEW_ROWS, 1), jnp.float32),
                pltpu.VMEM((N_KV_HEADS, NEW_ROWS, HEAD_DIM), jnp.float32),
                pltpu.VMEM((N_KV_HEADS, NEW_ROWS, LANE), jnp.float32),
            ],
        ),
        out_shape=jax.ShapeDtypeStruct((n_seq * DEC_SEQ, Q_DIM), jnp.float32),
        compiler_params=pltpu.CompilerParams(dimension_semantics=("parallel", "arbitrary")),
    )(page_table, rel_bias, proj, proj, proj, kmean, *([cache] * (2 * MOBA_STEP_PAGES)))


CMP_PAGES = 8
SEL_STEP_PAGES = 4
SEL_STEP_KEYS = SEL_STEP_PAGES * PAGE_SIZE
CHUNKS_PER_PAGE = PAGE_SIZE // CMP_STRIDE


def _nsa_sample_compress_kernel(pt_ref, *refs):
    page_refs = refs[:CMP_PAGES]
    pe_ref, w_ref, top_ref, bot_ref = refs[CMP_PAGES:]
    rows = CMP_PAGES * CHUNKS_PER_PAGE
    for c in range(2):
        top = jnp.zeros((N_KV_HEADS * rows, HEAD_DIM), jnp.float32)
        bot = jnp.zeros((N_KV_HEADS * rows, HEAD_DIM), jnp.float32)
        for r in range(CMP_STRIDE):
            x = jnp.concatenate(
                [ref[pl.ds(r, CHUNKS_PER_PAGE, stride=CMP_STRIDE), c, g, :]
                 for g in range(N_KV_HEADS) for ref in page_refs], axis=0)
            xt = (x + pe_ref[c, r:r + 1, :]).astype(jnp.bfloat16)
            xb = (x + pe_ref[c, CMP_STRIDE + r:CMP_STRIDE + r + 1, :]).astype(jnp.bfloat16)
            top = top + jnp.dot(xt, w_ref[c, r * HEAD_DIM:(r + 1) * HEAD_DIM, :], preferred_element_type=jnp.float32)
            bot = bot + jnp.dot(xb, w_ref[c, (CMP_STRIDE + r) * HEAD_DIM:(CMP_STRIDE + r + 1) * HEAD_DIM, :],
                                preferred_element_type=jnp.float32)
        for g in range(N_KV_HEADS):
            top_ref[c, g] = top[g * rows:(g + 1) * rows]
            bot_ref[c, g] = bot[g * rows:(g + 1) * rows]


def _nsa_sample_compress(cache, page_table, cmp_pe, cmp_w):
    n_seq, n_pages = page_table.shape
    n_chunks = n_pages * CHUNKS_PER_PAGE
    rows = CMP_PAGES * CHUNKS_PER_PAGE

    def page_spec(k):
        return pl.BlockSpec((None, PAGE_SIZE, 2, N_KV_HEADS, HEAD_DIM),
                            lambda b, s, pt: (pt[b, s * CMP_PAGES + k], 0, 0, 0, 0))

    out = jax.ShapeDtypeStruct((n_seq, 2, N_KV_HEADS, n_chunks, HEAD_DIM), jnp.float32)
    out_spec = pl.BlockSpec((None, 2, N_KV_HEADS, rows, HEAD_DIM), lambda b, s, pt: (b, 0, 0, s, 0))
    return pl.pallas_call(
        _nsa_sample_compress_kernel,
        grid_spec=pltpu.PrefetchScalarGridSpec(
            num_scalar_prefetch=1,
            grid=(n_seq, n_pages // CMP_PAGES),
            in_specs=[page_spec(k) for k in range(CMP_PAGES)] + [
                pl.BlockSpec((2, CMP_BLOCK, HEAD_DIM), lambda b, s, pt: (0, 0, 0)),
                pl.BlockSpec((2, CMP_BLOCK * HEAD_DIM, HEAD_DIM), lambda b, s, pt: (0, 0, 0)),
            ],
            out_specs=[out_spec, out_spec],
        ),
        out_shape=[out, out],
        compiler_params=pltpu.CompilerParams(
            dimension_semantics=("parallel", "parallel"), vmem_limit_bytes=VMEM_LIMIT),
    )(page_table, *([cache] * CMP_PAGES), cmp_pe, cmp_w)


def _nsa_sample_kernel(pt_ref, rb_ref, q_ref, gate_ref, ksn_ref, vsn_ref, kwn_ref, vwn_ref, top_ref, bot_ref,
                       ov_ref, win_ref, *refs, n_pages, past_len):
    page_refs = refs[:SEL_STEP_PAGES]
    o_ref, m_ref, l_ref, acc_ref, sel_ref, ocmp_ref, owin_ref = refs[SEL_STEP_PAGES:]
    p = pl.program_id(1)
    refs = (m_ref, l_ref, acc_ref)
    n_chunks = n_pages * CHUNKS_PER_PAGE
    n_sel_pad = ov_ref.shape[1]
    n_sel = past_len // SEL_BLOCK + 1
    win_len = win_ref.shape[0]
    tok = lax.broadcasted_iota(jnp.int32, (DEC_SEQ, DEC_SEQ), 0)
    new = lax.broadcasted_iota(jnp.int32, (DEC_SEQ, DEC_SEQ), 1)

    def group_cols(ref, g):
        return ref[:, g * HEAD_DIM:(g + 1) * HEAD_DIM].astype(jnp.bfloat16)

    @pl.when(p == 0)
    def _():
        for g in range(N_KV_HEADS):
            qs = _stack_group_heads(q_ref, g)
            kc = top_ref[0, g] + pltpu.roll(bot_ref[0, g], n_chunks - 1, axis=0)
            vc = top_ref[1, g] + pltpu.roll(bot_ref[1, g], n_chunks - 1, axis=0)
            q_pos = past_len + lax.broadcasted_iota(jnp.int32, (DEC_SEQ, n_chunks), 0)
            tok_id = lax.broadcasted_iota(jnp.int32, (DEC_SEQ, n_chunks), 1)
            c_end = tok_id * CMP_STRIDE + (CMP_BLOCK - 1)
            ok_c = _tile4((tok_id < n_chunks - 1) & (c_end <= q_pos))
            lc = lax.dot_general(qs, kc.astype(jnp.bfloat16), _NT, preferred_element_type=jnp.float32) * SCALE
            lc = jnp.where(ok_c, lc + _group_bias(rb_ref, g, q_pos - c_end), NEG_INF)
            e = jnp.where(ok_c, jnp.exp(lc - jnp.max(lc, axis=-1, keepdims=True)), 0.0)
            pc = e / jnp.maximum(jnp.sum(e, axis=-1, keepdims=True), 1e-30)
            ocmp_ref[g] = jnp.dot(pc.astype(jnp.bfloat16), vc.astype(jnp.bfloat16), preferred_element_type=jnp.float32)
            p_sum = pc[0:DEC_SEQ]
            for r in range(1, Q_PER_KV):
                p_sum = p_sum + pc[r * DEC_SEQ:(r + 1) * DEC_SEQ]
            imp = jnp.dot(p_sum.astype(jnp.bfloat16), ov_ref[...], preferred_element_type=jnp.float32)
            blk = lax.broadcasted_iota(jnp.int32, (DEC_SEQ, n_sel_pad), 1)
            own = (past_len + lax.broadcasted_iota(jnp.int32, (DEC_SEQ, n_sel_pad), 0)) // SEL_BLOCK
            diff = own - blk
            valid = diff >= 0
            forced = valid & ((blk == 0) | (diff < N_LOCAL_SEL))
            score = jnp.where(forced, FORCE_SCORE, jnp.where(valid, imp, NEG_INF))
            chosen = _rank_select_rows(score, n_sel, SEL_TOPK) & valid
            sel_ref[g] = _tile4(jnp.where(chosen, 1.0, 0.0))
            causal = _tile4(new <= tok)
            near = _group_bias(rb_ref, g, tok - new)
            _flash_update(refs, g, qs, group_cols(ksn_ref, g), group_cols(vsn_ref, g), near, causal, True)
            mw, lw, aw = _flash_first(qs, group_cols(kwn_ref, g), group_cols(vwn_ref, g), near, causal)
            tok_w = lax.broadcasted_iota(jnp.int32, (DEC_SEQ, win_len), 0)
            dist_w = win_len + tok_w - lax.broadcasted_iota(jnp.int32, (DEC_SEQ, win_len), 1)
            mw, lw, aw = _flash_next(qs, win_ref[:, 0, g, :].astype(jnp.bfloat16), win_ref[:, 1, g, :].astype(jnp.bfloat16),
                                     _group_bias(rb_ref, g, dist_w), _tile4(dist_w < WINDOW), mw, lw, aw)
            owin_ref[g] = aw / jnp.maximum(lw, 1e-30)

    q_pos = past_len + lax.broadcasted_iota(jnp.int32, (DEC_SEQ, SEL_STEP_KEYS), 0)
    k_pos = p * SEL_STEP_KEYS + lax.broadcasted_iota(jnp.int32, (DEC_SEQ, SEL_STEP_KEYS), 1)
    pick = jnp.where(
        lax.broadcasted_iota(jnp.int32, (n_sel_pad, SEL_STEP_KEYS), 0)
        == p * (SEL_STEP_KEYS // SEL_BLOCK) + lax.broadcasted_iota(jnp.int32, (n_sel_pad, SEL_STEP_KEYS), 1) // SEL_BLOCK,
        1.0, 0.0).astype(jnp.bfloat16)
    for g in range(N_KV_HEADS):
        qs = _stack_group_heads(q_ref, g)
        k_tile = jnp.concatenate([ref[:, 0, g, :] for ref in page_refs], axis=0).astype(jnp.bfloat16)
        v_tile = jnp.concatenate([ref[:, 1, g, :] for ref in page_refs], axis=0).astype(jnp.bfloat16)
        mask = jnp.dot(sel_ref[g].astype(jnp.bfloat16), pick, preferred_element_type=jnp.float32) > 0.5
        _flash_update(refs, g, qs, k_tile, v_tile, _group_bias(rb_ref, g, q_pos - k_pos), mask, False)

    @pl.when(p == n_pages // SEL_STEP_PAGES - 1)
    def _():
        gates = jax.nn.sigmoid(gate_ref[...])
        for g in range(N_KV_HEADS):
            o_sel = acc_ref[g] / jnp.maximum(l_ref[g], 1e-30)
            for r in range(Q_PER_KV):
                rows = slice(r * DEC_SEQ, (r + 1) * DEC_SEQ)
                col = (g * Q_PER_KV + r) * 3
                out = (gates[:, col:col + 1] * ocmp_ref[g][rows] + gates[:, col + 1:col + 2] * o_sel[rows]
                       + gates[:, col + 2:col + 3] * owin_ref[g][rows])
                h = g * Q_PER_KV + r
                o_ref[:, h * HEAD_DIM:(h + 1) * HEAD_DIM] = out


def _nsa_sample_overlap(n_chunks, n_sel_pad):
    ci = np.arange(n_chunks)[:, None]
    sj = np.arange(n_sel_pad)[None, :]
    ov = (ci * CMP_STRIDE < (sj + 1) * SEL_BLOCK) & (ci * CMP_STRIDE + CMP_BLOCK > sj * SEL_BLOCK) & (ci < n_chunks - 1)
    return jnp.asarray(ov, jnp.bfloat16)


def _nsa_sample(proj, cache, win_buf, page_table, cmp_pe, cmp_w, rel_bias, row_block0, past_len):
    n_seq, n_pages = page_table.shape
    n_chunks = n_pages * CHUNKS_PER_PAGE
    n_sel_pad = -(-(past_len // SEL_BLOCK + 1) // LANE) * LANE
    top, bot = _nsa_sample_compress(cache, page_table, cmp_pe, cmp_w)
    kv0 = Q_DIM // KV_DIM
    gate_col = (Q_DIM + 6 * KV_DIM) // LANE

    def new_spec(comp):
        return pl.BlockSpec((DEC_SEQ, KV_DIM), lambda b, p, pt: (row_block0 + b, kv0 + comp))

    def sel_page_spec(k):
        return pl.BlockSpec((None, PAGE_SIZE, 2, N_KV_HEADS, HEAD_DIM),
                            lambda b, p, pt: (pt[b, p * SEL_STEP_PAGES + k], 0, 1, 0, 0))

    cmp_spec = pl.BlockSpec((None, 2, N_KV_HEADS, n_chunks, HEAD_DIM), lambda b, p, pt: (b, 0, 0, 0, 0))
    scratch = [
        pltpu.VMEM((N_KV_HEADS, NEW_ROWS, 1), jnp.float32),
        pltpu.VMEM((N_KV_HEADS, NEW_ROWS, 1), jnp.float32),
        pltpu.VMEM((N_KV_HEADS, NEW_ROWS, HEAD_DIM), jnp.float32),
        pltpu.VMEM((N_KV_HEADS, NEW_ROWS, n_sel_pad), jnp.float32),
        pltpu.VMEM((N_KV_HEADS, NEW_ROWS, HEAD_DIM), jnp.float32),
        pltpu.VMEM((N_KV_HEADS, NEW_ROWS, HEAD_DIM), jnp.float32),
    ]
    return pl.pallas_call(
        functools.partial(_nsa_sample_kernel, n_pages=n_pages, past_len=past_len),
        grid_spec=pltpu.PrefetchScalarGridSpec(
            num_scalar_prefetch=1,
            grid=(n_seq, n_pages // SEL_STEP_PAGES),
            in_specs=[
                pl.BlockSpec(memory_space=pltpu.SMEM),
                pl.BlockSpec((DEC_SEQ, Q_DIM), lambda b, p, pt: (row_block0 + b, 0)),
                pl.BlockSpec((DEC_SEQ, LANE), lambda b, p, pt: (row_block0 + b, gate_col)),
                new_spec(2), new_spec(3), new_spec(4), new_spec(5),
                cmp_spec, cmp_spec,
                pl.BlockSpec((n_chunks, n_sel_pad), lambda b, p, pt: (0, 0)),
                pl.BlockSpec((None,) + win_buf.shape[1:], lambda b, p, pt: (b, 0, 0, 0, 0)),
            ] + [sel_page_spec(k) for k in range(SEL_STEP_PAGES)],
            out_specs=pl.BlockSpec((DEC_SEQ, Q_DIM), lambda b, p, pt: (b, 0)),
            scratch_shapes=scratch,
        ),
        out_shape=jax.ShapeDtypeStruct((n_seq * DEC_SEQ, Q_DIM), jnp.float32),
        compiler_params=pltpu.CompilerParams(
            dimension_semantics=("parallel", "arbitrary"), vmem_limit_bytes=VMEM_LIMIT),
    )(page_table, rel_bias, proj, proj, proj, proj, proj, proj, top, bot,
      _nsa_sample_overlap(n_chunks, n_sel_pad), win_buf, *([cache] * SEL_STEP_PAGES))


SSD_PAIR = 2 * SSD_HEAD_DIM
GROUP_COLS = HEADS_PER_GROUP * SSD_HEAD_DIM
CARRY_ROWS = 8


def _pad_rows(x, rows):
    if x.shape[0] == rows:
        return x
    return jnp.concatenate([x, jnp.zeros((rows - x.shape[0],) + x.shape[1:], x.dtype)], axis=0)


def _ssd_kernel(z_ref, x_ref, b_ref, c_ref, dt_ref, prev_ref, s0_ref, cw_ref, cb_ref, dtb_ref, alog_ref, dsk_ref,
                nw_ref, y_ref, conv_ref, state_ref, carry_ref, s_ref, act_ref, *, rows_in, n_chunks):
    c = pl.program_id(1)
    L = SSD_CHUNK

    @pl.when(c == 0)
    def _():
        carry_ref[...] = prev_ref[...]
        s_ref[...] = s0_ref[...]

    row8 = lax.broadcasted_iota(jnp.int32, (CARRY_ROWS, 1), 0)
    off = 0
    for ref in (x_ref, b_ref, c_ref):
        width = ref.shape[1]
        cols = slice(off, off + width)
        raw_in = ref[...]
        raw = _pad_rows(raw_in, L)
        prev = carry_ref[:, cols]
        acc = cb_ref[:, cols] + cw_ref[CONV_W - 1:CONV_W, cols] * raw
        for k in range(1, CONV_W):
            rolled = pltpu.roll(raw, k, axis=0)
            head = jnp.where(row8 < k, pltpu.roll(prev, k, axis=0), rolled[:CARRY_ROWS])
            shifted = jnp.concatenate([head, rolled[CARRY_ROWS:]], axis=0)
            acc = acc + cw_ref[CONV_W - 1 - k:CONV_W - k, cols] * shifted
        act_ref[:, cols] = acc * jax.nn.sigmoid(acc)
        carry_ref[:, cols] = raw_in[rows_in - CARRY_ROWS:rows_in]

        @pl.when(c == n_chunks - 1)
        def _():
            conv_ref[:, cols] = raw_in[rows_in - (CONV_W - 1):rows_in]

        off += width

    row = lax.broadcasted_iota(jnp.int32, (L, LANE), 0)
    pre = _pad_rows(dt_ref[...], L) + dtb_ref[...]
    dt = jnp.maximum(pre, 0.0) + jnp.log1p(jnp.exp(-jnp.abs(pre)))
    dt = jnp.where(row < rows_in, dt, 0.0)
    acum = dt * (-jnp.exp(alog_ref[...]))
    shift = 1
    while shift < L:
        acum = acum + jnp.where(row >= shift, pltpu.roll(acum, shift, axis=0), 0.0)
        shift *= 2
    acum_t = acum.T
    exp_a = jnp.exp(acum)
    to_end = jnp.exp(acum[L - 1:L, :] - acum)
    li = lax.broadcasted_iota(jnp.int32, (L, L), 0)
    si = lax.broadcasted_iota(jnp.int32, (L, L), 1)
    causal = li >= si
    first_half = lax.broadcasted_iota(jnp.int32, (L, SSD_PAIR), 1) < SSD_HEAD_DIM

    def pair_cols(v, h):
        return jnp.where(first_half, v[:, h:h + 1], v[:, h + 1:h + 2])

    for g in range(SSD_GROUPS):
        bg = act_ref[:, D_INNER + g * D_STATE:D_INNER + (g + 1) * D_STATE].astype(jnp.bfloat16)
        cg = act_ref[:, D_INNER + GN + g * D_STATE:D_INNER + GN + (g + 1) * D_STATE].astype(jnp.bfloat16)
        cb = lax.dot_general(cg, bg, _NT, preferred_element_type=jnp.float32)
        s_prev = s_ref[g * GROUP_COLS:(g + 1) * GROUP_COLS, :]
        y_off = lax.dot_general(cg, s_prev.astype(jnp.bfloat16), _NT, preferred_element_type=jnp.float32)
        xw_parts = []
        for pr in range(HEADS_PER_GROUP // 2):
            h = g * HEADS_PER_GROUP + 2 * pr
            cols = slice(h * SSD_HEAD_DIM, (h + 2) * SSD_HEAD_DIM)
            xa = act_ref[:, cols]
            xdt = xa * pair_cols(dt, h)
            y = y_off[:, 2 * pr * SSD_HEAD_DIM:(2 * pr + 2) * SSD_HEAD_DIM] * pair_cols(exp_a, h)
            for half in range(2):
                hh = h + half
                decay = jnp.where(causal, jnp.exp(acum[:, hh:hh + 1] - acum_t[hh:hh + 1, :]), 0.0)
                mine = first_half if half == 0 else jnp.logical_not(first_half)
                y = y + jnp.dot((cb * decay).astype(jnp.bfloat16), jnp.where(mine, xdt, 0.0).astype(jnp.bfloat16),
                                preferred_element_type=jnp.float32)
            zz = z_ref[:, cols]
            y = y[:rows_in] + xa[:rows_in] * dsk_ref[:, cols]
            y_ref[:, cols] = y * (zz * jax.nn.sigmoid(zz))
            xw_parts.append(xdt * pair_cols(to_end, h))
        xw = jnp.concatenate(xw_parts, axis=1)
        chunk_state = jnp.dot(xw.T.astype(jnp.bfloat16), bg, preferred_element_type=jnp.float32)
        for r in range(HEADS_PER_GROUP):
            h = g * HEADS_PER_GROUP + r
            rows = slice(h * SSD_HEAD_DIM, (h + 1) * SSD_HEAD_DIM)
            s_ref[rows, :] = (s_ref[rows, :] * jnp.exp(acum_t[h:h + 1, L - 1:L])
                              + chunk_state[r * SSD_HEAD_DIM:(r + 1) * SSD_HEAD_DIM])
        gcols = slice(g * GROUP_COLS, (g + 1) * GROUP_COLS)
        yg = y_ref[:, gcols]
        y_ref[:, gcols] = yg * lax.rsqrt(jnp.mean(yg * yg, axis=-1, keepdims=True) + RMS_EPS) * nw_ref[:, gcols]

    @pl.when(c == n_chunks - 1)
    def _():
        state_ref[...] = s_ref[...]


def _ssd_mix(proj, conv_prev, ssm_prev, conv_w, conv_b, dt_bias, a_log, d_skip, norm_w, row_block0, rows_in, n_chunks):
    n_seq = conv_prev.shape[0]

    def rows_spec(width, col_block):
        return pl.BlockSpec((rows_in, width), lambda b, c: (row_block0 + b * n_chunks + c, col_block))

    def const_spec(shape):
        return pl.BlockSpec(shape, lambda b, c: (0,) * len(shape))

    pad_lanes = lambda v: jnp.pad(v, (0, LANE - v.shape[0])).reshape(1, LANE)
    return pl.pallas_call(
        functools.partial(_ssd_kernel, rows_in=rows_in, n_chunks=n_chunks),
        grid=(n_seq, n_chunks),
        in_specs=[
            rows_spec(D_INNER, 0),
            rows_spec(D_INNER, 1),
            rows_spec(GN, 2 * D_INNER // GN),
            rows_spec(GN, 2 * D_INNER // GN + 1),
            rows_spec(LANE, (D_INNER + CONV_DIM) // LANE),
            pl.BlockSpec((None, CARRY_ROWS, CONV_DIM), lambda b, c: (b, 0, 0)),
            pl.BlockSpec((None, D_INNER, D_STATE), lambda b, c: (b, 0, 0)),
            const_spec((CONV_W, CONV_DIM)),
            const_spec((1, CONV_DIM)),
            const_spec((1, LANE)),
            const_spec((1, LANE)),
            const_spec((1, D_INNER)),
            const_spec((1, D_INNER)),
        ],
        out_specs=[
            pl.BlockSpec((rows_in, D_INNER), lambda b, c: (b * n_chunks + c, 0)),
            pl.BlockSpec((None, CONV_W - 1, CONV_DIM), lambda b, c: (b, 0, 0)),
            pl.BlockSpec((None, D_INNER, D_STATE), lambda b, c: (b, 0, 0)),
        ],
        out_shape=[
            jax.ShapeDtypeStruct((n_seq * n_chunks * rows_in, D_INNER), jnp.float32),
            jax.ShapeDtypeStruct((n_seq, CONV_W - 1, CONV_DIM), jnp.float32),
            jax.ShapeDtypeStruct((n_seq, D_INNER, D_STATE), jnp.float32),
        ],
        scratch_shapes=[
            pltpu.VMEM((CARRY_ROWS, CONV_DIM), jnp.float32),
            pltpu.VMEM((D_INNER, D_STATE), jnp.float32),
            pltpu.VMEM((SSD_CHUNK, CONV_DIM), jnp.float32),
        ],
        compiler_params=pltpu.CompilerParams(
            dimension_semantics=("parallel", "arbitrary"), vmem_limit_bytes=VMEM_LIMIT),
    )(proj, proj, proj, proj, proj, conv_prev, ssm_prev, conv_w, conv_b.reshape(1, CONV_DIM),
      pad_lanes(dt_bias), pad_lanes(a_log), jnp.repeat(d_skip, SSD_HEAD_DIM).reshape(1, D_INNER),
      norm_w.reshape(1, D_INNER))


def _pad_cols(w, n):
    return jnp.pad(w, ((0, 0), (0, n - w.shape[1])))


def _split_rows(a):
    return (a[:M_PROMPT].reshape(BATCH, SEQ, a.shape[-1]),
            a[M_PROMPT:].reshape(DEC_BATCH, DEC_SEQ, a.shape[-1]))


def _join_rows(p, s):
    return jnp.concatenate([p.reshape(M_PROMPT, p.shape[-1]), s.reshape(M_SAMPLE, s.shape[-1])], axis=0)


def kernel(x_prompt, x_sample, state_ssm, state_conv, cache_nsa_kv, cache_nsa_win, cache_moba_kv, page_table, rel_bias, norm_w, final_norm_w, ffn_w_gate, ffn_w_up, ffn_w_down, ssd_w_in, ssd_conv_w, ssd_conv_b, ssd_dt_bias, ssd_a_log, ssd_d, ssd_norm_w, ssd_w_out, nsa_w_in, nsa_cmp_pe, nsa_cmp_w, nsa_w_out, moba_w_in, moba_w_out):
    bf = jnp.bfloat16
    h = _join_rows(x_prompt, x_sample)
    ssm_p, ssm_s, conv_p, conv_s = [], [], [], []
    nsa_kv_p, nsa_kv_s, nsa_win_p, nsa_win_s = [], [], [], []
    moba_kv_p, moba_kv_s = [], []
    for i in range(DEPTH):
        kind, j = i % N_MIXERS, i // N_MIXERS
        h = _ffn_half(h, norm_w[i, 0], ffn_w_gate[i, 0].astype(bf), ffn_w_up[i, 0].astype(bf), ffn_w_down[i, 0].astype(bf))
        if kind == 0:
            n_pad = -(-SSD_IN_DIM // 512) * 512
            proj = _norm_proj(h, norm_w[i, 1], _pad_cols(ssd_w_in[j], n_pad).astype(bf))
            w = (ssd_conv_w[j], ssd_conv_b[j], ssd_dt_bias[j], ssd_a_log[j], ssd_d[j], ssd_norm_w[j])
            conv0 = jnp.zeros((BATCH, CARRY_ROWS, CONV_DIM), jnp.float32)
            ssm0 = jnp.zeros((BATCH, D_INNER, D_STATE), jnp.float32)
            y_p, c_p, s_p = _ssd_mix(proj, conv0, ssm0, *w, 0, SSD_CHUNK, SEQ // SSD_CHUNK)
            conv1 = jnp.pad(state_conv[j], ((0, 0), (CARRY_ROWS - (CONV_W - 1), 0), (0, 0)))
            ssm1 = state_ssm[j].reshape(DEC_BATCH, D_INNER, D_STATE)
            y_s, c_s, s_s = _ssd_mix(proj, conv1, ssm1, *w, M_PROMPT // DEC_SEQ, DEC_SEQ, 1)
            conv_p.append(c_p)
            conv_s.append(c_s)
            ssm_p.append(s_p.reshape(BATCH, SSD_HEADS, SSD_HEAD_DIM, D_STATE))
            ssm_s.append(s_s.reshape(DEC_BATCH, SSD_HEADS, SSD_HEAD_DIM, D_STATE))
            w_out = ssd_w_out[j]
        elif kind == 1:
            n_pad = -(-NSA_IN_DIM // 512) * 512
            proj = _norm_proj(h, norm_w[i, 1], _pad_cols(nsa_w_in[j], n_pad).astype(bf))
            proj_p, proj_s = _split_rows(proj)
            kc, vc = _nsa_compress_prompt(proj, nsa_cmp_pe[j], nsa_cmp_w[j].astype(bf), BATCH, SEQ)
            y_p = _nsa_attention(proj, kc, vc, rel_bias, BATCH, SEQ)
            kv_all = proj_p[..., Q_DIM:Q_DIM + 6 * KV_DIM].reshape(BATCH, SEQ, 6, N_KV_HEADS, HEAD_DIM)
            kv_p, win_p = kv_all[:, :, :4], kv_all[:, SEQ - WINDOW:, 4:]
            cmp_w_bf = nsa_cmp_w[j].astype(bf)
            y_s = _nsa_sample(proj, cache_nsa_kv[j], cache_nsa_win[j], page_table, nsa_cmp_pe[j], cmp_w_bf,
                              rel_bias, M_PROMPT // DEC_SEQ, PAST_LEN)
            kv_new = proj_s[..., Q_DIM:Q_DIM + 6 * KV_DIM].reshape(DEC_BATCH, DEC_SEQ, 6, N_KV_HEADS, HEAD_DIM)
            kv_s = kv_new[:, :, :4]
            win_s = jnp.concatenate([cache_nsa_win[j][:, DEC_SEQ:], kv_new[:, :, 4:]], axis=1)
            nsa_kv_p.append(kv_p)
            nsa_kv_s.append(kv_s)
            nsa_win_p.append(win_p)
            nsa_win_s.append(win_s)
            w_out = nsa_w_out[j]
        else:
            proj = _norm_proj(h, norm_w[i, 1], moba_w_in[j].astype(bf))
            proj_p, proj_s = _split_rows(proj)
            y_p = _moba_attention(proj, rel_bias, BATCH, SEQ)
            kv_p = proj_p[..., Q_DIM:].reshape(BATCH, SEQ, 2, N_KV_HEADS, HEAD_DIM)
            kmean = _moba_sample_kmean(cache_moba_kv[j], page_table)
            y_s = _moba_sample_attention(proj, jnp.swapaxes(kmean, 1, 2), cache_moba_kv[j], page_table, rel_bias,
                                         M_PROMPT // DEC_SEQ, PAST_LEN)
            kv_s = proj_s[..., Q_DIM:].reshape(DEC_BATCH, DEC_SEQ, 2, N_KV_HEADS, HEAD_DIM)
            moba_kv_p.append(kv_p)
            moba_kv_s.append(kv_s)
            w_out = moba_w_out[j]
        h = _proj_residual(_join_rows(y_p, y_s), w_out.astype(bf), h)
        h = _ffn_half(h, norm_w[i, 2], ffn_w_gate[i, 1].astype(bf), ffn_w_up[i, 1].astype(bf), ffn_w_down[i, 1].astype(bf))
    y = _final_norm(h, final_norm_w)
    y_prompt, y_sample = _split_rows(y)
    return (y_prompt, y_sample,
            jnp.stack(ssm_p), jnp.stack(ssm_s), jnp.stack(conv_p), jnp.stack(conv_s),
            jnp.stack(nsa_kv_p), jnp.stack(nsa_kv_s), jnp.stack(nsa_win_p), jnp.stack(nsa_win_s),
            jnp.stack(moba_kv_p), jnp.stack(moba_kv_s))
```

```python
import functools
import math

import jax
import jax.numpy as jnp
import numpy as np
from jax import lax
from jax.experimental import pallas as pl
from jax.experimental.pallas import tpu as pltpu

D_MODEL = 2048
BATCH = 2
SEQ = 4096
DEPTH = 4
DEC_BATCH = 8
DEC_SEQ = 8
PAST_LEN = 16384
PAGE_SIZE = 128
N_MIXERS = 3
D_FF = 5632
RMS_EPS = 1e-6
D_INNER = 2 * D_MODEL
SSD_HEAD_DIM = 64
SSD_HEADS = D_INNER // SSD_HEAD_DIM
D_STATE = 128
SSD_GROUPS = 8
HEADS_PER_GROUP = SSD_HEADS // SSD_GROUPS
GN = SSD_GROUPS * D_STATE
CONV_W = 4
CONV_DIM = D_INNER + 2 * GN
SSD_IN_DIM = D_INNER + CONV_DIM + SSD_HEADS
SSD_CHUNK = 128
HEAD_DIM = 128
N_HEADS = D_MODEL // HEAD_DIM
N_KV_HEADS = 4
Q_PER_KV = N_HEADS // N_KV_HEADS
Q_DIM = N_HEADS * HEAD_DIM
KV_DIM = N_KV_HEADS * HEAD_DIM
Q_BLOCK = 64
CMP_BLOCK = 32
CMP_STRIDE = 16
SEL_BLOCK = 64
SEL_TOPK = 16
N_LOCAL_SEL = 2
WINDOW = 512
NSA_IN_DIM = Q_DIM + 6 * KV_DIM + 3 * N_HEADS
MOBA_BLOCK = 256
MOBA_TOPK = 3
MOBA_IN_DIM = Q_DIM + 2 * KV_DIM
REL_BUCKETS = 32
REL_MAX_EXACT = REL_BUCKETS // 2
REL_MAX_DIST = 128
NEG_INF = -1e30
FORCE_SCORE = 1e9

M_PROMPT = BATCH * SEQ
M_SAMPLE = DEC_BATCH * DEC_SEQ
M_TOTAL = M_PROMPT + M_SAMPLE
ROW_TILE = 688
LANE = 128
VMEM_LIMIT = 56 * 1024 * 1024


def _rms_rows(x, g):
    ms = jnp.mean(x * x, axis=-1, keepdims=True)
    return x * lax.rsqrt(ms + RMS_EPS) * g


def _ffn_kernel(h_ref, g_ref, wg_ref, wu_ref, wd_ref, o_ref, xn_ref, acc_ref):
    f = pl.program_id(1)

    @pl.when(f == 0)
    def _():
        xn_ref[...] = _rms_rows(h_ref[...], g_ref[...]).astype(jnp.bfloat16)
        acc_ref[...] = jnp.zeros_like(acc_ref)

    xn = xn_ref[...]
    gate = jnp.dot(xn, wg_ref[...], preferred_element_type=jnp.float32)
    up = jnp.dot(xn, wu_ref[...], preferred_element_type=jnp.float32)
    act = (gate * jax.nn.sigmoid(gate) * up).astype(jnp.bfloat16)
    acc_ref[...] += jnp.dot(act, wd_ref[...], preferred_element_type=jnp.float32)

    @pl.when(f == pl.num_programs(1) - 1)
    def _():
        o_ref[...] = h_ref[...] + 0.5 * acc_ref[...]


def _ffn_half(h, g, wg, wu, wd, *, ff_tile=512):
    m, d = h.shape
    dff = wg.shape[1]
    grid = (m // ROW_TILE, dff // ff_tile)
    return pl.pallas_call(
        _ffn_kernel,
        grid=grid,
        in_specs=[
            pl.BlockSpec((ROW_TILE, d), lambda i, f: (i, 0)),
            pl.BlockSpec((1, d), lambda i, f: (0, 0)),
            pl.BlockSpec((d, ff_tile), lambda i, f: (0, f)),
            pl.BlockSpec((d, ff_tile), lambda i, f: (0, f)),
            pl.BlockSpec((ff_tile, d), lambda i, f: (f, 0)),
        ],
        out_specs=pl.BlockSpec((ROW_TILE, d), lambda i, f: (i, 0)),
        out_shape=jax.ShapeDtypeStruct((m, d), jnp.float32),
        scratch_shapes=[
            pltpu.VMEM((ROW_TILE, d), jnp.bfloat16),
            pltpu.VMEM((ROW_TILE, d), jnp.float32),
        ],
        compiler_params=pltpu.CompilerParams(
            dimension_semantics=("parallel", "arbitrary"),
            vmem_limit_bytes=VMEM_LIMIT),
    )(h, g.reshape(1, d), wg, wu, wd)


def _norm_proj_kernel(h_ref, g_ref, w_ref, o_ref, xn_ref):
    @pl.when(pl.program_id(1) == 0)
    def _():
        xn_ref[...] = _rms_rows(h_ref[...], g_ref[...]).astype(jnp.bfloat16)

    o_ref[...] = jnp.dot(xn_ref[...], w_ref[...], preferred_element_type=jnp.float32)


PROJ_TILE_MAX = 1536


def _norm_proj(h, g, w):
    m, d = h.shape
    n = w.shape[1]
    n_tile = max(t for t in range(LANE, PROJ_TILE_MAX + 1, LANE) if n % t == 0)
    grid = (m // ROW_TILE, n // n_tile)
    return pl.pallas_call(
        _norm_proj_kernel,
        grid=grid,
        in_specs=[
            pl.BlockSpec((ROW_TILE, d), lambda i, j: (i, 0)),
            pl.BlockSpec((1, d), lambda i, j: (0, 0)),
            pl.BlockSpec((d, n_tile), lambda i, j: (0, j)),
        ],
        out_specs=pl.BlockSpec((ROW_TILE, n_tile), lambda i, j: (i, j)),
        out_shape=jax.ShapeDtypeStruct((m, n), jnp.float32),
        scratch_shapes=[pltpu.VMEM((ROW_TILE, d), jnp.bfloat16)],
        compiler_params=pltpu.CompilerParams(
            dimension_semantics=("parallel", "arbitrary"),
            vmem_limit_bytes=VMEM_LIMIT),
    )(h, g.reshape(1, d), w)


def _proj_residual_kernel(y_ref, w_ref, h_ref, o_ref):
    o_ref[...] = h_ref[...] + jnp.dot(
        y_ref[...].astype(jnp.bfloat16), w_ref[...], preferred_element_type=jnp.float32)


def _proj_residual(y, w, h, *, n_tile=512):
    m, k = y.shape
    d = w.shape[1]
    grid = (m // ROW_TILE, d // n_tile)
    return pl.pallas_call(
        _proj_residual_kernel,
        grid=grid,
        in_specs=[
            pl.BlockSpec((ROW_TILE, k), lambda i, j: (i, 0)),
            pl.BlockSpec((k, n_tile), lambda i, j: (0, j)),
            pl.BlockSpec((ROW_TILE, n_tile), lambda i, j: (i, j)),
        ],
        out_specs=pl.BlockSpec((ROW_TILE, n_tile), lambda i, j: (i, j)),
        out_shape=jax.ShapeDtypeStruct((m, d), jnp.float32),
        compiler_params=pltpu.CompilerParams(
            dimension_semantics=("parallel", "arbitrary"),
            vmem_limit_bytes=VMEM_LIMIT),
    )(y, w, h)


def _final_norm_kernel(h_ref, g_ref, o_ref):
    o_ref[...] = _rms_rows(h_ref[...], g_ref[...])


def _final_norm(h, g):
    m, d = h.shape
    return pl.pallas_call(
        _final_norm_kernel,
        grid=(m // ROW_TILE,),
        in_specs=[pl.BlockSpec((ROW_TILE, d), lambda i: (i, 0)),
                  pl.BlockSpec((1, d), lambda i: (0, 0))],
        out_specs=pl.BlockSpec((ROW_TILE, d), lambda i: (i, 0)),
        out_shape=jax.ShapeDtypeStruct((m, d), jnp.float32),
        compiler_params=pltpu.CompilerParams(dimension_semantics=("parallel",)),
    )(h, g.reshape(1, d))


ATT_TILE = 256
HEAD_ROWS = Q_PER_KV * ATT_TILE
SCALE = HEAD_DIM ** -0.5
CMP_BAND_START = LANE - ATT_TILE // CMP_STRIDE
_NT = (((1,), (1,)), ((), ()))


def _bucket_upper_bounds():
    n = np.arange(0, 2 * REL_MAX_DIST)
    ratio = np.log(np.maximum(n, 1).astype(np.float32) / np.float32(REL_MAX_EXACT)) / np.float32(
        math.log(REL_MAX_DIST / REL_MAX_EXACT))
    large = np.minimum(REL_MAX_EXACT + (ratio * np.float32(REL_BUCKETS - REL_MAX_EXACT)).astype(np.int32),
                       REL_BUCKETS - 1)
    bucket = np.where(n < REL_MAX_EXACT, n, large)
    return [int(n[bucket > b].min()) for b in range(REL_BUCKETS - 1)]


_BUCKET_HI = _bucket_upper_bounds()


def _bias_from_dist(dist, head_vals):
    accs = [jnp.zeros(dist.shape, jnp.float32) + vals[REL_BUCKETS - 1] for vals in head_vals]
    for b in range(REL_BUCKETS - 2, -1, -1):
        near = dist < _BUCKET_HI[b]
        accs = [jnp.where(near, vals[b], acc) for vals, acc in zip(head_vals, accs)]
    return accs


def _head_vals(rb_ref, g):
    return [[rb_ref[b, g * Q_PER_KV + r] for b in range(REL_BUCKETS)] for r in range(Q_PER_KV)]


def _build_bias_tiles_t(rb_ref, g, bias_ref):
    key = lax.broadcasted_iota(jnp.int32, (ATT_TILE, ATT_TILE), 0)
    qry = lax.broadcasted_iota(jnp.int32, (ATT_TILE, ATT_TILE), 1)
    vals = _head_vals(rb_ref, g)
    for kind in range(2):
        tiles = _bias_from_dist(qry - key + kind * ATT_TILE, vals)
        for r in range(Q_PER_KV):
            bias_ref[kind, :, r * ATT_TILE:(r + 1) * ATT_TILE] = tiles[r]
    for r in range(Q_PER_KV):
        bias_ref[2, :, r * ATT_TILE:(r + 1) * ATT_TILE] = (
            jnp.zeros((ATT_TILE, ATT_TILE), jnp.float32) + vals[r][REL_BUCKETS - 1])


def _flash_t(qs, k_tile, v_tile_t, bias, mask, state):
    s = lax.dot_general(k_tile, qs, _NT, preferred_element_type=jnp.float32) * SCALE + bias
    if mask is not None:
        s = jnp.where(mask, s, NEG_INF)
    tile_max = jnp.max(s, axis=0, keepdims=True)
    if state is None:
        m_new = tile_max
        p = jnp.exp(s - m_new)
        l = jnp.sum(p, axis=0, keepdims=True)
        acc = jnp.dot(v_tile_t, p.astype(jnp.bfloat16), preferred_element_type=jnp.float32)
    else:
        m, l, acc = state
        m_new = jnp.maximum(m, tile_max)
        alpha = jnp.exp(m - m_new)
        p = jnp.exp(s - m_new)
        l = alpha * l + jnp.sum(p, axis=0, keepdims=True)
        acc = alpha * acc + jnp.dot(v_tile_t, p.astype(jnp.bfloat16), preferred_element_type=jnp.float32)
    return m_new, l, acc


def _kv_tile_t(ref, j):
    return ref[pl.ds(pl.multiple_of(j * ATT_TILE, ATT_TILE), ATT_TILE), :].T.astype(jnp.bfloat16)


def _store_heads_t(o_ref, out_t):
    for r in range(Q_PER_KV):
        o_ref[:, r * HEAD_DIM:(r + 1) * HEAD_DIM] = out_t[:, r * ATT_TILE:(r + 1) * ATT_TILE].T


def _stack_heads(q_ref):
    return jnp.concatenate(
        [q_ref[:, r * HEAD_DIM:(r + 1) * HEAD_DIM] for r in range(Q_PER_KV)], axis=0).astype(jnp.bfloat16)


def _tile4(x):
    return jnp.concatenate([x] * Q_PER_KV, axis=0)


def _flash_first(qs, k_tile, v_tile, bias, mask):
    s = lax.dot_general(qs, k_tile, _NT, preferred_element_type=jnp.float32) * SCALE + bias
    s = jnp.where(mask, s, NEG_INF)
    m = jnp.max(s, axis=-1, keepdims=True)
    p = jnp.exp(s - m)
    l = jnp.sum(p, axis=-1, keepdims=True)
    acc = jnp.dot(p.astype(jnp.bfloat16), v_tile, preferred_element_type=jnp.float32)
    return m, l, acc


def _flash_next(qs, k_tile, v_tile, bias, mask, m, l, acc):
    s = lax.dot_general(qs, k_tile, _NT, preferred_element_type=jnp.float32) * SCALE + bias
    s = jnp.where(mask, s, NEG_INF)
    m_new = jnp.maximum(m, jnp.max(s, axis=-1, keepdims=True))
    alpha = jnp.exp(m - m_new)
    p = jnp.exp(s - m_new)
    l = alpha * l + jnp.sum(p, axis=-1, keepdims=True)
    acc = alpha * acc + jnp.dot(p.astype(jnp.bfloat16), v_tile, preferred_element_type=jnp.float32)
    return m_new, l, acc


def _kv_tile(ref, j):
    return ref[pl.ds(pl.multiple_of(j * ATT_TILE, ATT_TILE), ATT_TILE), :].astype(jnp.bfloat16)


def _rank_select(score, blk, top_k):
    cnt = jnp.zeros(score.shape, jnp.float32)
    for m in range(score.shape[0]):
        row = score[m:m + 1, :]
        beats = jnp.where(row > score, 1.0, jnp.where((row == score) & (blk > m), 1.0, 0.0))
        cnt = cnt + beats
    return cnt < top_k


def _moba_kernel(rb_ref, q_ref, k_ref, v_ref, o_ref, bias_ref, kmean_ref, sel_ref, *, n_tiles):
    g = pl.program_id(1)
    t = pl.program_id(2)

    @pl.when(t == 0)
    def _():
        _build_bias_tiles_t(rb_ref, g, bias_ref)
        kmean_ref[...] = jnp.zeros_like(kmean_ref)
        for n in range(n_tiles):
            kmean_ref[n:n + 1, :] = jnp.mean(k_ref[n * ATT_TILE:(n + 1) * ATT_TILE, :], axis=0, keepdims=True)

    qs = _stack_heads(q_ref)
    gate = lax.dot_general(kmean_ref[...].astype(jnp.bfloat16), qs, _NT, preferred_element_type=jnp.float32)
    blk = lax.broadcasted_iota(jnp.int32, gate.shape, 0)
    past = blk < t
    gate = jnp.where(past, gate, NEG_INF)
    chosen = _rank_select(gate, blk, MOBA_TOPK) & past
    sel_ref[...] = jnp.where(chosen, 1.0, 0.0)

    key = lax.broadcasted_iota(jnp.int32, (ATT_TILE, HEAD_ROWS), 0)
    qry = lax.broadcasted_iota(jnp.int32, (ATT_TILE, HEAD_ROWS), 1) & (ATT_TILE - 1)
    state = _flash_t(qs, _kv_tile(k_ref, t), _kv_tile_t(v_ref, t), bias_ref[0], key <= qry, None)

    def picked(j):
        return sel_ref[pl.ds(j, 1), :] > 0.5

    def far_body(j, state):
        bias = jnp.where(picked(j), bias_ref[2, 0:1, :], NEG_INF)
        return _flash_t(qs, _kv_tile(k_ref, j), _kv_tile_t(v_ref, j), bias, None, state)

    def prev_body(j, state):
        return _flash_t(qs, _kv_tile(k_ref, j), _kv_tile_t(v_ref, j), bias_ref[1], picked(j), state)

    prev = jnp.maximum(t - 1, 0)
    state = lax.fori_loop(0, prev, far_body, state)
    m, l, acc = lax.fori_loop(prev, t, prev_body, state)
    _store_heads_t(o_ref, acc / jnp.maximum(l, 1e-30))


def _moba_attention(proj, rel_bias, n_batch, seq):
    n_tiles = seq // ATT_TILE
    k_col = Q_DIM // HEAD_DIM
    v_col = (Q_DIM + KV_DIM) // HEAD_DIM
    q_cols = Q_PER_KV * HEAD_DIM
    return pl.pallas_call(
        functools.partial(_moba_kernel, n_tiles=n_tiles),
        grid=(n_batch, N_KV_HEADS, n_tiles),
        in_specs=[
            pl.BlockSpec(memory_space=pltpu.SMEM),
            pl.BlockSpec((ATT_TILE, q_cols), lambda b, g, t: (b * n_tiles + t, g)),
            pl.BlockSpec((seq, HEAD_DIM), lambda b, g, t: (b, k_col + g)),
            pl.BlockSpec((seq, HEAD_DIM), lambda b, g, t: (b, v_col + g)),
        ],
        out_specs=pl.BlockSpec((ATT_TILE, q_cols), lambda b, g, t: (b * n_tiles + t, g)),
        out_shape=jax.ShapeDtypeStruct((n_batch * seq, Q_DIM), jnp.float32),
        scratch_shapes=[
            pltpu.VMEM((3, ATT_TILE, HEAD_ROWS), jnp.float32),
            pltpu.VMEM((max(n_tiles, 16), HEAD_DIM), jnp.float32),
            pltpu.VMEM((max(n_tiles, 16), HEAD_ROWS), jnp.float32),
        ],
        compiler_params=pltpu.CompilerParams(
            dimension_semantics=("parallel", "parallel", "arbitrary"),
            vmem_limit_bytes=VMEM_LIMIT),
    )(rel_bias, proj, proj, proj)


def _nsa_compress_kernel(k_ref, v_ref, pe_ref, w_ref, kc_ref, vc_ref, *, n_cmp):
    for c, (x_ref, o_ref) in enumerate(((k_ref, kc_ref), (v_ref, vc_ref))):
        top = jnp.zeros((n_cmp, HEAD_DIM), jnp.float32)
        bot = jnp.zeros((n_cmp, HEAD_DIM), jnp.float32)
        for r in range(CMP_STRIDE):
            x = x_ref[pl.ds(r, n_cmp, stride=CMP_STRIDE), :]
            xt = (x + pe_ref[c, r:r + 1, :]).astype(jnp.bfloat16)
            xb = (x + pe_ref[c, CMP_STRIDE + r:CMP_STRIDE + r + 1, :]).astype(jnp.bfloat16)
            top = top + jnp.dot(xt, w_ref[c, r * HEAD_DIM:(r + 1) * HEAD_DIM, :], preferred_element_type=jnp.float32)
            bot = bot + jnp.dot(xb, w_ref[c, (CMP_STRIDE + r) * HEAD_DIM:(CMP_STRIDE + r + 1) * HEAD_DIM, :],
                                preferred_element_type=jnp.float32)
        o_ref[0, 0] = top + pltpu.roll(bot, n_cmp - 1, axis=0)


def _nsa_compress_prompt(proj, cmp_pe, cmp_w, n_batch, seq):
    n_cmp = seq // CMP_STRIDE
    k_col = Q_DIM // HEAD_DIM
    v_col = (Q_DIM + KV_DIM) // HEAD_DIM
    out = jax.ShapeDtypeStruct((n_batch, N_KV_HEADS, n_cmp, HEAD_DIM), jnp.float32)
    return pl.pallas_call(
        functools.partial(_nsa_compress_kernel, n_cmp=n_cmp),
        grid=(n_batch, N_KV_HEADS),
        in_specs=[
            pl.BlockSpec((seq, HEAD_DIM), lambda b, g: (b, k_col + g)),
            pl.BlockSpec((seq, HEAD_DIM), lambda b, g: (b, v_col + g)),
            pl.BlockSpec((2, CMP_BLOCK, HEAD_DIM), lambda b, g: (0, 0, 0)),
            pl.BlockSpec((2, CMP_BLOCK * HEAD_DIM, HEAD_DIM), lambda b, g: (0, 0, 0)),
        ],
        out_specs=[pl.BlockSpec((1, 1, n_cmp, HEAD_DIM), lambda b, g: (b, g, 0, 0))] * 2,
        out_shape=[out, out],
        compiler_params=pltpu.CompilerParams(
            dimension_semantics=("parallel", "parallel"), vmem_limit_bytes=VMEM_LIMIT),
    )(proj, proj, cmp_pe, cmp_w)


def _nsa_kernel_t(rb_ref, q_ref, gate_ref, kc_ref, vc_ref, ks_ref, vs_ref, kw_ref, vw_ref, ovt_ref, o_ref,
                  bias_ref, cbias_ref, *, n_cmp, n_sel):
    g = pl.program_id(1)
    t = pl.program_id(2)
    tile_tokens = ATT_TILE // CMP_STRIDE
    blocks_per_tile = ATT_TILE // SEL_BLOCK

    @pl.when(t == 0)
    def _():
        _build_bias_tiles_t(rb_ref, g, bias_ref)
        m = lax.broadcasted_iota(jnp.int32, (LANE, ATT_TILE), 0)
        i = lax.broadcasted_iota(jnp.int32, (LANE, ATT_TILE), 1)
        vals = _head_vals(rb_ref, g)
        band = _bias_from_dist(i - CMP_STRIDE * (m - CMP_BAND_START) - (CMP_BLOCK - 1), vals)
        for r in range(Q_PER_KV):
            cols = slice(r * ATT_TILE, (r + 1) * ATT_TILE)
            far = jnp.zeros((n_cmp, ATT_TILE), jnp.float32) + vals[r][REL_BUCKETS - 1]
            cbias_ref[0:n_cmp, cols] = far
            cbias_ref[n_cmp:n_cmp + LANE, cols] = band[r]
            cbias_ref[n_cmp + LANE:2 * n_cmp + LANE, cols] = far

    qs = _stack_heads(q_ref)
    key = lax.broadcasted_iota(jnp.int32, (ATT_TILE, HEAD_ROWS), 0)
    qry = lax.broadcasted_iota(jnp.int32, (ATT_TILE, HEAD_ROWS), 1) & (ATT_TILE - 1)

    start = pl.multiple_of(n_cmp + CMP_BAND_START - t * tile_tokens, tile_tokens)
    c_end = lax.broadcasted_iota(jnp.int32, (n_cmp, HEAD_ROWS), 0) * CMP_STRIDE + (CMP_BLOCK - 1)
    q_pos = t * ATT_TILE + (lax.broadcasted_iota(jnp.int32, (n_cmp, HEAD_ROWS), 1) & (ATT_TILE - 1))
    ok_c = c_end <= q_pos
    lc = lax.dot_general(kc_ref[0, 0].astype(jnp.bfloat16), qs, _NT, preferred_element_type=jnp.float32) * SCALE
    lc = jnp.where(ok_c, lc + cbias_ref[pl.ds(start, n_cmp), :], NEG_INF)
    e = jnp.where(ok_c, jnp.exp(lc - jnp.max(lc, axis=0, keepdims=True)), 0.0)
    pc = e / jnp.maximum(jnp.sum(e, axis=0, keepdims=True), 1e-30)
    o_cmp = jnp.dot(vc_ref[0, 0].T.astype(jnp.bfloat16), pc.astype(jnp.bfloat16), preferred_element_type=jnp.float32)

    p_sum = pc[:, 0:ATT_TILE]
    for r in range(1, Q_PER_KV):
        p_sum = p_sum + pc[:, r * ATT_TILE:(r + 1) * ATT_TILE]
    imp = jnp.dot(ovt_ref[...], p_sum.astype(jnp.bfloat16), preferred_element_type=jnp.float32)[:n_sel]
    blk = lax.broadcasted_iota(jnp.int32, (n_sel, ATT_TILE), 0)
    own = t * blocks_per_tile + (lax.broadcasted_iota(jnp.int32, (n_sel, ATT_TILE), 1) // SEL_BLOCK)
    diff = own - blk
    valid = diff >= 0
    forced = valid & ((blk == 0) | (diff < N_LOCAL_SEL))
    score = jnp.where(forced, FORCE_SCORE, jnp.where(valid, imp, NEG_INF))
    chosen = jnp.where(_rank_select(score, blk, SEL_TOPK) & valid, 1.0, 0.0)
    if n_sel < LANE:
        chosen = jnp.concatenate([chosen, jnp.zeros((LANE - n_sel, ATT_TILE), jnp.float32)], axis=0)
    chosen = chosen.astype(jnp.bfloat16)

    def sel_mask(j):
        pick = jnp.where(
            lax.broadcasted_iota(jnp.int32, (ATT_TILE, LANE), 1)
            == j * blocks_per_tile + lax.broadcasted_iota(jnp.int32, (ATT_TILE, LANE), 0) // SEL_BLOCK,
            1.0, 0.0).astype(jnp.bfloat16)
        return _tile4_lanes(jnp.dot(pick, chosen, preferred_element_type=jnp.float32)) > 0.5

    causal = key <= qry
    state = _flash_t(qs, _kv_tile(ks_ref, t), _kv_tile_t(vs_ref, t), bias_ref[0], sel_mask(t) & causal, None)

    def far_body(j, state):
        return _flash_t(qs, _kv_tile(ks_ref, j), _kv_tile_t(vs_ref, j), bias_ref[2, 0:1, :], sel_mask(j), state)

    def prev_body(j, state):
        return _flash_t(qs, _kv_tile(ks_ref, j), _kv_tile_t(vs_ref, j), bias_ref[1], sel_mask(j), state)

    prev = jnp.maximum(t - 1, 0)
    state = lax.fori_loop(0, prev, far_body, state)
    m, l, acc = lax.fori_loop(prev, t, prev_body, state)
    o_sel = acc / jnp.maximum(l, 1e-30)

    state = _flash_t(qs, _kv_tile(kw_ref, t), _kv_tile_t(vw_ref, t), bias_ref[0], causal, None)
    state = _flash_t(qs, _kv_tile(kw_ref, prev), _kv_tile_t(vw_ref, prev), bias_ref[1], (key >= 0) & (t >= 1), state)
    t2 = jnp.maximum(t - 2, 0)
    m, l, acc = _flash_t(qs, _kv_tile(kw_ref, t2), _kv_tile_t(vw_ref, t2), bias_ref[2, 0:1, :],
                         (key > qry) & (t >= 2), state)
    o_win = acc / jnp.maximum(l, 1e-30)

    gates = jax.nn.sigmoid(gate_ref[...]).T
    gate_row = lax.broadcasted_iota(jnp.int32, gates.shape, 0)
    for r in range(Q_PER_KV):
        cols = slice(r * ATT_TILE, (r + 1) * ATT_TILE)
        out = jnp.zeros((HEAD_DIM, ATT_TILE), jnp.float32)
        for c, branch in enumerate((o_cmp, o_sel, o_win)):
            gvec = jnp.sum(jnp.where(gate_row == g * (3 * Q_PER_KV) + r * 3 + c, gates, 0.0), axis=0, keepdims=True)
            out = out + gvec * branch[:, cols]
        o_ref[:, r * HEAD_DIM:(r + 1) * HEAD_DIM] = out.T


def _tile4_lanes(x):
    return jnp.concatenate([x] * Q_PER_KV, axis=1)


def _nsa_overlap_t(n_cmp):
    ci = np.arange(n_cmp)[None, :]
    sj = np.arange(LANE)[:, None]
    ov = (ci * CMP_STRIDE < (sj + 1) * SEL_BLOCK) & (ci * CMP_STRIDE + CMP_BLOCK > sj * SEL_BLOCK) & (ci < n_cmp - 1)
    return jnp.asarray(ov, jnp.bfloat16)


def _nsa_attention(proj, kc, vc, rel_bias, n_batch, seq):
    n_tiles = seq // ATT_TILE
    n_cmp = seq // CMP_STRIDE
    q_cols = Q_PER_KV * HEAD_DIM
    kv0 = Q_DIM // HEAD_DIM
    per = KV_DIM // HEAD_DIM
    gate_col = (Q_DIM + 6 * KV_DIM) // LANE

    def kv_spec(comp):
        return pl.BlockSpec((seq, HEAD_DIM), lambda b, g, t: (b, kv0 + comp * per + g))

    cmp_spec = pl.BlockSpec((1, 1, n_cmp, HEAD_DIM), lambda b, g, t: (b, g, 0, 0))
    return pl.pallas_call(
        functools.partial(_nsa_kernel_t, n_cmp=n_cmp, n_sel=seq // SEL_BLOCK),
        grid=(n_batch, N_KV_HEADS, n_tiles),
        in_specs=[
            pl.BlockSpec(memory_space=pltpu.SMEM),
            pl.BlockSpec((ATT_TILE, q_cols), lambda b, g, t: (b * n_tiles + t, g)),
            pl.BlockSpec((ATT_TILE, LANE), lambda b, g, t: (b * n_tiles + t, gate_col)),
            cmp_spec, cmp_spec,
            kv_spec(2), kv_spec(3), kv_spec(4), kv_spec(5),
            pl.BlockSpec((LANE, n_cmp), lambda b, g, t: (0, 0)),
        ],
        out_specs=pl.BlockSpec((ATT_TILE, q_cols), lambda b, g, t: (b * n_tiles + t, g)),
        out_shape=jax.ShapeDtypeStruct((n_batch * seq, Q_DIM), jnp.float32),
        scratch_shapes=[pltpu.VMEM((3, ATT_TILE, HEAD_ROWS), jnp.float32),
                        pltpu.VMEM((2 * n_cmp + LANE, HEAD_ROWS), jnp.float32)],
        compiler_params=pltpu.CompilerParams(
            dimension_semantics=("parallel", "parallel", "arbitrary"),
            vmem_limit_bytes=VMEM_LIMIT),
    )(rel_bias, proj, proj, kc, vc, proj, proj, proj, proj, _nsa_overlap_t(n_cmp))


NEW_ROWS = Q_PER_KV * DEC_SEQ


def _rank_select_rows(score, n_cand, top_k):
    lane = lax.broadcasted_iota(jnp.int32, score.shape, 1)
    cnt = jnp.zeros(score.shape, jnp.float32)
    for m in range(n_cand):
        col = score[:, m:m + 1]
        cnt = cnt + jnp.where(col > score, 1.0, jnp.where((col == score) & (lane > m), 1.0, 0.0))
    return cnt < top_k


def _stack_group_heads(q_ref, g):
    return jnp.concatenate(
        [q_ref[:, (g * Q_PER_KV + r) * HEAD_DIM:(g * Q_PER_KV + r + 1) * HEAD_DIM] for r in range(Q_PER_KV)],
        axis=0).astype(jnp.bfloat16)


def _group_bias(rb_ref, g, dist):
    return jnp.concatenate(_bias_from_dist(dist, _head_vals(rb_ref, g)), axis=0)


def _load_states(refs):
    m_ref, l_ref, acc_ref = refs
    return [(m_ref[g], l_ref[g], acc_ref[g]) for g in range(N_KV_HEADS)]


def _store_states(refs, states):
    m_ref, l_ref, acc_ref = refs
    for g, (m, l, acc) in enumerate(states):
        m_ref[g] = m
        l_ref[g] = l
        acc_ref[g] = acc


def _cache_step_bias(rb_ref, g, past_len, first_key, n_keys, near):
    if near:
        q_pos = past_len + lax.broadcasted_iota(jnp.int32, (DEC_SEQ, n_keys), 0)
        k_pos = first_key + lax.broadcasted_iota(jnp.int32, (DEC_SEQ, n_keys), 1)
        return _group_bias(rb_ref, g, q_pos - k_pos)
    return jnp.concatenate(
        [jnp.zeros((DEC_SEQ, 1), jnp.float32) + rb_ref[REL_BUCKETS - 1, g * Q_PER_KV + r]
         for r in range(Q_PER_KV)], axis=0)


def _when_near_or_far(past_len, first_key, n_keys, step):
    near = past_len - (first_key + n_keys - 1) < REL_MAX_DIST
    pl.when(near)(lambda: step(True))
    pl.when(jnp.logical_not(near))(lambda: step(False))


def _flash_update(refs, g, qs, k_tile, v_tile, bias, mask, first):
    m_ref, l_ref, acc_ref = refs
    if first:
        m, l, acc = _flash_first(qs, k_tile, v_tile, bias, mask)
    else:
        m, l, acc = _flash_next(qs, k_tile, v_tile, bias, mask, m_ref[g], l_ref[g], acc_ref[g])
    m_ref[g] = m
    l_ref[g] = l
    acc_ref[g] = acc


PAGES_PER_BLOCK = MOBA_BLOCK // PAGE_SIZE
KMEAN_PAGES = 8
MOBA_STEP_PAGES = 8
MOBA_STEP_KEYS = MOBA_STEP_PAGES * PAGE_SIZE


def _moba_kmean_kernel(pt_ref, *refs):
    o_ref = refs[KMEAN_PAGES]
    for i in range(KMEAN_PAGES // PAGES_PER_BLOCK):
        total = jnp.sum(refs[PAGES_PER_BLOCK * i][...], axis=0)
        for k in range(1, PAGES_PER_BLOCK):
            total = total + jnp.sum(refs[PAGES_PER_BLOCK * i + k][...], axis=0)
        o_ref[i] = total * (1.0 / MOBA_BLOCK)


def _page_spec(comp, which, per_step):
    return pl.BlockSpec((None, PAGE_SIZE, None, N_KV_HEADS, HEAD_DIM),
                        lambda b, n, pt: (pt[b, n * per_step + which], 0, comp, 0, 0))


MOBA_ROW_VECTORS = 2 * N_KV_HEADS
NSA_ROW_VECTORS = 4 * N_KV_HEADS


def _flat_pages(cache):
    return cache.reshape(-1, HEAD_DIM)


def _flat_page_spec(vectors_per_row, which, per_step):
    return pl.BlockSpec((PAGE_SIZE * vectors_per_row, HEAD_DIM),
                        lambda b, n, pt: (pt[b, n * per_step + which], 0))


def _page_vectors(ref, index, vectors_per_row):
    return ref[pl.ds(index, PAGE_SIZE, stride=vectors_per_row), :]


def _moba_sample_kmean(cache, page_table):
    n_seq, n_pages = page_table.shape
    n_blk = n_pages // PAGES_PER_BLOCK
    blk_per_step = KMEAN_PAGES // PAGES_PER_BLOCK
    return pl.pallas_call(
        _moba_kmean_kernel,
        grid_spec=pltpu.PrefetchScalarGridSpec(
            num_scalar_prefetch=1,
            grid=(n_seq, n_pages // KMEAN_PAGES),
            in_specs=[_page_spec(0, k, KMEAN_PAGES) for k in range(KMEAN_PAGES)],
            out_specs=pl.BlockSpec((None, blk_per_step, N_KV_HEADS, HEAD_DIM), lambda b, n, pt: (b, n, 0, 0)),
        ),
        out_shape=jax.ShapeDtypeStruct((n_seq, n_blk, N_KV_HEADS, HEAD_DIM), jnp.float32),
        compiler_params=pltpu.CompilerParams(dimension_semantics=("parallel", "arbitrary")),
    )(page_table, *([cache] * KMEAN_PAGES))


def _moba_sample_kernel(pt_ref, rb_ref, q_ref, kn_ref, vn_ref, kmean_ref, *refs, n_blk, n_steps, past_len):
    page_refs = refs[:MOBA_STEP_PAGES]
    o_ref, m_ref, l_ref, acc_ref, sel_ref = refs[MOBA_STEP_PAGES:]
    n = pl.program_id(1)
    refs = (m_ref, l_ref, acc_ref)
    tok = lax.broadcasted_iota(jnp.int32, (DEC_SEQ, DEC_SEQ), 0)
    new = lax.broadcasted_iota(jnp.int32, (DEC_SEQ, DEC_SEQ), 1)

    @pl.when(n == 0)
    def _():
        for g in range(N_KV_HEADS):
            qs = _stack_group_heads(q_ref, g)
            gate = lax.dot_general(qs, kmean_ref[g].astype(jnp.bfloat16), _NT, preferred_element_type=jnp.float32)
            gate = jnp.concatenate([gate, jnp.full((NEW_ROWS, LANE - n_blk), NEG_INF, jnp.float32)], axis=1)
            lane = lax.broadcasted_iota(jnp.int32, gate.shape, 1)
            chosen = _rank_select_rows(gate, n_blk, MOBA_TOPK) & (lane < n_blk)
            sel_ref[g] = jnp.where(chosen, 1.0, 0.0)
            _flash_update(refs, g, qs, kn_ref[:, g * HEAD_DIM:(g + 1) * HEAD_DIM].astype(jnp.bfloat16),
                          vn_ref[:, g * HEAD_DIM:(g + 1) * HEAD_DIM].astype(jnp.bfloat16),
                          _group_bias(rb_ref, g, tok - new), _tile4(new <= tok), True)

    pick = jnp.where(
        lax.broadcasted_iota(jnp.int32, (LANE, MOBA_STEP_KEYS), 0)
        == n * (MOBA_STEP_KEYS // MOBA_BLOCK) + lax.broadcasted_iota(jnp.int32, (LANE, MOBA_STEP_KEYS), 1) // MOBA_BLOCK,
        1.0, 0.0).astype(jnp.bfloat16)

    def step(near):
        old, new = _load_states(refs), []
        for g in range(N_KV_HEADS):
            qs = _stack_group_heads(q_ref, g)
            k_tile = jnp.concatenate(
                [_page_vectors(ref, g, MOBA_ROW_VECTORS) for ref in page_refs], axis=0).astype(jnp.bfloat16)
            v_tile = jnp.concatenate(
                [_page_vectors(ref, N_KV_HEADS + g, MOBA_ROW_VECTORS) for ref in page_refs],
                axis=0).astype(jnp.bfloat16)
            mask = jnp.dot(sel_ref[g].astype(jnp.bfloat16), pick, preferred_element_type=jnp.float32) > 0.5
            bias = _cache_step_bias(rb_ref, g, past_len, n * MOBA_STEP_KEYS, MOBA_STEP_KEYS, near)
            new.append(_flash_next(qs, k_tile, v_tile, bias, mask, *old[g]))
        _store_states(refs, new)

    _when_near_or_far(past_len, n * MOBA_STEP_KEYS, MOBA_STEP_KEYS, step)

    @pl.when(n == n_steps - 1)
    def _():
        for g in range(N_KV_HEADS):
            out = acc_ref[g] / jnp.maximum(l_ref[g], 1e-30)
            for r in range(Q_PER_KV):
                h = g * Q_PER_KV + r
                o_ref[:, h * HEAD_DIM:(h + 1) * HEAD_DIM] = out[r * DEC_SEQ:(r + 1) * DEC_SEQ]


def _moba_sample_attention(proj, kmean, cache, page_table, rel_bias, row_block0, past_len):
    n_seq, n_pages = page_table.shape
    n_blk = n_pages // PAGES_PER_BLOCK
    n_steps = n_pages // MOBA_STEP_PAGES
    kv_cols = KV_DIM
    return pl.pallas_call(
        functools.partial(_moba_sample_kernel, n_blk=n_blk, n_steps=n_steps, past_len=past_len),
        grid_spec=pltpu.PrefetchScalarGridSpec(
            num_scalar_prefetch=1,
            grid=(n_seq, n_steps),
            in_specs=[
                pl.BlockSpec(memory_space=pltpu.SMEM),
                pl.BlockSpec((DEC_SEQ, Q_DIM), lambda b, n, pt: (row_block0 + b, 0)),
                pl.BlockSpec((DEC_SEQ, kv_cols), lambda b, n, pt: (row_block0 + b, Q_DIM // kv_cols)),
                pl.BlockSpec((DEC_SEQ, kv_cols), lambda b, n, pt: (row_block0 + b, Q_DIM // kv_cols + 1)),
                pl.BlockSpec((None, N_KV_HEADS, n_blk, HEAD_DIM), lambda b, n, pt: (b, 0, 0, 0)),
            ] + [_flat_page_spec(MOBA_ROW_VECTORS, k, MOBA_STEP_PAGES) for k in range(MOBA_STEP_PAGES)],
            out_specs=pl.BlockSpec((DEC_SEQ, Q_DIM), lambda b, n, pt: (b, 0)),
            scratch_shapes=[
                pltpu.VMEM((N_KV_HEADS, NEW_ROWS, 1), jnp.float32),
                pltpu.VMEM((N_KV_HEADS, NEW_ROWS, 1), jnp.float32),
                pltpu.VMEM((N_KV_HEADS, NEW_ROWS, HEAD_DIM), jnp.float32),
                pltpu.VMEM((N_KV_HEADS, NEW_ROWS, LANE), jnp.float32),
            ],
        ),
        out_shape=jax.ShapeDtypeStruct((n_seq * DEC_SEQ, Q_DIM), jnp.float32),
        compiler_params=pltpu.CompilerParams(dimension_semantics=("parallel", "arbitrary")),
    )(page_table, rel_bias, proj, proj, proj, kmean, *([_flat_pages(cache)] * MOBA_STEP_PAGES))


CMP_PAGES = 8
SEL_STEP_PAGES = 8
SEL_STEP_KEYS = SEL_STEP_PAGES * PAGE_SIZE
CHUNKS_PER_PAGE = PAGE_SIZE // CMP_STRIDE


def _nsa_sample_compress_kernel(pt_ref, *refs):
    page_refs = refs[:CMP_PAGES]
    pe_ref, w_ref, top_ref, bot_ref = refs[CMP_PAGES:]
    rows = CMP_PAGES * CHUNKS_PER_PAGE
    for c in range(2):
        top = jnp.zeros((N_KV_HEADS * rows, HEAD_DIM), jnp.float32)
        bot = jnp.zeros((N_KV_HEADS * rows, HEAD_DIM), jnp.float32)
        for r in range(CMP_STRIDE):
            x = jnp.concatenate(
                [ref[pl.ds(r, CHUNKS_PER_PAGE, stride=CMP_STRIDE), c, g, :]
                 for g in range(N_KV_HEADS) for ref in page_refs], axis=0)
            xt = (x + pe_ref[c, r:r + 1, :]).astype(jnp.bfloat16)
            xb = (x + pe_ref[c, CMP_STRIDE + r:CMP_STRIDE + r + 1, :]).astype(jnp.bfloat16)
            top = top + jnp.dot(xt, w_ref[c, r * HEAD_DIM:(r + 1) * HEAD_DIM, :], preferred_element_type=jnp.float32)
            bot = bot + jnp.dot(xb, w_ref[c, (CMP_STRIDE + r) * HEAD_DIM:(CMP_STRIDE + r + 1) * HEAD_DIM, :],
                                preferred_element_type=jnp.float32)
        for g in range(N_KV_HEADS):
            top_ref[c, g] = top[g * rows:(g + 1) * rows]
            bot_ref[c, g] = bot[g * rows:(g + 1) * rows]


def _nsa_sample_compress(cache, page_table, cmp_pe, cmp_w):
    n_seq, n_pages = page_table.shape
    n_chunks = n_pages * CHUNKS_PER_PAGE
    rows = CMP_PAGES * CHUNKS_PER_PAGE

    def page_spec(k):
        return pl.BlockSpec((None, PAGE_SIZE, 2, N_KV_HEADS, HEAD_DIM),
                            lambda b, s, pt: (pt[b, s * CMP_PAGES + k], 0, 0, 0, 0))

    out = jax.ShapeDtypeStruct((n_seq, 2, N_KV_HEADS, n_chunks, HEAD_DIM), jnp.float32)
    out_spec = pl.BlockSpec((None, 2, N_KV_HEADS, rows, HEAD_DIM), lambda b, s, pt: (b, 0, 0, s, 0))
    return pl.pallas_call(
        _nsa_sample_compress_kernel,
        grid_spec=pltpu.PrefetchScalarGridSpec(
            num_scalar_prefetch=1,
            grid=(n_seq, n_pages // CMP_PAGES),
            in_specs=[page_spec(k) for k in range(CMP_PAGES)] + [
                pl.BlockSpec((2, CMP_BLOCK, HEAD_DIM), lambda b, s, pt: (0, 0, 0)),
                pl.BlockSpec((2, CMP_BLOCK * HEAD_DIM, HEAD_DIM), lambda b, s, pt: (0, 0, 0)),
            ],
            out_specs=[out_spec, out_spec],
        ),
        out_shape=[out, out],
        compiler_params=pltpu.CompilerParams(
            dimension_semantics=("parallel", "parallel"), vmem_limit_bytes=VMEM_LIMIT),
    )(page_table, *([cache] * CMP_PAGES), cmp_pe, cmp_w)


def _nsa_sample_kernel(pt_ref, rb_ref, q_ref, gate_ref, ksn_ref, vsn_ref, kwn_ref, vwn_ref, top_ref, bot_ref,
                       ov_ref, win_ref, *refs, n_pages, past_len):
    page_refs = refs[:SEL_STEP_PAGES]
    o_ref, m_ref, l_ref, acc_ref, sel_ref, ocmp_ref, owin_ref = refs[SEL_STEP_PAGES:]
    p = pl.program_id(1)
    refs = (m_ref, l_ref, acc_ref)
    n_chunks = n_pages * CHUNKS_PER_PAGE
    n_sel_pad = ov_ref.shape[1]
    n_sel = past_len // SEL_BLOCK + 1
    win_len = win_ref.shape[0]
    tok = lax.broadcasted_iota(jnp.int32, (DEC_SEQ, DEC_SEQ), 0)
    new = lax.broadcasted_iota(jnp.int32, (DEC_SEQ, DEC_SEQ), 1)

    def group_cols(ref, g):
        return ref[:, g * HEAD_DIM:(g + 1) * HEAD_DIM].astype(jnp.bfloat16)

    @pl.when(p == 0)
    def _():
        for g in range(N_KV_HEADS):
            qs = _stack_group_heads(q_ref, g)
            kc = top_ref[0, g] + pltpu.roll(bot_ref[0, g], n_chunks - 1, axis=0)
            vc = top_ref[1, g] + pltpu.roll(bot_ref[1, g], n_chunks - 1, axis=0)
            q_pos = past_len + lax.broadcasted_iota(jnp.int32, (DEC_SEQ, n_chunks), 0)
            tok_id = lax.broadcasted_iota(jnp.int32, (DEC_SEQ, n_chunks), 1)
            c_end = tok_id * CMP_STRIDE + (CMP_BLOCK - 1)
            ok_c = _tile4((tok_id < n_chunks - 1) & (c_end <= q_pos))
            lc = lax.dot_general(qs, kc.astype(jnp.bfloat16), _NT, preferred_element_type=jnp.float32) * SCALE
            lc = jnp.where(ok_c, lc + _group_bias(rb_ref, g, q_pos - c_end), NEG_INF)
            e = jnp.where(ok_c, jnp.exp(lc - jnp.max(lc, axis=-1, keepdims=True)), 0.0)
            pc = e / jnp.maximum(jnp.sum(e, axis=-1, keepdims=True), 1e-30)
            ocmp_ref[g] = jnp.dot(pc.astype(jnp.bfloat16), vc.astype(jnp.bfloat16), preferred_element_type=jnp.float32)
            p_sum = pc[0:DEC_SEQ]
            for r in range(1, Q_PER_KV):
                p_sum = p_sum + pc[r * DEC_SEQ:(r + 1) * DEC_SEQ]
            imp = jnp.dot(p_sum.astype(jnp.bfloat16), ov_ref[...], preferred_element_type=jnp.float32)
            blk = lax.broadcasted_iota(jnp.int32, (DEC_SEQ, n_sel_pad), 1)
            own = (past_len + lax.broadcasted_iota(jnp.int32, (DEC_SEQ, n_sel_pad), 0)) // SEL_BLOCK
            diff = own - blk
            valid = diff >= 0
            forced = valid & ((blk == 0) | (diff < N_LOCAL_SEL))
            score = jnp.where(forced, FORCE_SCORE, jnp.where(valid, imp, NEG_INF))
            chosen = _rank_select_rows(score, n_sel, SEL_TOPK) & valid
            sel_ref[g] = _tile4(jnp.where(chosen, 1.0, 0.0))
            causal = _tile4(new <= tok)
            near = _group_bias(rb_ref, g, tok - new)
            _flash_update(refs, g, qs, group_cols(ksn_ref, g), group_cols(vsn_ref, g), near, causal, True)
            mw, lw, aw = _flash_first(qs, group_cols(kwn_ref, g), group_cols(vwn_ref, g), near, causal)
            tok_w = lax.broadcasted_iota(jnp.int32, (DEC_SEQ, win_len), 0)
            dist_w = win_len + tok_w - lax.broadcasted_iota(jnp.int32, (DEC_SEQ, win_len), 1)
            mw, lw, aw = _flash_next(qs, win_ref[:, 0, g, :].astype(jnp.bfloat16), win_ref[:, 1, g, :].astype(jnp.bfloat16),
                                     _group_bias(rb_ref, g, dist_w), _tile4(dist_w < WINDOW), mw, lw, aw)
            owin_ref[g] = aw / jnp.maximum(lw, 1e-30)

    pick = jnp.where(
        lax.broadcasted_iota(jnp.int32, (n_sel_pad, SEL_STEP_KEYS), 0)
        == p * (SEL_STEP_KEYS // SEL_BLOCK) + lax.broadcasted_iota(jnp.int32, (n_sel_pad, SEL_STEP_KEYS), 1) // SEL_BLOCK,
        1.0, 0.0).astype(jnp.bfloat16)

    def step(near):
        old, new = _load_states(refs), []
        for g in range(N_KV_HEADS):
            qs = _stack_group_heads(q_ref, g)
            k_tile = jnp.concatenate(
                [_page_vectors(ref, 2 * N_KV_HEADS + g, NSA_ROW_VECTORS) for ref in page_refs],
                axis=0).astype(jnp.bfloat16)
            v_tile = jnp.concatenate(
                [_page_vectors(ref, 3 * N_KV_HEADS + g, NSA_ROW_VECTORS) for ref in page_refs],
                axis=0).astype(jnp.bfloat16)
            mask = jnp.dot(sel_ref[g].astype(jnp.bfloat16), pick, preferred_element_type=jnp.float32) > 0.5
            bias = _cache_step_bias(rb_ref, g, past_len, p * SEL_STEP_KEYS, SEL_STEP_KEYS, near)
            new.append(_flash_next(qs, k_tile, v_tile, bias, mask, *old[g]))
        _store_states(refs, new)

    _when_near_or_far(past_len, p * SEL_STEP_KEYS, SEL_STEP_KEYS, step)

    @pl.when(p == n_pages // SEL_STEP_PAGES - 1)
    def _():
        gates = jax.nn.sigmoid(gate_ref[...])
        for g in range(N_KV_HEADS):
            o_sel = acc_ref[g] / jnp.maximum(l_ref[g], 1e-30)
            for r in range(Q_PER_KV):
                rows = slice(r * DEC_SEQ, (r + 1) * DEC_SEQ)
                col = (g * Q_PER_KV + r) * 3
                out = (gates[:, col:col + 1] * ocmp_ref[g][rows] + gates[:, col + 1:col + 2] * o_sel[rows]
                       + gates[:, col + 2:col + 3] * owin_ref[g][rows])
                h = g * Q_PER_KV + r
                o_ref[:, h * HEAD_DIM:(h + 1) * HEAD_DIM] = out


def _nsa_sample_overlap(n_chunks, n_sel_pad):
    ci = np.arange(n_chunks)[:, None]
    sj = np.arange(n_sel_pad)[None, :]
    ov = (ci * CMP_STRIDE < (sj + 1) * SEL_BLOCK) & (ci * CMP_STRIDE + CMP_BLOCK > sj * SEL_BLOCK) & (ci < n_chunks - 1)
    return jnp.asarray(ov, jnp.bfloat16)


def _nsa_sample(proj, cache, win_buf, page_table, cmp_pe, cmp_w, rel_bias, row_block0, past_len):
    n_seq, n_pages = page_table.shape
    n_chunks = n_pages * CHUNKS_PER_PAGE
    n_sel_pad = -(-(past_len // SEL_BLOCK + 1) // LANE) * LANE
    top, bot = _nsa_sample_compress(cache, page_table, cmp_pe, cmp_w)
    kv0 = Q_DIM // KV_DIM
    gate_col = (Q_DIM + 6 * KV_DIM) // LANE

    def new_spec(comp):
        return pl.BlockSpec((DEC_SEQ, KV_DIM), lambda b, p, pt: (row_block0 + b, kv0 + comp))

    cmp_spec = pl.BlockSpec((None, 2, N_KV_HEADS, n_chunks, HEAD_DIM), lambda b, p, pt: (b, 0, 0, 0, 0))
    scratch = [
        pltpu.VMEM((N_KV_HEADS, NEW_ROWS, 1), jnp.float32),
        pltpu.VMEM((N_KV_HEADS, NEW_ROWS, 1), jnp.float32),
        pltpu.VMEM((N_KV_HEADS, NEW_ROWS, HEAD_DIM), jnp.float32),
        pltpu.VMEM((N_KV_HEADS, NEW_ROWS, n_sel_pad), jnp.float32),
        pltpu.VMEM((N_KV_HEADS, NEW_ROWS, HEAD_DIM), jnp.float32),
        pltpu.VMEM((N_KV_HEADS, NEW_ROWS, HEAD_DIM), jnp.float32),
    ]
    return pl.pallas_call(
        functools.partial(_nsa_sample_kernel, n_pages=n_pages, past_len=past_len),
        grid_spec=pltpu.PrefetchScalarGridSpec(
            num_scalar_prefetch=1,
            grid=(n_seq, n_pages // SEL_STEP_PAGES),
            in_specs=[
                pl.BlockSpec(memory_space=pltpu.SMEM),
                pl.BlockSpec((DEC_SEQ, Q_DIM), lambda b, p, pt: (row_block0 + b, 0)),
                pl.BlockSpec((DEC_SEQ, LANE), lambda b, p, pt: (row_block0 + b, gate_col)),
                new_spec(2), new_spec(3), new_spec(4), new_spec(5),
                cmp_spec, cmp_spec,
                pl.BlockSpec((n_chunks, n_sel_pad), lambda b, p, pt: (0, 0)),
                pl.BlockSpec((None,) + win_buf.shape[1:], lambda b, p, pt: (b, 0, 0, 0, 0)),
            ] + [_flat_page_spec(NSA_ROW_VECTORS, k, SEL_STEP_PAGES) for k in range(SEL_STEP_PAGES)],
            out_specs=pl.BlockSpec((DEC_SEQ, Q_DIM), lambda b, p, pt: (b, 0)),
            scratch_shapes=scratch,
        ),
        out_shape=jax.ShapeDtypeStruct((n_seq * DEC_SEQ, Q_DIM), jnp.float32),
        compiler_params=pltpu.CompilerParams(
            dimension_semantics=("parallel", "arbitrary"), vmem_limit_bytes=VMEM_LIMIT),
    )(page_table, rel_bias, proj, proj, proj, proj, proj, proj, top, bot,
      _nsa_sample_overlap(n_chunks, n_sel_pad), win_buf, *([_flat_pages(cache)] * SEL_STEP_PAGES))


SSD_PAIR = 2 * SSD_HEAD_DIM
GROUP_COLS = HEADS_PER_GROUP * SSD_HEAD_DIM
CARRY_ROWS = 8


def _pad_rows(x, rows):
    if x.shape[0] == rows:
        return x
    return jnp.concatenate([x, jnp.zeros((rows - x.shape[0],) + x.shape[1:], x.dtype)], axis=0)


def _ssd_kernel(z_ref, x_ref, b_ref, c_ref, dt_ref, prev_ref, s0_ref, cw_ref, cb_ref, dtb_ref, alog_ref, dsk_ref,
                nw_ref, y_ref, conv_ref, state_ref, carry_ref, s_ref, act_ref, *, rows_in, n_chunks):
    c = pl.program_id(1)
    L = SSD_CHUNK

    @pl.when(c == 0)
    def _():
        carry_ref[...] = prev_ref[...]
        s_ref[...] = s0_ref[...]

    row8 = lax.broadcasted_iota(jnp.int32, (CARRY_ROWS, 1), 0)
    off = 0
    for ref in (x_ref, b_ref, c_ref):
        width = ref.shape[1]
        cols = slice(off, off + width)
        raw_in = ref[...]
        raw = _pad_rows(raw_in, L)
        prev = carry_ref[:, cols]
        acc = cb_ref[:, cols] + cw_ref[CONV_W - 1:CONV_W, cols] * raw
        for k in range(1, CONV_W):
            rolled = pltpu.roll(raw, k, axis=0)
            head = jnp.where(row8 < k, pltpu.roll(prev, k, axis=0), rolled[:CARRY_ROWS])
            shifted = jnp.concatenate([head, rolled[CARRY_ROWS:]], axis=0)
            acc = acc + cw_ref[CONV_W - 1 - k:CONV_W - k, cols] * shifted
        act_ref[:, cols] = acc * jax.nn.sigmoid(acc)
        carry_ref[:, cols] = raw_in[rows_in - CARRY_ROWS:rows_in]

        @pl.when(c == n_chunks - 1)
        def _():
            conv_ref[:, cols] = raw_in[rows_in - (CONV_W - 1):rows_in]

        off += width

    row = lax.broadcasted_iota(jnp.int32, (L, LANE), 0)
    pre = _pad_rows(dt_ref[...], L) + dtb_ref[...]
    dt = jnp.maximum(pre, 0.0) + jnp.log1p(jnp.exp(-jnp.abs(pre)))
    dt = jnp.where(row < rows_in, dt, 0.0)
    acum = dt * (-jnp.exp(alog_ref[...]))
    shift = 1
    while shift < L:
        acum = acum + jnp.where(row >= shift, pltpu.roll(acum, shift, axis=0), 0.0)
        shift *= 2
    acum_t = acum.T
    exp_a = jnp.exp(acum)
    to_end = jnp.exp(acum[L - 1:L, :] - acum)
    li = lax.broadcasted_iota(jnp.int32, (L, L), 0)
    si = lax.broadcasted_iota(jnp.int32, (L, L), 1)
    causal = li >= si
    first_half = lax.broadcasted_iota(jnp.int32, (L, SSD_PAIR), 1) < SSD_HEAD_DIM

    def pair_cols(v, h):
        return jnp.where(first_half, v[:, h:h + 1], v[:, h + 1:h + 2])

    for g in range(SSD_GROUPS):
        bg = act_ref[:, D_INNER + g * D_STATE:D_INNER + (g + 1) * D_STATE].astype(jnp.bfloat16)
        cg = act_ref[:, D_INNER + GN + g * D_STATE:D_INNER + GN + (g + 1) * D_STATE].astype(jnp.bfloat16)
        cb = lax.dot_general(cg, bg, _NT, preferred_element_type=jnp.float32)
        s_prev = s_ref[g * GROUP_COLS:(g + 1) * GROUP_COLS, :]
        y_off = lax.dot_general(cg, s_prev.astype(jnp.bfloat16), _NT, preferred_element_type=jnp.float32)
        xw_parts = []
        for pr in range(HEADS_PER_GROUP // 2):
            h = g * HEADS_PER_GROUP + 2 * pr
            cols = slice(h * SSD_HEAD_DIM, (h + 2) * SSD_HEAD_DIM)
            xa = act_ref[:, cols]
            xdt = xa * pair_cols(dt, h)
            y = y_off[:, 2 * pr * SSD_HEAD_DIM:(2 * pr + 2) * SSD_HEAD_DIM] * pair_cols(exp_a, h)
            for half in range(2):
                hh = h + half
                decay = jnp.where(causal, jnp.exp(acum[:, hh:hh + 1] - acum_t[hh:hh + 1, :]), 0.0)
                mine = first_half if half == 0 else jnp.logical_not(first_half)
                y = y + jnp.dot((cb * decay).astype(jnp.bfloat16), jnp.where(mine, xdt, 0.0).astype(jnp.bfloat16),
                                preferred_element_type=jnp.float32)
            zz = z_ref[:, cols]
            y = y[:rows_in] + xa[:rows_in] * dsk_ref[:, cols]
            y_ref[:, cols] = y * (zz * jax.nn.sigmoid(zz))
            xw_parts.append(xdt * pair_cols(to_end, h))
        xw = jnp.concatenate(xw_parts, axis=1)
        chunk_state = jnp.dot(xw.T.astype(jnp.bfloat16), bg, preferred_element_type=jnp.float32)
        for r in range(HEADS_PER_GROUP):
            h = g * HEADS_PER_GROUP + r
            rows = slice(h * SSD_HEAD_DIM, (h + 1) * SSD_HEAD_DIM)
            s_ref[rows, :] = (s_ref[rows, :] * jnp.exp(acum_t[h:h + 1, L - 1:L])
                              + chunk_state[r * SSD_HEAD_DIM:(r + 1) * SSD_HEAD_DIM])
        gcols = slice(g * GROUP_COLS, (g + 1) * GROUP_COLS)
        yg = y_ref[:, gcols]
        y_ref[:, gcols] = yg * lax.rsqrt(jnp.mean(yg * yg, axis=-1, keepdims=True) + RMS_EPS) * nw_ref[:, gcols]

    @pl.when(c == n_chunks - 1)
    def _():
        state_ref[...] = s_ref[...]


def _ssd_mix(proj, conv_prev, ssm_prev, conv_w, conv_b, dt_bias, a_log, d_skip, norm_w, row_block0, rows_in, n_chunks):
    n_seq = conv_prev.shape[0]

    def rows_spec(width, col_block):
        return pl.BlockSpec((rows_in, width), lambda b, c: (row_block0 + b * n_chunks + c, col_block))

    def const_spec(shape):
        return pl.BlockSpec(shape, lambda b, c: (0,) * len(shape))

    pad_lanes = lambda v: jnp.pad(v, (0, LANE - v.shape[0])).reshape(1, LANE)
    return pl.pallas_call(
        functools.partial(_ssd_kernel, rows_in=rows_in, n_chunks=n_chunks),
        grid=(n_seq, n_chunks),
        in_specs=[
            rows_spec(D_INNER, 0),
            rows_spec(D_INNER, 1),
            rows_spec(GN, 2 * D_INNER // GN),
            rows_spec(GN, 2 * D_INNER // GN + 1),
            rows_spec(LANE, (D_INNER + CONV_DIM) // LANE),
            pl.BlockSpec((None, CARRY_ROWS, CONV_DIM), lambda b, c: (b, 0, 0)),
            pl.BlockSpec((None, D_INNER, D_STATE), lambda b, c: (b, 0, 0)),
            const_spec((CONV_W, CONV_DIM)),
            const_spec((1, CONV_DIM)),
            const_spec((1, LANE)),
            const_spec((1, LANE)),
            const_spec((1, D_INNER)),
            const_spec((1, D_INNER)),
        ],
        out_specs=[
            pl.BlockSpec((rows_in, D_INNER), lambda b, c: (b * n_chunks + c, 0)),
            pl.BlockSpec((None, CONV_W - 1, CONV_DIM), lambda b, c: (b, 0, 0)),
            pl.BlockSpec((None, D_INNER, D_STATE), lambda b, c: (b, 0, 0)),
        ],
        out_shape=[
            jax.ShapeDtypeStruct((n_seq * n_chunks * rows_in, D_INNER), jnp.float32),
            jax.ShapeDtypeStruct((n_seq, CONV_W - 1, CONV_DIM), jnp.float32),
            jax.ShapeDtypeStruct((n_seq, D_INNER, D_STATE), jnp.float32),
        ],
        scratch_shapes=[
            pltpu.VMEM((CARRY_ROWS, CONV_DIM), jnp.float32),
            pltpu.VMEM((D_INNER, D_STATE), jnp.float32),
            pltpu.VMEM((SSD_CHUNK, CONV_DIM), jnp.float32),
        ],
        compiler_params=pltpu.CompilerParams(
            dimension_semantics=("parallel", "arbitrary"), vmem_limit_bytes=VMEM_LIMIT),
    )(proj, proj, proj, proj, proj, conv_prev, ssm_prev, conv_w, conv_b.reshape(1, CONV_DIM),
      pad_lanes(dt_bias), pad_lanes(a_log), jnp.repeat(d_skip, SSD_HEAD_DIM).reshape(1, D_INNER),
      norm_w.reshape(1, D_INNER))


def _pad_cols(w, n):
    return jnp.pad(w, ((0, 0), (0, n - w.shape[1])))


def _split_rows(a):
    return (a[:M_PROMPT].reshape(BATCH, SEQ, a.shape[-1]),
            a[M_PROMPT:].reshape(DEC_BATCH, DEC_SEQ, a.shape[-1]))


def _join_rows(p, s):
    return jnp.concatenate([p.reshape(M_PROMPT, p.shape[-1]), s.reshape(M_SAMPLE, s.shape[-1])], axis=0)


def kernel(x_prompt, x_sample, state_ssm, state_conv, cache_nsa_kv, cache_nsa_win, cache_moba_kv, page_table, rel_bias, norm_w, final_norm_w, ffn_w_gate, ffn_w_up, ffn_w_down, ssd_w_in, ssd_conv_w, ssd_conv_b, ssd_dt_bias, ssd_a_log, ssd_d, ssd_norm_w, ssd_w_out, nsa_w_in, nsa_cmp_pe, nsa_cmp_w, nsa_w_out, moba_w_in, moba_w_out):
    bf = jnp.bfloat16
    h = _join_rows(x_prompt, x_sample)
    ssm_p, ssm_s, conv_p, conv_s = [], [], [], []
    nsa_kv_p, nsa_kv_s, nsa_win_p, nsa_win_s = [], [], [], []
    moba_kv_p, moba_kv_s = [], []
    for i in range(DEPTH):
        kind, j = i % N_MIXERS, i // N_MIXERS
        h = _ffn_half(h, norm_w[i, 0], ffn_w_gate[i, 0].astype(bf), ffn_w_up[i, 0].astype(bf), ffn_w_down[i, 0].astype(bf))
        if kind == 0:
            n_pad = -(-SSD_IN_DIM // 512) * 512
            proj = _norm_proj(h, norm_w[i, 1], _pad_cols(ssd_w_in[j], n_pad).astype(bf))
            w = (ssd_conv_w[j], ssd_conv_b[j], ssd_dt_bias[j], ssd_a_log[j], ssd_d[j], ssd_norm_w[j])
            conv0 = jnp.zeros((BATCH, CARRY_ROWS, CONV_DIM), jnp.float32)
            ssm0 = jnp.zeros((BATCH, D_INNER, D_STATE), jnp.float32)
            y_p, c_p, s_p = _ssd_mix(proj, conv0, ssm0, *w, 0, SSD_CHUNK, SEQ // SSD_CHUNK)
            conv1 = jnp.pad(state_conv[j], ((0, 0), (CARRY_ROWS - (CONV_W - 1), 0), (0, 0)))
            ssm1 = state_ssm[j].reshape(DEC_BATCH, D_INNER, D_STATE)
            y_s, c_s, s_s = _ssd_mix(proj, conv1, ssm1, *w, M_PROMPT // DEC_SEQ, DEC_SEQ, 1)
            conv_p.append(c_p)
            conv_s.append(c_s)
            ssm_p.append(s_p.reshape(BATCH, SSD_HEADS, SSD_HEAD_DIM, D_STATE))
            ssm_s.append(s_s.reshape(DEC_BATCH, SSD_HEADS, SSD_HEAD_DIM, D_STATE))
            w_out = ssd_w_out[j]
        elif kind == 1:
            n_pad = -(-NSA_IN_DIM // 512) * 512
            proj = _norm_proj(h, norm_w[i, 1], _pad_cols(nsa_w_in[j], n_pad).astype(bf))
            proj_p, proj_s = _split_rows(proj)
            kc, vc = _nsa_compress_prompt(proj, nsa_cmp_pe[j], nsa_cmp_w[j].astype(bf), BATCH, SEQ)
            y_p = _nsa_attention(proj, kc, vc, rel_bias, BATCH, SEQ)
            kv_all = proj_p[..., Q_DIM:Q_DIM + 6 * KV_DIM].reshape(BATCH, SEQ, 6, N_KV_HEADS, HEAD_DIM)
            kv_p, win_p = kv_all[:, :, :4], kv_all[:, SEQ - WINDOW:, 4:]
            cmp_w_bf = nsa_cmp_w[j].astype(bf)
            y_s = _nsa_sample(proj, cache_nsa_kv[j], cache_nsa_win[j], page_table, nsa_cmp_pe[j], cmp_w_bf,
                              rel_bias, M_PROMPT // DEC_SEQ, PAST_LEN)
            kv_new = proj_s[..., Q_DIM:Q_DIM + 6 * KV_DIM].reshape(DEC_BATCH, DEC_SEQ, 6, N_KV_HEADS, HEAD_DIM)
            kv_s = kv_new[:, :, :4]
            win_s = jnp.concatenate([cache_nsa_win[j][:, DEC_SEQ:], kv_new[:, :, 4:]], axis=1)
            nsa_kv_p.append(kv_p)
            nsa_kv_s.append(kv_s)
            nsa_win_p.append(win_p)
            nsa_win_s.append(win_s)
            w_out = nsa_w_out[j]
        else:
            proj = _norm_proj(h, norm_w[i, 1], moba_w_in[j].astype(bf))
            proj_p, proj_s = _split_rows(proj)
            y_p = _moba_attention(proj, rel_bias, BATCH, SEQ)
            kv_p = proj_p[..., Q_DIM:].reshape(BATCH, SEQ, 2, N_KV_HEADS, HEAD_DIM)
            kmean = _moba_sample_kmean(cache_moba_kv[j], page_table)
            y_s = _moba_sample_attention(proj, jnp.swapaxes(kmean, 1, 2), cache_moba_kv[j], page_table, rel_bias,
                                         M_PROMPT // DEC_SEQ, PAST_LEN)
            kv_s = proj_s[..., Q_DIM:].reshape(DEC_BATCH, DEC_SEQ, 2, N_KV_HEADS, HEAD_DIM)
            moba_kv_p.append(kv_p)
            moba_kv_s.append(kv_s)
            w_out = moba_w_out[j]
        h = _proj_residual(_join_rows(y_p, y_s), w_out.astype(bf), h)
        h = _ffn_half(h, norm_w[i, 2], ffn_w_gate[i, 1].astype(bf), ffn_w_up[i, 1].astype(bf), ffn_w_down[i, 1].astype(bf))
    y = _final_norm(h, final_norm_w)
    y_prompt, y_sample = _split_rows(y)
    return (y_prompt, y_sample,
            jnp.stack(ssm_p), jnp.stack(ssm_s), jnp.stack(conv_p), jnp.stack(conv_s),
            jnp.stack(nsa_kv_p), jnp.stack(nsa_kv_s), jnp.stack(nsa_win_p), jnp.stack(nsa_win_s),
            jnp.stack(moba_kv_p), jnp.stack(moba_kv_s))
```

```python
import functools
import math

import jax
import jax.numpy as jnp
import numpy as np
from jax import lax
from jax.experimental import pallas as pl
from jax.experimental.pallas import tpu as pltpu

D_MODEL = 2048
BATCH = 2
SEQ = 4096
DEPTH = 4
DEC_BATCH = 8
DEC_SEQ = 8
PAST_LEN = 16384
PAGE_SIZE = 128
N_MIXERS = 3
D_FF = 5632
RMS_EPS = 1e-6
D_INNER = 2 * D_MODEL
SSD_HEAD_DIM = 64
SSD_HEADS = D_INNER // SSD_HEAD_DIM
D_STATE = 128
SSD_GROUPS = 8
HEADS_PER_GROUP = SSD_HEADS // SSD_GROUPS
GN = SSD_GROUPS * D_STATE
CONV_W = 4
CONV_DIM = D_INNER + 2 * GN
SSD_IN_DIM = D_INNER + CONV_DIM + SSD_HEADS
SSD_CHUNK = 128
HEAD_DIM = 128
N_HEADS = D_MODEL // HEAD_DIM
N_KV_HEADS = 4
Q_PER_KV = N_HEADS // N_KV_HEADS
Q_DIM = N_HEADS * HEAD_DIM
KV_DIM = N_KV_HEADS * HEAD_DIM
Q_BLOCK = 64
CMP_BLOCK = 32
CMP_STRIDE = 16
SEL_BLOCK = 64
SEL_TOPK = 16
N_LOCAL_SEL = 2
WINDOW = 512
NSA_IN_DIM = Q_DIM + 6 * KV_DIM + 3 * N_HEADS
MOBA_BLOCK = 256
MOBA_TOPK = 3
MOBA_IN_DIM = Q_DIM + 2 * KV_DIM
REL_BUCKETS = 32
REL_MAX_EXACT = REL_BUCKETS // 2
REL_MAX_DIST = 128
NEG_INF = -1e30
FORCE_SCORE = 1e9

M_PROMPT = BATCH * SEQ
M_SAMPLE = DEC_BATCH * DEC_SEQ
M_TOTAL = M_PROMPT + M_SAMPLE
ROW_TILE = 688
LANE = 128
VMEM_LIMIT = 56 * 1024 * 1024


def _rms_rows(x, g):
    ms = jnp.mean(x * x, axis=-1, keepdims=True)
    return x * lax.rsqrt(ms + RMS_EPS) * g


def _ffn_kernel(h_ref, g_ref, wg_ref, wu_ref, wd_ref, o_ref, xn_ref, acc_ref):
    f = pl.program_id(1)

    @pl.when(f == 0)
    def _():
        xn_ref[...] = _rms_rows(h_ref[...], g_ref[...]).astype(jnp.bfloat16)
        acc_ref[...] = jnp.zeros_like(acc_ref)

    xn = xn_ref[...]
    gate = jnp.dot(xn, wg_ref[...], preferred_element_type=jnp.float32)
    up = jnp.dot(xn, wu_ref[...], preferred_element_type=jnp.float32)
    act = (gate * jax.nn.sigmoid(gate) * up).astype(jnp.bfloat16)
    acc_ref[...] += jnp.dot(act, wd_ref[...], preferred_element_type=jnp.float32)

    @pl.when(f == pl.num_programs(1) - 1)
    def _():
        o_ref[...] = h_ref[...] + 0.5 * acc_ref[...]


def _ffn_half(h, g, wg, wu, wd, *, ff_tile=512):
    m, d = h.shape
    dff = wg.shape[1]
    grid = (m // ROW_TILE, dff // ff_tile)
    return pl.pallas_call(
        _ffn_kernel,
        grid=grid,
        in_specs=[
            pl.BlockSpec((ROW_TILE, d), lambda i, f: (i, 0)),
            pl.BlockSpec((1, d), lambda i, f: (0, 0)),
            pl.BlockSpec((d, ff_tile), lambda i, f: (0, f)),
            pl.BlockSpec((d, ff_tile), lambda i, f: (0, f)),
            pl.BlockSpec((ff_tile, d), lambda i, f: (f, 0)),
        ],
        out_specs=pl.BlockSpec((ROW_TILE, d), lambda i, f: (i, 0)),
        out_shape=jax.ShapeDtypeStruct((m, d), jnp.float32),
        scratch_shapes=[
            pltpu.VMEM((ROW_TILE, d), jnp.bfloat16),
            pltpu.VMEM((ROW_TILE, d), jnp.float32),
        ],
        compiler_params=pltpu.CompilerParams(
            dimension_semantics=("parallel", "arbitrary"),
            vmem_limit_bytes=VMEM_LIMIT),
    )(h, g.reshape(1, d), wg, wu, wd)


def _norm_proj_kernel(h_ref, g_ref, w_ref, o_ref, xn_ref):
    @pl.when(pl.program_id(1) == 0)
    def _():
        xn_ref[...] = _rms_rows(h_ref[...], g_ref[...]).astype(jnp.bfloat16)

    o_ref[...] = jnp.dot(xn_ref[...], w_ref[...], preferred_element_type=jnp.float32)


PROJ_TILE_MAX = 1536


def _norm_proj(h, g, w):
    m, d = h.shape
    n = w.shape[1]
    n_tile = max(t for t in range(LANE, PROJ_TILE_MAX + 1, LANE) if n % t == 0)
    grid = (m // ROW_TILE, n // n_tile)
    return pl.pallas_call(
        _norm_proj_kernel,
        grid=grid,
        in_specs=[
            pl.BlockSpec((ROW_TILE, d), lambda i, j: (i, 0)),
            pl.BlockSpec((1, d), lambda i, j: (0, 0)),
            pl.BlockSpec((d, n_tile), lambda i, j: (0, j)),
        ],
        out_specs=pl.BlockSpec((ROW_TILE, n_tile), lambda i, j: (i, j)),
        out_shape=jax.ShapeDtypeStruct((m, n), jnp.float32),
        scratch_shapes=[pltpu.VMEM((ROW_TILE, d), jnp.bfloat16)],
        compiler_params=pltpu.CompilerParams(
            dimension_semantics=("parallel", "arbitrary"),
            vmem_limit_bytes=VMEM_LIMIT),
    )(h, g.reshape(1, d), w)


def _proj_residual_kernel(y_ref, w_ref, h_ref, o_ref):
    o_ref[...] = h_ref[...] + jnp.dot(
        y_ref[...].astype(jnp.bfloat16), w_ref[...], preferred_element_type=jnp.float32)


def _proj_residual(y, w, h, *, n_tile=512):
    m, k = y.shape
    d = w.shape[1]
    grid = (m // ROW_TILE, d // n_tile)
    return pl.pallas_call(
        _proj_residual_kernel,
        grid=grid,
        in_specs=[
            pl.BlockSpec((ROW_TILE, k), lambda i, j: (i, 0)),
            pl.BlockSpec((k, n_tile), lambda i, j: (0, j)),
            pl.BlockSpec((ROW_TILE, n_tile), lambda i, j: (i, j)),
        ],
        out_specs=pl.BlockSpec((ROW_TILE, n_tile), lambda i, j: (i, j)),
        out_shape=jax.ShapeDtypeStruct((m, d), jnp.float32),
        compiler_params=pltpu.CompilerParams(
            dimension_semantics=("parallel", "arbitrary"),
            vmem_limit_bytes=VMEM_LIMIT),
    )(y, w, h)


def _final_norm_kernel(h_ref, g_ref, o_ref):
    o_ref[...] = _rms_rows(h_ref[...], g_ref[...])


def _final_norm(h, g):
    m, d = h.shape
    return pl.pallas_call(
        _final_norm_kernel,
        grid=(m // ROW_TILE,),
        in_specs=[pl.BlockSpec((ROW_TILE, d), lambda i: (i, 0)),
                  pl.BlockSpec((1, d), lambda i: (0, 0))],
        out_specs=pl.BlockSpec((ROW_TILE, d), lambda i: (i, 0)),
        out_shape=jax.ShapeDtypeStruct((m, d), jnp.float32),
        compiler_params=pltpu.CompilerParams(dimension_semantics=("parallel",)),
    )(h, g.reshape(1, d))


ATT_TILE = 256
HEAD_ROWS = Q_PER_KV * ATT_TILE
SCALE = HEAD_DIM ** -0.5
LOG2E = math.log2(math.e)
CMP_BAND_START = LANE - ATT_TILE // CMP_STRIDE
_NT = (((1,), (1,)), ((), ()))


def _bucket_upper_bounds():
    n = np.arange(0, 2 * REL_MAX_DIST)
    ratio = np.log(np.maximum(n, 1).astype(np.float32) / np.float32(REL_MAX_EXACT)) / np.float32(
        math.log(REL_MAX_DIST / REL_MAX_EXACT))
    large = np.minimum(REL_MAX_EXACT + (ratio * np.float32(REL_BUCKETS - REL_MAX_EXACT)).astype(np.int32),
                       REL_BUCKETS - 1)
    bucket = np.where(n < REL_MAX_EXACT, n, large)
    return [int(n[bucket > b].min()) for b in range(REL_BUCKETS - 1)]


_BUCKET_HI = _bucket_upper_bounds()


def _bias_from_dist(dist, head_vals):
    accs = [jnp.zeros(dist.shape, jnp.float32) + vals[REL_BUCKETS - 1] for vals in head_vals]
    for b in range(REL_BUCKETS - 2, -1, -1):
        near = dist < _BUCKET_HI[b]
        accs = [jnp.where(near, vals[b], acc) for vals, acc in zip(head_vals, accs)]
    return accs


def _head_vals(rb_ref, g):
    return [[rb_ref[b, g * Q_PER_KV + r] for b in range(REL_BUCKETS)] for r in range(Q_PER_KV)]


def _build_bias_tiles_t(rb_ref, g, bias_ref):
    key = lax.broadcasted_iota(jnp.int32, (ATT_TILE, ATT_TILE), 0)
    qry = lax.broadcasted_iota(jnp.int32, (ATT_TILE, ATT_TILE), 1)
    vals = _head_vals(rb_ref, g)
    for kind in range(2):
        tiles = _bias_from_dist(qry - key + kind * ATT_TILE, vals)
        for r in range(Q_PER_KV):
            bias_ref[kind, :, r * ATT_TILE:(r + 1) * ATT_TILE] = tiles[r] * LOG2E
    for r in range(Q_PER_KV):
        bias_ref[2, :, r * ATT_TILE:(r + 1) * ATT_TILE] = (
            jnp.zeros((ATT_TILE, ATT_TILE), jnp.float32) + vals[r][REL_BUCKETS - 1] * LOG2E)


def _flash_t(qs, k_tile, v_tile_t, bias, mask, state):
    s = lax.dot_general(k_tile, qs, _NT, preferred_element_type=jnp.float32) * (SCALE * LOG2E) + bias
    if mask is not None:
        s = jnp.where(mask, s, NEG_INF)
    tile_max = jnp.max(s, axis=0, keepdims=True)
    if state is None:
        m_new = tile_max
        p = jnp.exp2(s - m_new)
        l = jnp.sum(p, axis=0, keepdims=True)
        acc = jnp.dot(v_tile_t, p.astype(jnp.bfloat16), preferred_element_type=jnp.float32)
    else:
        m, l, acc = state
        m_new = jnp.maximum(m, tile_max)
        alpha = jnp.exp2(m - m_new)
        p = jnp.exp2(s - m_new)
        l = alpha * l + jnp.sum(p, axis=0, keepdims=True)
        acc = alpha * acc + jnp.dot(v_tile_t, p.astype(jnp.bfloat16), preferred_element_type=jnp.float32)
    return m_new, l, acc


def _kv_tile_t(ref, j):
    return ref[pl.ds(pl.multiple_of(j * ATT_TILE, ATT_TILE), ATT_TILE), :].T.astype(jnp.bfloat16)


def _store_heads_t(o_ref, out_t):
    for r in range(Q_PER_KV):
        o_ref[:, r * HEAD_DIM:(r + 1) * HEAD_DIM] = out_t[:, r * ATT_TILE:(r + 1) * ATT_TILE].T


def _stack_heads(q_ref):
    return jnp.concatenate(
        [q_ref[:, r * HEAD_DIM:(r + 1) * HEAD_DIM] for r in range(Q_PER_KV)], axis=0).astype(jnp.bfloat16)


def _tile4(x):
    return jnp.concatenate([x] * Q_PER_KV, axis=0)


def _flash_first(qs, k_tile, v_tile, bias, mask):
    s = lax.dot_general(qs, k_tile, _NT, preferred_element_type=jnp.float32) * SCALE + bias
    s = jnp.where(mask, s, NEG_INF)
    m = jnp.max(s, axis=-1, keepdims=True)
    p = jnp.exp(s - m)
    l = jnp.sum(p, axis=-1, keepdims=True)
    acc = jnp.dot(p.astype(jnp.bfloat16), v_tile, preferred_element_type=jnp.float32)
    return m, l, acc


def _flash_next(qs, k_tile, v_tile, bias, mask, m, l, acc):
    s = lax.dot_general(qs, k_tile, _NT, preferred_element_type=jnp.float32) * SCALE + bias
    s = jnp.where(mask, s, NEG_INF)
    m_new = jnp.maximum(m, jnp.max(s, axis=-1, keepdims=True))
    alpha = jnp.exp(m - m_new)
    p = jnp.exp(s - m_new)
    l = alpha * l + jnp.sum(p, axis=-1, keepdims=True)
    acc = alpha * acc + jnp.dot(p.astype(jnp.bfloat16), v_tile, preferred_element_type=jnp.float32)
    return m_new, l, acc


def _kv_tile(ref, j):
    return ref[pl.ds(pl.multiple_of(j * ATT_TILE, ATT_TILE), ATT_TILE), :].astype(jnp.bfloat16)


def _rank_select(score, blk, top_k):
    cnt = jnp.zeros(score.shape, jnp.float32)
    for m in range(score.shape[0]):
        row = score[m:m + 1, :]
        beats = jnp.where(row > score, 1.0, jnp.where((row == score) & (blk > m), 1.0, 0.0))
        cnt = cnt + beats
    return cnt < top_k


def _moba_kernel(rb_ref, q_ref, k_ref, v_ref, o_ref, bias_ref, kmean_ref, sel_ref, *, n_tiles):
    g = pl.program_id(1)
    t = pl.program_id(2)

    @pl.when(t == 0)
    def _():
        _build_bias_tiles_t(rb_ref, g, bias_ref)
        kmean_ref[...] = jnp.zeros_like(kmean_ref)
        for n in range(n_tiles):
            kmean_ref[n:n + 1, :] = jnp.mean(k_ref[n * ATT_TILE:(n + 1) * ATT_TILE, :], axis=0, keepdims=True)

    qs = _stack_heads(q_ref)
    gate = lax.dot_general(kmean_ref[...].astype(jnp.bfloat16), qs, _NT, preferred_element_type=jnp.float32)
    blk = lax.broadcasted_iota(jnp.int32, gate.shape, 0)
    past = blk < t
    gate = jnp.where(past, gate, NEG_INF)
    chosen = _rank_select(gate, blk, MOBA_TOPK) & past
    sel_ref[...] = jnp.where(chosen, 1.0, 0.0)

    key = lax.broadcasted_iota(jnp.int32, (ATT_TILE, HEAD_ROWS), 0)
    qry = lax.broadcasted_iota(jnp.int32, (ATT_TILE, HEAD_ROWS), 1) & (ATT_TILE - 1)
    state = _flash_t(qs, _kv_tile(k_ref, t), _kv_tile_t(v_ref, t), bias_ref[0], key <= qry, None)

    def picked(j):
        return sel_ref[pl.ds(j, 1), :] > 0.5

    def far_body(j, state):
        bias = jnp.where(picked(j), bias_ref[2, 0:1, :], NEG_INF)
        return _flash_t(qs, _kv_tile(k_ref, j), _kv_tile_t(v_ref, j), bias, None, state)

    def prev_body(j, state):
        return _flash_t(qs, _kv_tile(k_ref, j), _kv_tile_t(v_ref, j), bias_ref[1], picked(j), state)

    prev = jnp.maximum(t - 1, 0)
    state = lax.fori_loop(0, prev, far_body, state)
    m, l, acc = lax.fori_loop(prev, t, prev_body, state)
    _store_heads_t(o_ref, acc / jnp.maximum(l, 1e-30))


def _moba_attention(proj, rel_bias, n_batch, seq):
    n_tiles = seq // ATT_TILE
    k_col = Q_DIM // HEAD_DIM
    v_col = (Q_DIM + KV_DIM) // HEAD_DIM
    q_cols = Q_PER_KV * HEAD_DIM
    return pl.pallas_call(
        functools.partial(_moba_kernel, n_tiles=n_tiles),
        grid=(n_batch, N_KV_HEADS, n_tiles),
        in_specs=[
            pl.BlockSpec(memory_space=pltpu.SMEM),
            pl.BlockSpec((ATT_TILE, q_cols), lambda b, g, t: (b * n_tiles + t, g)),
            pl.BlockSpec((seq, HEAD_DIM), lambda b, g, t: (b, k_col + g)),
            pl.BlockSpec((seq, HEAD_DIM), lambda b, g, t: (b, v_col + g)),
        ],
        out_specs=pl.BlockSpec((ATT_TILE, q_cols), lambda b, g, t: (b * n_tiles + t, g)),
        out_shape=jax.ShapeDtypeStruct((n_batch * seq, Q_DIM), jnp.float32),
        scratch_shapes=[
            pltpu.VMEM((3, ATT_TILE, HEAD_ROWS), jnp.float32),
            pltpu.VMEM((max(n_tiles, 16), HEAD_DIM), jnp.float32),
            pltpu.VMEM((max(n_tiles, 16), HEAD_ROWS), jnp.float32),
        ],
        compiler_params=pltpu.CompilerParams(
            dimension_semantics=("parallel", "parallel", "arbitrary"),
            vmem_limit_bytes=VMEM_LIMIT),
    )(rel_bias, proj, proj, proj)


def _nsa_compress_kernel(k_ref, v_ref, pe_ref, w_ref, kc_ref, vc_ref, *, n_cmp):
    for c, (x_ref, o_ref) in enumerate(((k_ref, kc_ref), (v_ref, vc_ref))):
        top = jnp.zeros((n_cmp, HEAD_DIM), jnp.float32)
        bot = jnp.zeros((n_cmp, HEAD_DIM), jnp.float32)
        for r in range(CMP_STRIDE):
            x = x_ref[pl.ds(r, n_cmp, stride=CMP_STRIDE), :]
            xt = (x + pe_ref[c, r:r + 1, :]).astype(jnp.bfloat16)
            xb = (x + pe_ref[c, CMP_STRIDE + r:CMP_STRIDE + r + 1, :]).astype(jnp.bfloat16)
            top = top + jnp.dot(xt, w_ref[c, r * HEAD_DIM:(r + 1) * HEAD_DIM, :], preferred_element_type=jnp.float32)
            bot = bot + jnp.dot(xb, w_ref[c, (CMP_STRIDE + r) * HEAD_DIM:(CMP_STRIDE + r + 1) * HEAD_DIM, :],
                                preferred_element_type=jnp.float32)
        o_ref[0, 0] = top + pltpu.roll(bot, n_cmp - 1, axis=0)


def _nsa_compress_prompt(proj, cmp_pe, cmp_w, n_batch, seq):
    n_cmp = seq // CMP_STRIDE
    k_col = Q_DIM // HEAD_DIM
    v_col = (Q_DIM + KV_DIM) // HEAD_DIM
    out = jax.ShapeDtypeStruct((n_batch, N_KV_HEADS, n_cmp, HEAD_DIM), jnp.float32)
    return pl.pallas_call(
        functools.partial(_nsa_compress_kernel, n_cmp=n_cmp),
        grid=(n_batch, N_KV_HEADS),
        in_specs=[
            pl.BlockSpec((seq, HEAD_DIM), lambda b, g: (b, k_col + g)),
            pl.BlockSpec((seq, HEAD_DIM), lambda b, g: (b, v_col + g)),
            pl.BlockSpec((2, CMP_BLOCK, HEAD_DIM), lambda b, g: (0, 0, 0)),
            pl.BlockSpec((2, CMP_BLOCK * HEAD_DIM, HEAD_DIM), lambda b, g: (0, 0, 0)),
        ],
        out_specs=[pl.BlockSpec((1, 1, n_cmp, HEAD_DIM), lambda b, g: (b, g, 0, 0))] * 2,
        out_shape=[out, out],
        compiler_params=pltpu.CompilerParams(
            dimension_semantics=("parallel", "parallel"), vmem_limit_bytes=VMEM_LIMIT),
    )(proj, proj, cmp_pe, cmp_w)


def _nsa_kernel_t(rb_ref, q_ref, gate_ref, kc_ref, vc_ref, ks_ref, vs_ref, kw_ref, vw_ref, ovt_ref, o_ref,
                  bias_ref, cbias_ref, *, n_cmp, n_sel):
    g = pl.program_id(1)
    t = pl.program_id(2)
    tile_tokens = ATT_TILE // CMP_STRIDE
    blocks_per_tile = ATT_TILE // SEL_BLOCK

    @pl.when(t == 0)
    def _():
        _build_bias_tiles_t(rb_ref, g, bias_ref)
        m = lax.broadcasted_iota(jnp.int32, (LANE, ATT_TILE), 0)
        i = lax.broadcasted_iota(jnp.int32, (LANE, ATT_TILE), 1)
        vals = _head_vals(rb_ref, g)
        band = _bias_from_dist(i - CMP_STRIDE * (m - CMP_BAND_START) - (CMP_BLOCK - 1), vals)
        for r in range(Q_PER_KV):
            cols = slice(r * ATT_TILE, (r + 1) * ATT_TILE)
            far = jnp.zeros((n_cmp, ATT_TILE), jnp.float32) + vals[r][REL_BUCKETS - 1]
            cbias_ref[0:n_cmp, cols] = far
            cbias_ref[n_cmp:n_cmp + LANE, cols] = band[r]
            cbias_ref[n_cmp + LANE:2 * n_cmp + LANE, cols] = far

    qs = _stack_heads(q_ref)
    key = lax.broadcasted_iota(jnp.int32, (ATT_TILE, HEAD_ROWS), 0)
    qry = lax.broadcasted_iota(jnp.int32, (ATT_TILE, HEAD_ROWS), 1) & (ATT_TILE - 1)

    start = pl.multiple_of(n_cmp + CMP_BAND_START - t * tile_tokens, tile_tokens)
    c_end = lax.broadcasted_iota(jnp.int32, (n_cmp, HEAD_ROWS), 0) * CMP_STRIDE + (CMP_BLOCK - 1)
    q_pos = t * ATT_TILE + (lax.broadcasted_iota(jnp.int32, (n_cmp, HEAD_ROWS), 1) & (ATT_TILE - 1))
    ok_c = c_end <= q_pos
    lc = lax.dot_general(kc_ref[0, 0].astype(jnp.bfloat16), qs, _NT, preferred_element_type=jnp.float32) * SCALE
    lc = jnp.where(ok_c, lc + cbias_ref[pl.ds(start, n_cmp), :], NEG_INF)
    e = jnp.where(ok_c, jnp.exp(lc - jnp.max(lc, axis=0, keepdims=True)), 0.0)
    pc = e / jnp.maximum(jnp.sum(e, axis=0, keepdims=True), 1e-30)
    o_cmp = jnp.dot(vc_ref[0, 0].T.astype(jnp.bfloat16), pc.astype(jnp.bfloat16), preferred_element_type=jnp.float32)

    p_sum = pc[:, 0:ATT_TILE]
    for r in range(1, Q_PER_KV):
        p_sum = p_sum + pc[:, r * ATT_TILE:(r + 1) * ATT_TILE]
    imp = jnp.dot(ovt_ref[...], p_sum.astype(jnp.bfloat16), preferred_element_type=jnp.float32)[:n_sel]
    blk = lax.broadcasted_iota(jnp.int32, (n_sel, ATT_TILE), 0)
    own = t * blocks_per_tile + (lax.broadcasted_iota(jnp.int32, (n_sel, ATT_TILE), 1) // SEL_BLOCK)
    diff = own - blk
    valid = diff >= 0
    forced = valid & ((blk == 0) | (diff < N_LOCAL_SEL))
    score = jnp.where(forced, FORCE_SCORE, jnp.where(valid, imp, NEG_INF))
    chosen = jnp.where(_rank_select(score, blk, SEL_TOPK) & valid, 1.0, 0.0)
    if n_sel < LANE:
        chosen = jnp.concatenate([chosen, jnp.zeros((LANE - n_sel, ATT_TILE), jnp.float32)], axis=0)
    chosen = chosen.astype(jnp.bfloat16)

    def sel_mask(j):
        pick = jnp.where(
            lax.broadcasted_iota(jnp.int32, (ATT_TILE, LANE), 1)
            == j * blocks_per_tile + lax.broadcasted_iota(jnp.int32, (ATT_TILE, LANE), 0) // SEL_BLOCK,
            1.0, 0.0).astype(jnp.bfloat16)
        return _tile4_lanes(jnp.dot(pick, chosen, preferred_element_type=jnp.float32)) > 0.5

    causal = key <= qry
    state = _flash_t(qs, _kv_tile(ks_ref, t), _kv_tile_t(vs_ref, t), bias_ref[0], sel_mask(t) & causal, None)

    def far_body(j, state):
        return _flash_t(qs, _kv_tile(ks_ref, j), _kv_tile_t(vs_ref, j), bias_ref[2, 0:1, :], sel_mask(j), state)

    def prev_body(j, state):
        return _flash_t(qs, _kv_tile(ks_ref, j), _kv_tile_t(vs_ref, j), bias_ref[1], sel_mask(j), state)

    prev = jnp.maximum(t - 1, 0)
    state = lax.fori_loop(0, prev, far_body, state)
    m, l, acc = lax.fori_loop(prev, t, prev_body, state)
    o_sel = acc / jnp.maximum(l, 1e-30)

    state = _flash_t(qs, _kv_tile(kw_ref, t), _kv_tile_t(vw_ref, t), bias_ref[0], causal, None)
    state = _flash_t(qs, _kv_tile(kw_ref, prev), _kv_tile_t(vw_ref, prev), bias_ref[1], (key >= 0) & (t >= 1), state)
    t2 = jnp.maximum(t - 2, 0)
    m, l, acc = _flash_t(qs, _kv_tile(kw_ref, t2), _kv_tile_t(vw_ref, t2), bias_ref[2, 0:1, :],
                         (key > qry) & (t >= 2), state)
    o_win = acc / jnp.maximum(l, 1e-30)

    gates = jax.nn.sigmoid(gate_ref[...]).T
    gate_row = lax.broadcasted_iota(jnp.int32, gates.shape, 0)
    for r in range(Q_PER_KV):
        cols = slice(r * ATT_TILE, (r + 1) * ATT_TILE)
        out = jnp.zeros((HEAD_DIM, ATT_TILE), jnp.float32)
        for c, branch in enumerate((o_cmp, o_sel, o_win)):
            gvec = jnp.sum(jnp.where(gate_row == g * (3 * Q_PER_KV) + r * 3 + c, gates, 0.0), axis=0, keepdims=True)
            out = out + gvec * branch[:, cols]
        o_ref[:, r * HEAD_DIM:(r + 1) * HEAD_DIM] = out.T


def _tile4_lanes(x):
    return jnp.concatenate([x] * Q_PER_KV, axis=1)


def _nsa_overlap_t(n_cmp):
    ci = np.arange(n_cmp)[None, :]
    sj = np.arange(LANE)[:, None]
    ov = (ci * CMP_STRIDE < (sj + 1) * SEL_BLOCK) & (ci * CMP_STRIDE + CMP_BLOCK > sj * SEL_BLOCK) & (ci < n_cmp - 1)
    return jnp.asarray(ov, jnp.bfloat16)


def _nsa_attention(proj, kc, vc, rel_bias, n_batch, seq):
    n_tiles = seq // ATT_TILE
    n_cmp = seq // CMP_STRIDE
    q_cols = Q_PER_KV * HEAD_DIM
    kv0 = Q_DIM // HEAD_DIM
    per = KV_DIM // HEAD_DIM
    gate_col = (Q_DIM + 6 * KV_DIM) // LANE

    def kv_spec(comp):
        return pl.BlockSpec((seq, HEAD_DIM), lambda b, g, t: (b, kv0 + comp * per + g))

    cmp_spec = pl.BlockSpec((1, 1, n_cmp, HEAD_DIM), lambda b, g, t: (b, g, 0, 0))
    return pl.pallas_call(
        functools.partial(_nsa_kernel_t, n_cmp=n_cmp, n_sel=seq // SEL_BLOCK),
        grid=(n_batch, N_KV_HEADS, n_tiles),
        in_specs=[
            pl.BlockSpec(memory_space=pltpu.SMEM),
            pl.BlockSpec((ATT_TILE, q_cols), lambda b, g, t: (b * n_tiles + t, g)),
            pl.BlockSpec((ATT_TILE, LANE), lambda b, g, t: (b * n_tiles + t, gate_col)),
            cmp_spec, cmp_spec,
            kv_spec(2), kv_spec(3), kv_spec(4), kv_spec(5),
            pl.BlockSpec((LANE, n_cmp), lambda b, g, t: (0, 0)),
        ],
        out_specs=pl.BlockSpec((ATT_TILE, q_cols), lambda b, g, t: (b * n_tiles + t, g)),
        out_shape=jax.ShapeDtypeStruct((n_batch * seq, Q_DIM), jnp.float32),
        scratch_shapes=[pltpu.VMEM((3, ATT_TILE, HEAD_ROWS), jnp.float32),
                        pltpu.VMEM((2 * n_cmp + LANE, HEAD_ROWS), jnp.float32)],
        compiler_params=pltpu.CompilerParams(
            dimension_semantics=("parallel", "parallel", "arbitrary"),
            vmem_limit_bytes=VMEM_LIMIT),
    )(rel_bias, proj, proj, kc, vc, proj, proj, proj, proj, _nsa_overlap_t(n_cmp))


NEW_ROWS = Q_PER_KV * DEC_SEQ


def _rank_select_rows(score, n_cand, top_k):
    lane = lax.broadcasted_iota(jnp.int32, score.shape, 1)
    cnt = jnp.zeros(score.shape, jnp.float32)
    for m in range(n_cand):
        col = score[:, m:m + 1]
        cnt = cnt + jnp.where(col > score, 1.0, jnp.where((col == score) & (lane > m), 1.0, 0.0))
    return cnt < top_k


def _stack_group_heads(q_ref, g):
    return jnp.concatenate(
        [q_ref[:, (g * Q_PER_KV + r) * HEAD_DIM:(g * Q_PER_KV + r + 1) * HEAD_DIM] for r in range(Q_PER_KV)],
        axis=0).astype(jnp.bfloat16)


def _group_bias(rb_ref, g, dist):
    return jnp.concatenate(_bias_from_dist(dist, _head_vals(rb_ref, g)), axis=0)


def _load_states(refs):
    m_ref, l_ref, acc_ref = refs
    return [(m_ref[g], l_ref[g], acc_ref[g]) for g in range(N_KV_HEADS)]


def _store_states(refs, states):
    m_ref, l_ref, acc_ref = refs
    for g, (m, l, acc) in enumerate(states):
        m_ref[g] = m
        l_ref[g] = l
        acc_ref[g] = acc


def _cache_step_bias(rb_ref, g, past_len, first_key, n_keys, near):
    if near:
        q_pos = past_len + lax.broadcasted_iota(jnp.int32, (DEC_SEQ, n_keys), 0)
        k_pos = first_key + lax.broadcasted_iota(jnp.int32, (DEC_SEQ, n_keys), 1)
        return _group_bias(rb_ref, g, q_pos - k_pos)
    return jnp.concatenate(
        [jnp.zeros((DEC_SEQ, 1), jnp.float32) + rb_ref[REL_BUCKETS - 1, g * Q_PER_KV + r]
         for r in range(Q_PER_KV)], axis=0)


def _when_near_or_far(past_len, first_key, n_keys, step):
    near = past_len - (first_key + n_keys - 1) < REL_MAX_DIST
    pl.when(near)(lambda: step(True))
    pl.when(jnp.logical_not(near))(lambda: step(False))


def _flash_update(refs, g, qs, k_tile, v_tile, bias, mask, first):
    m_ref, l_ref, acc_ref = refs
    if first:
        m, l, acc = _flash_first(qs, k_tile, v_tile, bias, mask)
    else:
        m, l, acc = _flash_next(qs, k_tile, v_tile, bias, mask, m_ref[g], l_ref[g], acc_ref[g])
    m_ref[g] = m
    l_ref[g] = l
    acc_ref[g] = acc


PAGES_PER_BLOCK = MOBA_BLOCK // PAGE_SIZE
KMEAN_PAGES = 8
MOBA_STEP_PAGES = 8
MOBA_STEP_KEYS = MOBA_STEP_PAGES * PAGE_SIZE


def _moba_kmean_kernel(pt_ref, *refs):
    o_ref = refs[KMEAN_PAGES]
    for i in range(KMEAN_PAGES // PAGES_PER_BLOCK):
        total = jnp.sum(refs[PAGES_PER_BLOCK * i][...], axis=0)
        for k in range(1, PAGES_PER_BLOCK):
            total = total + jnp.sum(refs[PAGES_PER_BLOCK * i + k][...], axis=0)
        o_ref[i] = total * (1.0 / MOBA_BLOCK)


def _page_spec(comp, which, per_step):
    return pl.BlockSpec((None, PAGE_SIZE, None, N_KV_HEADS, HEAD_DIM),
                        lambda b, n, pt: (pt[b, n * per_step + which], 0, comp, 0, 0))


MOBA_ROW_VECTORS = 2 * N_KV_HEADS
NSA_ROW_VECTORS = 4 * N_KV_HEADS


def _flat_pages(cache):
    return cache.reshape(-1, HEAD_DIM)


def _flat_page_spec(vectors_per_row, which, per_step):
    return pl.BlockSpec((PAGE_SIZE * vectors_per_row, HEAD_DIM),
                        lambda b, n, pt: (pt[b, n * per_step + which], 0))


def _page_vectors(ref, index, vectors_per_row):
    return ref[pl.ds(index, PAGE_SIZE, stride=vectors_per_row), :]


def _moba_sample_kmean(cache, page_table):
    n_seq, n_pages = page_table.shape
    n_blk = n_pages // PAGES_PER_BLOCK
    blk_per_step = KMEAN_PAGES // PAGES_PER_BLOCK
    return pl.pallas_call(
        _moba_kmean_kernel,
        grid_spec=pltpu.PrefetchScalarGridSpec(
            num_scalar_prefetch=1,
            grid=(n_seq, n_pages // KMEAN_PAGES),
            in_specs=[_page_spec(0, k, KMEAN_PAGES) for k in range(KMEAN_PAGES)],
            out_specs=pl.BlockSpec((None, blk_per_step, N_KV_HEADS, HEAD_DIM), lambda b, n, pt: (b, n, 0, 0)),
        ),
        out_shape=jax.ShapeDtypeStruct((n_seq, n_blk, N_KV_HEADS, HEAD_DIM), jnp.float32),
        compiler_params=pltpu.CompilerParams(dimension_semantics=("parallel", "arbitrary")),
    )(page_table, *([cache] * KMEAN_PAGES))


def _moba_sample_kernel(pt_ref, rb_ref, q_ref, kn_ref, vn_ref, kmean_ref, *refs, n_blk, n_steps, past_len):
    page_refs = refs[:MOBA_STEP_PAGES]
    o_ref, m_ref, l_ref, acc_ref, sel_ref = refs[MOBA_STEP_PAGES:]
    n = pl.program_id(1)
    refs = (m_ref, l_ref, acc_ref)
    tok = lax.broadcasted_iota(jnp.int32, (DEC_SEQ, DEC_SEQ), 0)
    new = lax.broadcasted_iota(jnp.int32, (DEC_SEQ, DEC_SEQ), 1)

    @pl.when(n == 0)
    def _():
        for g in range(N_KV_HEADS):
            qs = _stack_group_heads(q_ref, g)
            gate = lax.dot_general(qs, kmean_ref[g].astype(jnp.bfloat16), _NT, preferred_element_type=jnp.float32)
            gate = jnp.concatenate([gate, jnp.full((NEW_ROWS, LANE - n_blk), NEG_INF, jnp.float32)], axis=1)
            lane = lax.broadcasted_iota(jnp.int32, gate.shape, 1)
            chosen = _rank_select_rows(gate, n_blk, MOBA_TOPK) & (lane < n_blk)
            sel_ref[g] = jnp.where(chosen, 1.0, 0.0)
            _flash_update(refs, g, qs, kn_ref[:, g * HEAD_DIM:(g + 1) * HEAD_DIM].astype(jnp.bfloat16),
                          vn_ref[:, g * HEAD_DIM:(g + 1) * HEAD_DIM].astype(jnp.bfloat16),
                          _group_bias(rb_ref, g, tok - new), _tile4(new <= tok), True)

    pick = jnp.where(
        lax.broadcasted_iota(jnp.int32, (LANE, MOBA_STEP_KEYS), 0)
        == n * (MOBA_STEP_KEYS // MOBA_BLOCK) + lax.broadcasted_iota(jnp.int32, (LANE, MOBA_STEP_KEYS), 1) // MOBA_BLOCK,
        1.0, 0.0).astype(jnp.bfloat16)

    def step(near):
        old, new = _load_states(refs), []
        for g in range(N_KV_HEADS):
            qs = _stack_group_heads(q_ref, g)
            k_tile = jnp.concatenate(
                [_page_vectors(ref, g, MOBA_ROW_VECTORS) for ref in page_refs], axis=0).astype(jnp.bfloat16)
            v_tile = jnp.concatenate(
                [_page_vectors(ref, N_KV_HEADS + g, MOBA_ROW_VECTORS) for ref in page_refs],
                axis=0).astype(jnp.bfloat16)
            mask = jnp.dot(sel_ref[g].astype(jnp.bfloat16), pick, preferred_element_type=jnp.float32) > 0.5
            bias = _cache_step_bias(rb_ref, g, past_len, n * MOBA_STEP_KEYS, MOBA_STEP_KEYS, near)
            new.append(_flash_next(qs, k_tile, v_tile, bias, mask, *old[g]))
        _store_states(refs, new)

    _when_near_or_far(past_len, n * MOBA_STEP_KEYS, MOBA_STEP_KEYS, step)

    @pl.when(n == n_steps - 1)
    def _():
        for g in range(N_KV_HEADS):
            out = acc_ref[g] / jnp.maximum(l_ref[g], 1e-30)
            for r in range(Q_PER_KV):
                h = g * Q_PER_KV + r
                o_ref[:, h * HEAD_DIM:(h + 1) * HEAD_DIM] = out[r * DEC_SEQ:(r + 1) * DEC_SEQ]


def _moba_sample_attention(proj, kmean, cache, page_table, rel_bias, row_block0, past_len):
    n_seq, n_pages = page_table.shape
    n_blk = n_pages // PAGES_PER_BLOCK
    n_steps = n_pages // MOBA_STEP_PAGES
    kv_cols = KV_DIM
    return pl.pallas_call(
        functools.partial(_moba_sample_kernel, n_blk=n_blk, n_steps=n_steps, past_len=past_len),
        grid_spec=pltpu.PrefetchScalarGridSpec(
            num_scalar_prefetch=1,
            grid=(n_seq, n_steps),
            in_specs=[
                pl.BlockSpec(memory_space=pltpu.SMEM),
                pl.BlockSpec((DEC_SEQ, Q_DIM), lambda b, n, pt: (row_block0 + b, 0)),
                pl.BlockSpec((DEC_SEQ, kv_cols), lambda b, n, pt: (row_block0 + b, Q_DIM // kv_cols)),
                pl.BlockSpec((DEC_SEQ, kv_cols), lambda b, n, pt: (row_block0 + b, Q_DIM // kv_cols + 1)),
                pl.BlockSpec((None, N_KV_HEADS, n_blk, HEAD_DIM), lambda b, n, pt: (b, 0, 0, 0)),
            ] + [_flat_page_spec(MOBA_ROW_VECTORS, k, MOBA_STEP_PAGES) for k in range(MOBA_STEP_PAGES)],
            out_specs=pl.BlockSpec((DEC_SEQ, Q_DIM), lambda b, n, pt: (b, 0)),
            scratch_shapes=[
                pltpu.VMEM((N_KV_HEADS, NEW_ROWS, 1), jnp.float32),
                pltpu.VMEM((N_KV_HEADS, NEW_ROWS, 1), jnp.float32),
                pltpu.VMEM((N_KV_HEADS, NEW_ROWS, HEAD_DIM), jnp.float32),
                pltpu.VMEM((N_KV_HEADS, NEW_ROWS, LANE), jnp.float32),
            ],
        ),
        out_shape=jax.ShapeDtypeStruct((n_seq * DEC_SEQ, Q_DIM), jnp.float32),
        compiler_params=pltpu.CompilerParams(dimension_semantics=("parallel", "arbitrary")),
    )(page_table, rel_bias, proj, proj, proj, kmean, *([_flat_pages(cache)] * MOBA_STEP_PAGES))


CMP_PAGES = 8
SEL_STEP_PAGES = 8
SEL_STEP_KEYS = SEL_STEP_PAGES * PAGE_SIZE
CHUNKS_PER_PAGE = PAGE_SIZE // CMP_STRIDE


def _nsa_sample_compress_kernel(pt_ref, *refs):
    page_refs = refs[:CMP_PAGES]
    pe_ref, w_ref, top_ref, bot_ref = refs[CMP_PAGES:]
    rows = CMP_PAGES * CHUNKS_PER_PAGE
    for c in range(2):
        top = jnp.zeros((N_KV_HEADS * rows, HEAD_DIM), jnp.float32)
        bot = jnp.zeros((N_KV_HEADS * rows, HEAD_DIM), jnp.float32)
        for r in range(CMP_STRIDE):
            x = jnp.concatenate(
                [ref[pl.ds(r, CHUNKS_PER_PAGE, stride=CMP_STRIDE), c, g, :]
                 for g in range(N_KV_HEADS) for ref in page_refs], axis=0)
            xt = (x + pe_ref[c, r:r + 1, :]).astype(jnp.bfloat16)
            xb = (x + pe_ref[c, CMP_STRIDE + r:CMP_STRIDE + r + 1, :]).astype(jnp.bfloat16)
            top = top + jnp.dot(xt, w_ref[c, r * HEAD_DIM:(r + 1) * HEAD_DIM, :], preferred_element_type=jnp.float32)
            bot = bot + jnp.dot(xb, w_ref[c, (CMP_STRIDE + r) * HEAD_DIM:(CMP_STRIDE + r + 1) * HEAD_DIM, :],
                                preferred_element_type=jnp.float32)
        for g in range(N_KV_HEADS):
            top_ref[c, g] = top[g * rows:(g + 1) * rows]
            bot_ref[c, g] = bot[g * rows:(g + 1) * rows]


def _nsa_sample_compress(cache, page_table, cmp_pe, cmp_w):
    n_seq, n_pages = page_table.shape
    n_chunks = n_pages * CHUNKS_PER_PAGE
    rows = CMP_PAGES * CHUNKS_PER_PAGE

    def page_spec(k):
        return pl.BlockSpec((None, PAGE_SIZE, 2, N_KV_HEADS, HEAD_DIM),
                            lambda b, s, pt: (pt[b, s * CMP_PAGES + k], 0, 0, 0, 0))

    out = jax.ShapeDtypeStruct((n_seq, 2, N_KV_HEADS, n_chunks, HEAD_DIM), jnp.float32)
    out_spec = pl.BlockSpec((None, 2, N_KV_HEADS, rows, HEAD_DIM), lambda b, s, pt: (b, 0, 0, s, 0))
    return pl.pallas_call(
        _nsa_sample_compress_kernel,
        grid_spec=pltpu.PrefetchScalarGridSpec(
            num_scalar_prefetch=1,
            grid=(n_seq, n_pages // CMP_PAGES),
            in_specs=[page_spec(k) for k in range(CMP_PAGES)] + [
                pl.BlockSpec((2, CMP_BLOCK, HEAD_DIM), lambda b, s, pt: (0, 0, 0)),
                pl.BlockSpec((2, CMP_BLOCK * HEAD_DIM, HEAD_DIM), lambda b, s, pt: (0, 0, 0)),
            ],
            out_specs=[out_spec, out_spec],
        ),
        out_shape=[out, out],
        compiler_params=pltpu.CompilerParams(
            dimension_semantics=("parallel", "parallel"), vmem_limit_bytes=VMEM_LIMIT),
    )(page_table, *([cache] * CMP_PAGES), cmp_pe, cmp_w)


def _nsa_sample_kernel(pt_ref, rb_ref, q_ref, gate_ref, ksn_ref, vsn_ref, kwn_ref, vwn_ref, top_ref, bot_ref,
                       ov_ref, win_ref, *refs, n_pages, past_len):
    page_refs = refs[:SEL_STEP_PAGES]
    o_ref, m_ref, l_ref, acc_ref, sel_ref, ocmp_ref, owin_ref = refs[SEL_STEP_PAGES:]
    p = pl.program_id(1)
    refs = (m_ref, l_ref, acc_ref)
    n_chunks = n_pages * CHUNKS_PER_PAGE
    n_sel_pad = ov_ref.shape[1]
    n_sel = past_len // SEL_BLOCK + 1
    win_len = win_ref.shape[0]
    tok = lax.broadcasted_iota(jnp.int32, (DEC_SEQ, DEC_SEQ), 0)
    new = lax.broadcasted_iota(jnp.int32, (DEC_SEQ, DEC_SEQ), 1)

    def group_cols(ref, g):
        return ref[:, g * HEAD_DIM:(g + 1) * HEAD_DIM].astype(jnp.bfloat16)

    @pl.when(p == 0)
    def _():
        for g in range(N_KV_HEADS):
            qs = _stack_group_heads(q_ref, g)
            kc = top_ref[0, g] + pltpu.roll(bot_ref[0, g], n_chunks - 1, axis=0)
            vc = top_ref[1, g] + pltpu.roll(bot_ref[1, g], n_chunks - 1, axis=0)
            q_pos = past_len + lax.broadcasted_iota(jnp.int32, (DEC_SEQ, n_chunks), 0)
            tok_id = lax.broadcasted_iota(jnp.int32, (DEC_SEQ, n_chunks), 1)
            c_end = tok_id * CMP_STRIDE + (CMP_BLOCK - 1)
            ok_c = _tile4((tok_id < n_chunks - 1) & (c_end <= q_pos))
            lc = lax.dot_general(qs, kc.astype(jnp.bfloat16), _NT, preferred_element_type=jnp.float32) * SCALE
            lc = jnp.where(ok_c, lc + _group_bias(rb_ref, g, q_pos - c_end), NEG_INF)
            e = jnp.where(ok_c, jnp.exp(lc - jnp.max(lc, axis=-1, keepdims=True)), 0.0)
            pc = e / jnp.maximum(jnp.sum(e, axis=-1, keepdims=True), 1e-30)
            ocmp_ref[g] = jnp.dot(pc.astype(jnp.bfloat16), vc.astype(jnp.bfloat16), preferred_element_type=jnp.float32)
            p_sum = pc[0:DEC_SEQ]
            for r in range(1, Q_PER_KV):
                p_sum = p_sum + pc[r * DEC_SEQ:(r + 1) * DEC_SEQ]
            imp = jnp.dot(p_sum.astype(jnp.bfloat16), ov_ref[...], preferred_element_type=jnp.float32)
            blk = lax.broadcasted_iota(jnp.int32, (DEC_SEQ, n_sel_pad), 1)
            own = (past_len + lax.broadcasted_iota(jnp.int32, (DEC_SEQ, n_sel_pad), 0)) // SEL_BLOCK
            diff = own - blk
            valid = diff >= 0
            forced = valid & ((blk == 0) | (diff < N_LOCAL_SEL))
            score = jnp.where(forced, FORCE_SCORE, jnp.where(valid, imp, NEG_INF))
            chosen = _rank_select_rows(score, n_sel, SEL_TOPK) & valid
            sel_ref[g] = _tile4(jnp.where(chosen, 1.0, 0.0))
            causal = _tile4(new <= tok)
            near = _group_bias(rb_ref, g, tok - new)
            _flash_update(refs, g, qs, group_cols(ksn_ref, g), group_cols(vsn_ref, g), near, causal, True)
            mw, lw, aw = _flash_first(qs, group_cols(kwn_ref, g), group_cols(vwn_ref, g), near, causal)
            tok_w = lax.broadcasted_iota(jnp.int32, (DEC_SEQ, win_len), 0)
            dist_w = win_len + tok_w - lax.broadcasted_iota(jnp.int32, (DEC_SEQ, win_len), 1)
            mw, lw, aw = _flash_next(qs, win_ref[:, 0, g, :].astype(jnp.bfloat16), win_ref[:, 1, g, :].astype(jnp.bfloat16),
                                     _group_bias(rb_ref, g, dist_w), _tile4(dist_w < WINDOW), mw, lw, aw)
            owin_ref[g] = aw / jnp.maximum(lw, 1e-30)

    pick = jnp.where(
        lax.broadcasted_iota(jnp.int32, (n_sel_pad, SEL_STEP_KEYS), 0)
        == p * (SEL_STEP_KEYS // SEL_BLOCK) + lax.broadcasted_iota(jnp.int32, (n_sel_pad, SEL_STEP_KEYS), 1) // SEL_BLOCK,
        1.0, 0.0).astype(jnp.bfloat16)

    def step(near):
        old, new = _load_states(refs), []
        for g in range(N_KV_HEADS):
            qs = _stack_group_heads(q_ref, g)
            k_tile = jnp.concatenate(
                [_page_vectors(ref, 2 * N_KV_HEADS + g, NSA_ROW_VECTORS) for ref in page_refs],
                axis=0).astype(jnp.bfloat16)
            v_tile = jnp.concatenate(
                [_page_vectors(ref, 3 * N_KV_HEADS + g, NSA_ROW_VECTORS) for ref in page_refs],
                axis=0).astype(jnp.bfloat16)
            mask = jnp.dot(sel_ref[g].astype(jnp.bfloat16), pick, preferred_element_type=jnp.float32) > 0.5
            bias = _cache_step_bias(rb_ref, g, past_len, p * SEL_STEP_KEYS, SEL_STEP_KEYS, near)
            new.append(_flash_next(qs, k_tile, v_tile, bias, mask, *old[g]))
        _store_states(refs, new)

    _when_near_or_far(past_len, p * SEL_STEP_KEYS, SEL_STEP_KEYS, step)

    @pl.when(p == n_pages // SEL_STEP_PAGES - 1)
    def _():
        gates = jax.nn.sigmoid(gate_ref[...])
        for g in range(N_KV_HEADS):
            o_sel = acc_ref[g] / jnp.maximum(l_ref[g], 1e-30)
            for r in range(Q_PER_KV):
                rows = slice(r * DEC_SEQ, (r + 1) * DEC_SEQ)
                col = (g * Q_PER_KV + r) * 3
                out = (gates[:, col:col + 1] * ocmp_ref[g][rows] + gates[:, col + 1:col + 2] * o_sel[rows]
                       + gates[:, col + 2:col + 3] * owin_ref[g][rows])
                h = g * Q_PER_KV + r
                o_ref[:, h * HEAD_DIM:(h + 1) * HEAD_DIM] = out


def _nsa_sample_overlap(n_chunks, n_sel_pad):
    ci = np.arange(n_chunks)[:, None]
    sj = np.arange(n_sel_pad)[None, :]
    ov = (ci * CMP_STRIDE < (sj + 1) * SEL_BLOCK) & (ci * CMP_STRIDE + CMP_BLOCK > sj * SEL_BLOCK) & (ci < n_chunks - 1)
    return jnp.asarray(ov, jnp.bfloat16)


def _nsa_sample(proj, cache, win_buf, page_table, cmp_pe, cmp_w, rel_bias, row_block0, past_len):
    n_seq, n_pages = page_table.shape
    n_chunks = n_pages * CHUNKS_PER_PAGE
    n_sel_pad = -(-(past_len // SEL_BLOCK + 1) // LANE) * LANE
    top, bot = _nsa_sample_compress(cache, page_table, cmp_pe, cmp_w)
    kv0 = Q_DIM // KV_DIM
    gate_col = (Q_DIM + 6 * KV_DIM) // LANE

    def new_spec(comp):
        return pl.BlockSpec((DEC_SEQ, KV_DIM), lambda b, p, pt: (row_block0 + b, kv0 + comp))

    cmp_spec = pl.BlockSpec((None, 2, N_KV_HEADS, n_chunks, HEAD_DIM), lambda b, p, pt: (b, 0, 0, 0, 0))
    scratch = [
        pltpu.VMEM((N_KV_HEADS, NEW_ROWS, 1), jnp.float32),
        pltpu.VMEM((N_KV_HEADS, NEW_ROWS, 1), jnp.float32),
        pltpu.VMEM((N_KV_HEADS, NEW_ROWS, HEAD_DIM), jnp.float32),
        pltpu.VMEM((N_KV_HEADS, NEW_ROWS, n_sel_pad), jnp.float32),
        pltpu.VMEM((N_KV_HEADS, NEW_ROWS, HEAD_DIM), jnp.float32),
        pltpu.VMEM((N_KV_HEADS, NEW_ROWS, HEAD_DIM), jnp.float32),
    ]
    return pl.pallas_call(
        functools.partial(_nsa_sample_kernel, n_pages=n_pages, past_len=past_len),
        grid_spec=pltpu.PrefetchScalarGridSpec(
            num_scalar_prefetch=1,
            grid=(n_seq, n_pages // SEL_STEP_PAGES),
            in_specs=[
                pl.BlockSpec(memory_space=pltpu.SMEM),
                pl.BlockSpec((DEC_SEQ, Q_DIM), lambda b, p, pt: (row_block0 + b, 0)),
                pl.BlockSpec((DEC_SEQ, LANE), lambda b, p, pt: (row_block0 + b, gate_col)),
                new_spec(2), new_spec(3), new_spec(4), new_spec(5),
                cmp_spec, cmp_spec,
                pl.BlockSpec((n_chunks, n_sel_pad), lambda b, p, pt: (0, 0)),
                pl.BlockSpec((None,) + win_buf.shape[1:], lambda b, p, pt: (b, 0, 0, 0, 0)),
            ] + [_flat_page_spec(NSA_ROW_VECTORS, k, SEL_STEP_PAGES) for k in range(SEL_STEP_PAGES)],
            out_specs=pl.BlockSpec((DEC_SEQ, Q_DIM), lambda b, p, pt: (b, 0)),
            scratch_shapes=scratch,
        ),
        out_shape=jax.ShapeDtypeStruct((n_seq * DEC_SEQ, Q_DIM), jnp.float32),
        compiler_params=pltpu.CompilerParams(
            dimension_semantics=("parallel", "arbitrary"), vmem_limit_bytes=VMEM_LIMIT),
    )(page_table, rel_bias, proj, proj, proj, proj, proj, proj, top, bot,
      _nsa_sample_overlap(n_chunks, n_sel_pad), win_buf, *([_flat_pages(cache)] * SEL_STEP_PAGES))


SSD_PAIR = 2 * SSD_HEAD_DIM
GROUP_COLS = HEADS_PER_GROUP * SSD_HEAD_DIM
CARRY_ROWS = 8


def _pad_rows(x, rows):
    if x.shape[0] == rows:
        return x
    return jnp.concatenate([x, jnp.zeros((rows - x.shape[0],) + x.shape[1:], x.dtype)], axis=0)


def _ssd_kernel(z_ref, x_ref, b_ref, c_ref, dt_ref, prev_ref, s0_ref, cw_ref, cb_ref, dtb_ref, alog_ref, dsk_ref,
                nw_ref, y_ref, conv_ref, state_ref, carry_ref, s_ref, act_ref, *, rows_in, n_chunks):
    c = pl.program_id(1)
    L = SSD_CHUNK

    @pl.when(c == 0)
    def _():
        carry_ref[...] = prev_ref[...]
        s_ref[...] = s0_ref[...]

    row8 = lax.broadcasted_iota(jnp.int32, (CARRY_ROWS, 1), 0)
    off = 0
    for ref in (x_ref, b_ref, c_ref):
        width = ref.shape[1]
        cols = slice(off, off + width)
        raw_in = ref[...]
        raw = _pad_rows(raw_in, L)
        prev = carry_ref[:, cols]
        acc = cb_ref[:, cols] + cw_ref[CONV_W - 1:CONV_W, cols] * raw
        for k in range(1, CONV_W):
            rolled = pltpu.roll(raw, k, axis=0)
            head = jnp.where(row8 < k, pltpu.roll(prev, k, axis=0), rolled[:CARRY_ROWS])
            shifted = jnp.concatenate([head, rolled[CARRY_ROWS:]], axis=0)
            acc = acc + cw_ref[CONV_W - 1 - k:CONV_W - k, cols] * shifted
        act_ref[:, cols] = acc * jax.nn.sigmoid(acc)
        carry_ref[:, cols] = raw_in[rows_in - CARRY_ROWS:rows_in]

        @pl.when(c == n_chunks - 1)
        def _():
            conv_ref[:, cols] = raw_in[rows_in - (CONV_W - 1):rows_in]

        off += width

    row = lax.broadcasted_iota(jnp.int32, (L, LANE), 0)
    pre = _pad_rows(dt_ref[...], L) + dtb_ref[...]
    dt = jnp.maximum(pre, 0.0) + jnp.log1p(jnp.exp(-jnp.abs(pre)))
    dt = jnp.where(row < rows_in, dt, 0.0)
    acum = dt * (-jnp.exp(alog_ref[...]))
    shift = 1
    while shift < L:
        acum = acum + jnp.where(row >= shift, pltpu.roll(acum, shift, axis=0), 0.0)
        shift *= 2
    acum_t = acum.T
    exp_a = jnp.exp(acum)
    to_end = jnp.exp(acum[L - 1:L, :] - acum)
    li = lax.broadcasted_iota(jnp.int32, (L, L), 0)
    si = lax.broadcasted_iota(jnp.int32, (L, L), 1)
    causal = li >= si
    first_half = lax.broadcasted_iota(jnp.int32, (L, SSD_PAIR), 1) < SSD_HEAD_DIM

    def pair_cols(v, h):
        return jnp.where(first_half, v[:, h:h + 1], v[:, h + 1:h + 2])

    for g in range(SSD_GROUPS):
        bg = act_ref[:, D_INNER + g * D_STATE:D_INNER + (g + 1) * D_STATE].astype(jnp.bfloat16)
        cg = act_ref[:, D_INNER + GN + g * D_STATE:D_INNER + GN + (g + 1) * D_STATE].astype(jnp.bfloat16)
        cb = lax.dot_general(cg, bg, _NT, preferred_element_type=jnp.float32)
        s_prev = s_ref[g * GROUP_COLS:(g + 1) * GROUP_COLS, :]
        y_off = lax.dot_general(cg, s_prev.astype(jnp.bfloat16), _NT, preferred_element_type=jnp.float32)
        xw_parts = []
        for pr in range(HEADS_PER_GROUP // 2):
            h = g * HEADS_PER_GROUP + 2 * pr
            cols = slice(h * SSD_HEAD_DIM, (h + 2) * SSD_HEAD_DIM)
            xa = act_ref[:, cols]
            xdt = xa * pair_cols(dt, h)
            y = y_off[:, 2 * pr * SSD_HEAD_DIM:(2 * pr + 2) * SSD_HEAD_DIM] * pair_cols(exp_a, h)
            for half in range(2):
                hh = h + half
                decay = jnp.where(causal, jnp.exp(acum[:, hh:hh + 1] - acum_t[hh:hh + 1, :]), 0.0)
                mine = first_half if half == 0 else jnp.logical_not(first_half)
                y = y + jnp.dot((cb * decay).astype(jnp.bfloat16), jnp.where(mine, xdt, 0.0).astype(jnp.bfloat16),
                                preferred_element_type=jnp.float32)
            zz = z_ref[:, cols]
            y = y[:rows_in] + xa[:rows_in] * dsk_ref[:, cols]
            y_ref[:, cols] = y * (zz * jax.nn.sigmoid(zz))
            xw_parts.append(xdt * pair_cols(to_end, h))
        xw = jnp.concatenate(xw_parts, axis=1)
        chunk_state = jnp.dot(xw.T.astype(jnp.bfloat16), bg, preferred_element_type=jnp.float32)
        for r in range(HEADS_PER_GROUP):
            h = g * HEADS_PER_GROUP + r
            rows = slice(h * SSD_HEAD_DIM, (h + 1) * SSD_HEAD_DIM)
            s_ref[rows, :] = (s_ref[rows, :] * jnp.exp(acum_t[h:h + 1, L - 1:L])
                              + chunk_state[r * SSD_HEAD_DIM:(r + 1) * SSD_HEAD_DIM])
        gcols = slice(g * GROUP_COLS, (g + 1) * GROUP_COLS)
        yg = y_ref[:, gcols]
        y_ref[:, gcols] = yg * lax.rsqrt(jnp.mean(yg * yg, axis=-1, keepdims=True) + RMS_EPS) * nw_ref[:, gcols]

    @pl.when(c == n_chunks - 1)
    def _():
        state_ref[...] = s_ref[...]


def _ssd_mix(proj, conv_prev, ssm_prev, conv_w, conv_b, dt_bias, a_log, d_skip, norm_w, row_block0, rows_in, n_chunks):
    n_seq = conv_prev.shape[0]

    def rows_spec(width, col_block):
        return pl.BlockSpec((rows_in, width), lambda b, c: (row_block0 + b * n_chunks + c, col_block))

    def const_spec(shape):
        return pl.BlockSpec(shape, lambda b, c: (0,) * len(shape))

    pad_lanes = lambda v: jnp.pad(v, (0, LANE - v.shape[0])).reshape(1, LANE)
    return pl.pallas_call(
        functools.partial(_ssd_kernel, rows_in=rows_in, n_chunks=n_chunks),
        grid=(n_seq, n_chunks),
        in_specs=[
            rows_spec(D_INNER, 0),
            rows_spec(D_INNER, 1),
            rows_spec(GN, 2 * D_INNER // GN),
            rows_spec(GN, 2 * D_INNER // GN + 1),
            rows_spec(LANE, (D_INNER + CONV_DIM) // LANE),
            pl.BlockSpec((None, CARRY_ROWS, CONV_DIM), lambda b, c: (b, 0, 0)),
            pl.BlockSpec((None, D_INNER, D_STATE), lambda b, c: (b, 0, 0)),
            const_spec((CONV_W, CONV_DIM)),
            const_spec((1, CONV_DIM)),
            const_spec((1, LANE)),
            const_spec((1, LANE)),
            const_spec((1, D_INNER)),
            const_spec((1, D_INNER)),
        ],
        out_specs=[
            pl.BlockSpec((rows_in, D_INNER), lambda b, c: (b * n_chunks + c, 0)),
            pl.BlockSpec((None, CONV_W - 1, CONV_DIM), lambda b, c: (b, 0, 0)),
            pl.BlockSpec((None, D_INNER, D_STATE), lambda b, c: (b, 0, 0)),
        ],
        out_shape=[
            jax.ShapeDtypeStruct((n_seq * n_chunks * rows_in, D_INNER), jnp.float32),
            jax.ShapeDtypeStruct((n_seq, CONV_W - 1, CONV_DIM), jnp.float32),
            jax.ShapeDtypeStruct((n_seq, D_INNER, D_STATE), jnp.float32),
        ],
        scratch_shapes=[
            pltpu.VMEM((CARRY_ROWS, CONV_DIM), jnp.float32),
            pltpu.VMEM((D_INNER, D_STATE), jnp.float32),
            pltpu.VMEM((SSD_CHUNK, CONV_DIM), jnp.float32),
        ],
        compiler_params=pltpu.CompilerParams(
            dimension_semantics=("parallel", "arbitrary"), vmem_limit_bytes=VMEM_LIMIT),
    )(proj, proj, proj, proj, proj, conv_prev, ssm_prev, conv_w, conv_b.reshape(1, CONV_DIM),
      pad_lanes(dt_bias), pad_lanes(a_log), jnp.repeat(d_skip, SSD_HEAD_DIM).reshape(1, D_INNER),
      norm_w.reshape(1, D_INNER))


def _pad_cols(w, n):
    return jnp.pad(w, ((0, 0), (0, n - w.shape[1])))


def _split_rows(a):
    return (a[:M_PROMPT].reshape(BATCH, SEQ, a.shape[-1]),
            a[M_PROMPT:].reshape(DEC_BATCH, DEC_SEQ, a.shape[-1]))


def _join_rows(p, s):
    return jnp.concatenate([p.reshape(M_PROMPT, p.shape[-1]), s.reshape(M_SAMPLE, s.shape[-1])], axis=0)


def kernel(x_prompt, x_sample, state_ssm, state_conv, cache_nsa_kv, cache_nsa_win, cache_moba_kv, page_table, rel_bias, norm_w, final_norm_w, ffn_w_gate, ffn_w_up, ffn_w_down, ssd_w_in, ssd_conv_w, ssd_conv_b, ssd_dt_bias, ssd_a_log, ssd_d, ssd_norm_w, ssd_w_out, nsa_w_in, nsa_cmp_pe, nsa_cmp_w, nsa_w_out, moba_w_in, moba_w_out):
    bf = jnp.bfloat16
    h = _join_rows(x_prompt, x_sample)
    ssm_p, ssm_s, conv_p, conv_s = [], [], [], []
    nsa_kv_p, nsa_kv_s, nsa_win_p, nsa_win_s = [], [], [], []
    moba_kv_p, moba_kv_s = [], []
    for i in range(DEPTH):
        kind, j = i % N_MIXERS, i // N_MIXERS
        h = _ffn_half(h, norm_w[i, 0], ffn_w_gate[i, 0].astype(bf), ffn_w_up[i, 0].astype(bf), ffn_w_down[i, 0].astype(bf))
        if kind == 0:
            n_pad = -(-SSD_IN_DIM // 512) * 512
            proj = _norm_proj(h, norm_w[i, 1], _pad_cols(ssd_w_in[j], n_pad).astype(bf))
            w = (ssd_conv_w[j], ssd_conv_b[j], ssd_dt_bias[j], ssd_a_log[j], ssd_d[j], ssd_norm_w[j])
            conv0 = jnp.zeros((BATCH, CARRY_ROWS, CONV_DIM), jnp.float32)
            ssm0 = jnp.zeros((BATCH, D_INNER, D_STATE), jnp.float32)
            y_p, c_p, s_p = _ssd_mix(proj, conv0, ssm0, *w, 0, SSD_CHUNK, SEQ // SSD_CHUNK)
            conv1 = jnp.pad(state_conv[j], ((0, 0), (CARRY_ROWS - (CONV_W - 1), 0), (0, 0)))
            ssm1 = state_ssm[j].reshape(DEC_BATCH, D_INNER, D_STATE)
            y_s, c_s, s_s = _ssd_mix(proj, conv1, ssm1, *w, M_PROMPT // DEC_SEQ, DEC_SEQ, 1)
            conv_p.append(c_p)
            conv_s.append(c_s)
            ssm_p.append(s_p.reshape(BATCH, SSD_HEADS, SSD_HEAD_DIM, D_STATE))
            ssm_s.append(s_s.reshape(DEC_BATCH, SSD_HEADS, SSD_HEAD_DIM, D_STATE))
            w_out = ssd_w_out[j]
        elif kind == 1:
            n_pad = -(-NSA_IN_DIM // 512) * 512
            proj = _norm_proj(h, norm_w[i, 1], _pad_cols(nsa_w_in[j], n_pad).astype(bf))
            proj_p, proj_s = _split_rows(proj)
            kc, vc = _nsa_compress_prompt(proj, nsa_cmp_pe[j], nsa_cmp_w[j].astype(bf), BATCH, SEQ)
            y_p = _nsa_attention(proj, kc, vc, rel_bias, BATCH, SEQ)
            kv_all = proj_p[..., Q_DIM:Q_DIM + 6 * KV_DIM].reshape(BATCH, SEQ, 6, N_KV_HEADS, HEAD_DIM)
            kv_p, win_p = kv_all[:, :, :4], kv_all[:, SEQ - WINDOW:, 4:]
            cmp_w_bf = nsa_cmp_w[j].astype(bf)
            y_s = _nsa_sample(proj, cache_nsa_kv[j], cache_nsa_win[j], page_table, nsa_cmp_pe[j], cmp_w_bf,
                              rel_bias, M_PROMPT // DEC_SEQ, PAST_LEN)
            kv_new = proj_s[..., Q_DIM:Q_DIM + 6 * KV_DIM].reshape(DEC_BATCH, DEC_SEQ, 6, N_KV_HEADS, HEAD_DIM)
            kv_s = kv_new[:, :, :4]
            win_s = jnp.concatenate([cache_nsa_win[j][:, DEC_SEQ:], kv_new[:, :, 4:]], axis=1)
            nsa_kv_p.append(kv_p)
            nsa_kv_s.append(kv_s)
            nsa_win_p.append(win_p)
            nsa_win_s.append(win_s)
            w_out = nsa_w_out[j]
        else:
            proj = _norm_proj(h, norm_w[i, 1], moba_w_in[j].astype(bf))
            proj_p, proj_s = _split_rows(proj)
            y_p = _moba_attention(proj, rel_bias, BATCH, SEQ)
            kv_p = proj_p[..., Q_DIM:].reshape(BATCH, SEQ, 2, N_KV_HEADS, HEAD_DIM)
            kmean = _moba_sample_kmean(cache_moba_kv[j], page_table)
            y_s = _moba_sample_attention(proj, jnp.swapaxes(kmean, 1, 2), cache_moba_kv[j], page_table, rel_bias,
                                         M_PROMPT // DEC_SEQ, PAST_LEN)
            kv_s = proj_s[..., Q_DIM:].reshape(DEC_BATCH, DEC_SEQ, 2, N_KV_HEADS, HEAD_DIM)
            moba_kv_p.append(kv_p)
            moba_kv_s.append(kv_s)
            w_out = moba_w_out[j]
        h = _proj_residual(_join_rows(y_p, y_s), w_out.astype(bf), h)
        h = _ffn_half(h, norm_w[i, 2], ffn_w_gate[i, 1].astype(bf), ffn_w_up[i, 1].astype(bf), ffn_w_down[i, 1].astype(bf))
    y = _final_norm(h, final_norm_w)
    y_prompt, y_sample = _split_rows(y)
    return (y_prompt, y_sample,
            jnp.stack(ssm_p), jnp.stack(ssm_s), jnp.stack(conv_p), jnp.stack(conv_s),
            jnp.stack(nsa_kv_p), jnp.stack(nsa_kv_s), jnp.stack(nsa_win_p), jnp.stack(nsa_win_s),
            jnp.stack(moba_kv_p), jnp.stack(moba_kv_s))
```
